```python
import math
import numpy as np
import jax
import jax.numpy as jnp
from jax import lax

D_MODEL = 1024
BATCH = 16
SEQ = 2048
DEPTH = 4

HEAD_DIM = 64
Q_BLOCK = 128
EPS = 1e-6
NEG = -1e30
FORCE_SCORE = 1e4

SB_HEADS = 8
SB_W = SB_HEADS * HEAD_DIM
NSA_HEADS = 8
NSA_GROUPS = 2
NSA_HPG = NSA_HEADS // NSA_GROUPS
NSA_W = NSA_HEADS * HEAD_DIM
NSA_KV_W = NSA_GROUPS * HEAD_DIM
CMP_LEN = 32
CMP_STRIDE = 16
SEL_BLOCK = 64
SEL_TOPN = 4
WINDOW = 256
HGRN_HEADS = 4
HGRN_DK = 128
HGRN_DV = 128
HGRN_K_W = HGRN_HEADS * HGRN_DK
HGRN_V_W = HGRN_HEADS * HGRN_DV
HGRN_CHUNK = 64
N_BRANCH = 3
IN_SPLITS = (SB_W, SB_W, SB_W, NSA_W, 6 * NSA_KV_W, 3 * NSA_HEADS, HGRN_K_W, HGRN_K_W, HGRN_V_W, HGRN_V_W, N_BRANCH * D_MODEL)
IN_WIDTH = sum(IN_SPLITS)
PEER_HEADS = 8
PEER_NKEYS = 128
PEER_EXPERTS = PEER_NKEYS * PEER_NKEYS
PEER_TOPK = 16
PEER_QDIM = 128
PEER_TOK_CHUNK = 128

kernel_name = "hybrid_sb_nsa_hgrn2_peer"


def rms_norm(x, g):
    xf = x.astype(jnp.float32)
    y = xf * lax.rsqrt(jnp.mean(xf * xf, axis=-1, keepdims=True) + EPS)
    return (y * g.astype(jnp.float32)).astype(x.dtype)


def alibi_slopes(n_heads):
    return jnp.exp2(-8.0 * (jnp.arange(n_heads, dtype=jnp.float32) + 1.0) / n_heads)


def stick_breaking_attention(q, k, v):
    B, H, T, dh = q.shape
    scale = dh ** -0.5
    outs = []
    for i in range(T // Q_BLOCK):
        q0 = i * Q_BLOCK
        kl = q0 + Q_BLOCK
        z = jnp.einsum("bhtd,bhsd->bhts", q[:, :, q0:kl], k[:, :, :kl]).astype(jnp.float32) * scale
        before = jnp.arange(kl)[None, :] < (q0 + jnp.arange(Q_BLOCK))[:, None]
        log_1m = jnp.where(before, jax.nn.log_sigmoid(-z), 0.0)
        after = lax.cumsum(log_1m, axis=3, reverse=True) - log_1m
        a = jnp.where(before, jnp.exp(jax.nn.log_sigmoid(z) + after), 0.0)
        outs.append(jnp.einsum("bhts,bhsd->bhtd", a.astype(v.dtype), v[:, :, :kl]))
    return jnp.concatenate(outs, axis=2)


def nsa_attention(q, k_cmp, v_cmp, k_sel, v_sel, k_win, v_win, gates, w_cmp_k, w_cmp_v, cmp_pe):
    B, G, P, T, dh = q.shape
    f32 = jnp.float32
    scale = dh ** -0.5
    slopes = alibi_slopes(G * P).reshape(G, P)
    t_pos = jnp.arange(T)

    n_piece = T // CMP_STRIDE
    per = CMP_LEN // CMP_STRIDE
    n_cmp = n_piece - per + 1

    def compress(x, w):
        pieces = x.reshape(B, G, n_piece, CMP_STRIDE, dh)
        blocks = jnp.concatenate([pieces[:, :, j:j + n_cmp] for j in range(per)], axis=3) + cmp_pe
        return blocks.reshape(B, G, n_cmp, CMP_LEN * dh) @ w

    kc = compress(k_cmp, w_cmp_k)
    vc = compress(v_cmp, w_cmp_v)
    c_start = jnp.arange(n_cmp) * CMP_STRIDE
    dist_c = t_pos[:, None] - (c_start + CMP_LEN - 1)[None, :]
    valid_c = dist_c >= 0
    s_c = jnp.einsum("bgptd,bgcd->bgptc", q, kc).astype(f32) * scale - slopes[:, :, None, None] * dist_c.astype(f32)
    p_c = jnp.where(valid_c, jax.nn.softmax(jnp.where(valid_c, s_c, NEG), axis=-1), 0.0)
    o_c = jnp.einsum("bgptc,bgcd->bgptd", p_c.astype(vc.dtype), vc)

    n_blk = T // SEL_BLOCK
    n_sel = min(SEL_TOPN, n_blk)
    b_start = jnp.arange(n_blk) * SEL_BLOCK
    overlap = ((c_start[:, None] < b_start[None, :] + SEL_BLOCK) & (c_start[:, None] + CMP_LEN > b_start[None, :])).astype(f32)
    imp = jnp.einsum("bgptc,cn->bgtn", p_c, overlap)
    bid = jnp.arange(n_blk)
    forced = (bid[None, :] == (t_pos // SEL_BLOCK)[:, None]) | (bid[None, :] == 0)
    causal_b = b_start[None, :] <= t_pos[:, None]
    imp = jnp.where(forced, FORCE_SCORE, jnp.where(causal_b, imp, NEG))
    _, sel_idx = lax.top_k(imp, n_sel)

    kblk = k_sel.reshape(B, G, n_blk, SEL_BLOCK, dh)
    vblk = v_sel.reshape(B, G, n_blk, SEL_BLOCK, dh)
    nq = T // Q_BLOCK
    q_chunks = jnp.moveaxis(q.reshape(B, G, P, nq, Q_BLOCK, dh), 3, 0)
    idx_chunks = jnp.moveaxis(sel_idx.reshape(B, G, nq, Q_BLOCK, n_sel), 2, 0)
    bi = jnp.arange(B)[:, None, None, None]
    gi = jnp.arange(G)[None, :, None, None]
    n_keys_sel = n_sel * SEL_BLOCK

    def sel_chunk(args):
        qc, ic, c = args
        kg = kblk[bi, gi, ic].reshape(B, G, Q_BLOCK, n_keys_sel, dh)
        vg = vblk[bi, gi, ic].reshape(B, G, Q_BLOCK, n_keys_sel, dh)
        pos = (ic[..., None] * SEL_BLOCK + jnp.arange(SEL_BLOCK)).reshape(B, G, Q_BLOCK, n_keys_sel)
        dist = (c * Q_BLOCK + jnp.arange(Q_BLOCK))[:, None] - pos
        sc = jnp.einsum("bgpqd,bgqsd->bgpqs", qc, kg).astype(f32) * scale - slopes[None, :, :, None, None] * dist[:, :, None].astype(f32)
        pr = jax.nn.softmax(jnp.where(dist[:, :, None] >= 0, sc, NEG), axis=-1)
        return jnp.einsum("bgpqs,bgqsd->bgpqd", pr.astype(vg.dtype), vg)

    o_s = lax.map(sel_chunk, (q_chunks, idx_chunks, jnp.arange(nq)))
    o_s = jnp.moveaxis(o_s, 0, 3).reshape(B, G, P, T, dh)

    nb = T // Q_BLOCK
    nwb = WINDOW // Q_BLOCK
    span = (nwb + 1) * Q_BLOCK

    def band(x):
        xb = jnp.pad(x.reshape(B, G, nb, Q_BLOCK, dh), ((0, 0), (0, 0), (nwb, 0), (0, 0), (0, 0)))
        return jnp.concatenate([xb[:, :, j:j + nb] for j in range(nwb + 1)], axis=3)

    kwb = band(k_win)
    vwb = band(v_win)
    tq = t_pos.reshape(nb, Q_BLOCK)
    sk = jnp.arange(nb)[:, None] * Q_BLOCK - nwb * Q_BLOCK + jnp.arange(span)[None, :]
    dist_w = tq[:, :, None] - sk[:, None, :]
    valid_w = (dist_w >= 0) & (dist_w < WINDOW) & (sk[:, None, :] >= 0)
    s_w = jnp.einsum("bgpnqd,bgnsd->bgpnqs", q.reshape(B, G, P, nb, Q_BLOCK, dh), kwb).astype(f32) * scale - slopes[:, :, None, None, None] * dist_w.astype(f32)
    p_w = jax.nn.softmax(jnp.where(valid_w, s_w, NEG), axis=-1)
    o_w = jnp.einsum("bgpnqs,bgnsd->bgpnqd", p_w.astype(vwb.dtype), vwb).reshape(B, G, P, T, dh)

    g = jax.nn.sigmoid(gates.astype(f32)).reshape(B, T, G, P, 3).transpose(4, 0, 2, 3, 1)
    o = g[0][..., None] * o_c + g[1][..., None] * o_s + g[2][..., None] * o_w
    return o.astype(q.dtype)


def hgrn2_recurrence(q, log_f, k, i):
    B, H, T, dk = q.shape
    dv = i.shape[-1]
    C = HGRN_CHUNK
    nc = T // C

    def to_chunks(x):
        return jnp.moveaxis(x.reshape(B, H, nc, C, x.shape[-1]), 2, 0)

    causal = jnp.tril(jnp.ones((C, C), dtype=bool))

    def step(S, inp):
        qc, gc, kc, ic = inp
        b = jnp.cumsum(gc, axis=2)
        o_inter = jnp.einsum("bhtk,bhkv->bhtv", qc * jnp.exp(b), S)
        rel = b[:, :, :, None, :] - b[:, :, None, :, :]
        decay = jnp.exp(jnp.where(causal[:, :, None], rel, -jnp.inf))
        att = jnp.einsum("bhtk,bhsk,bhtsk->bhts", qc, kc, decay)
        o = o_inter + jnp.einsum("bhts,bhsv->bhtv", att, ic)
        b_last = b[:, :, -1:, :]
        S_new = jnp.exp(b_last[:, :, 0, :, None]) * S + jnp.einsum("bhsk,bhsv->bhkv", kc * jnp.exp(b_last - b), ic)
        return S_new, o

    S0 = jnp.zeros((B, H, dk, dv), dtype=jnp.float32)
    _, o = lax.scan(step, S0, (to_chunks(q), to_chunks(log_f), to_chunks(k), to_chunks(i)))
    return jnp.moveaxis(o, 0, 2).reshape(B, H, T, dv)


def hybrid_mixer(h, w_in, w_cmp_k, w_cmp_v, cmp_pe, hgrn_norm, lower, w_sb, w_nsa, w_hg, w_out):
    B, T, _ = h.shape
    f32 = jnp.float32
    proj = h @ w_in
    offsets = np.cumsum(IN_SPLITS)[:-1].tolist()
    sb_q, sb_k, sb_v, nsa_q, nsa_kv, nsa_g, hg_q, hg_f, hg_i, hg_g, merge_g = jnp.split(proj, offsets, axis=-1)

    def heads(t, n):
        return t.reshape(B, T, n, -1).transpose(0, 2, 1, 3)

    o_sb = stick_breaking_attention(heads(sb_q, SB_HEADS), heads(sb_k, SB_HEADS), heads(sb_v, SB_HEADS))
    o_sb = o_sb.transpose(0, 2, 1, 3).reshape(B, T, SB_W)

    qn = heads(nsa_q, NSA_HEADS).reshape(B, NSA_GROUPS, NSA_HPG, T, HEAD_DIM)
    kv = nsa_kv.reshape(B, T, 6, NSA_GROUPS, HEAD_DIM).transpose(2, 0, 3, 1, 4)
    o_nsa = nsa_attention(qn, kv[0], kv[1], kv[2], kv[3], kv[4], kv[5], nsa_g.reshape(B, T, NSA_HEADS, 3), w_cmp_k, w_cmp_v, cmp_pe)
    o_nsa = o_nsa.reshape(B, NSA_HEADS, T, HEAD_DIM).transpose(0, 2, 1, 3).reshape(B, T, NSA_W)

    lb = lower.reshape(HGRN_HEADS, 1, HGRN_DK)
    f = lb + (1.0 - lb) * jax.nn.sigmoid(heads(hg_f, HGRN_HEADS).astype(f32))
    qh = jax.nn.silu(heads(hg_q, HGRN_HEADS).astype(f32))
    o_hg = hgrn2_recurrence(qh, jnp.log(f), 1.0 - f, heads(hg_i, HGRN_HEADS).astype(f32))
    o_hg = rms_norm(o_hg.transpose(0, 2, 1, 3), hgrn_norm.reshape(HGRN_HEADS, HGRN_DV))
    o_hg = (o_hg * jax.nn.silu(hg_g.astype(f32).reshape(B, T, HGRN_HEADS, HGRN_DV))).reshape(B, T, HGRN_V_W).astype(h.dtype)

    gm = jax.nn.sigmoid(merge_g.astype(f32)).reshape(B, T, N_BRANCH, D_MODEL)
    m = gm[:, :, 0] * (o_sb @ w_sb) + gm[:, :, 1] * (o_nsa @ w_nsa) + gm[:, :, 2] * (o_hg @ w_hg)
    return m.astype(h.dtype) @ w_out


def peer_ffn(x, w_q, sub_keys, u_tab, v_tab):
    B, T, D = x.shape
    f32 = jnp.float32
    Ct = PEER_TOK_CHUNK
    H, K = PEER_HEADS, PEER_TOPK
    xt = x.reshape((B * T) // Ct, Ct, D)

    def chunk(xc):
        qh = (xc @ w_q).reshape(Ct, H, 2, PEER_QDIM // 2)
        s = jnp.einsum("nhad,akd->nhak", qh, sub_keys).astype(f32)
        top_s, top_i = lax.top_k(s, K)
        cand = (top_s[:, :, 0, :, None] + top_s[:, :, 1, None, :]).reshape(Ct, H, K * K)
        cand_idx = (top_i[:, :, 0, :, None] * PEER_NKEYS + top_i[:, :, 1, None, :]).reshape(Ct, H, K * K)
        best_s, pos = lax.top_k(cand, K)
        eidx = jnp.take_along_axis(cand_idx, pos, axis=-1)
        gate = jax.nn.softmax(best_s, axis=-1)
        ug = u_tab[eidx]
        vg = v_tab[eidx]
        hpre = jnp.einsum("nd,nhkd->nhk", xc, ug).astype(f32)
        act = (gate * jax.nn.gelu(hpre, approximate=False)).astype(x.dtype)
        return jnp.einsum("nhk,nhkd->nd", act, vg)

    return lax.map(chunk, xt).reshape(B, T, D)


def setup_inputs(seed: int = 0) -> dict:
    key = jax.random.key(seed)
    ks = jax.random.split(key, 20)
    f32 = jnp.float32
    nrm = lambda k, shape, s: jax.random.normal(k, shape, dtype=f32) * s
    L, D = DEPTH, D_MODEL
    return {
        "x": nrm(ks[0], (BATCH, SEQ, D), 1.0),
        "norm_mix": 1.0 + nrm(ks[1], (L, D), 0.02),
        "norm_ffn": 1.0 + nrm(ks[2], (L, D), 0.02),
        "w_in": nrm(ks[3], (L, D, IN_WIDTH), D ** -0.5),
        "nsa_w_cmp_k": nrm(ks[4], (L, CMP_LEN * HEAD_DIM, HEAD_DIM), (CMP_LEN * HEAD_DIM) ** -0.5),
        "nsa_w_cmp_v": nrm(ks[5], (L, CMP_LEN * HEAD_DIM, HEAD_DIM), (CMP_LEN * HEAD_DIM) ** -0.5),
        "nsa_cmp_pe": nrm(ks[6], (L, CMP_LEN, HEAD_DIM), 0.1),
        "hgrn_norm": 1.0 + nrm(ks[7], (L, HGRN_V_W), 0.02),
        "hgrn_lower_bounds": nrm(ks[8], (L, HGRN_K_W), 0.1),
        "w_branch_sb": nrm(ks[9], (L, SB_W, D), SB_W ** -0.5),
        "w_branch_nsa": nrm(ks[10], (L, NSA_W, D), NSA_W ** -0.5),
        "w_branch_hgrn": nrm(ks[11], (L, HGRN_V_W, D), HGRN_V_W ** -0.5),
        "w_out": nrm(ks[12], (L, D, D), 0.5 * D ** -0.5),
        "peer_w_q": nrm(ks[13], (L, D, PEER_HEADS * PEER_QDIM), D ** -0.5),
        "peer_sub_keys": nrm(ks[14], (L, 2, PEER_NKEYS, PEER_QDIM // 2), (PEER_QDIM // 2) ** -0.5),
        "peer_u": nrm(ks[15], (L, PEER_EXPERTS, D), D ** -0.5),
        "peer_v": nrm(ks[16], (L, PEER_EXPERTS, D), (PEER_HEADS * PEER_TOPK) ** -0.5),
        "norm_final": 1.0 + nrm(ks[17], (D,), 0.02),
    }


def reference(x, norm_mix, norm_ffn, w_in, nsa_w_cmp_k, nsa_w_cmp_v, nsa_cmp_pe, hgrn_norm, hgrn_lower_bounds, w_branch_sb, w_branch_nsa, w_branch_hgrn, w_out, peer_w_q, peer_sub_keys, peer_u, peer_v, norm_final):
    lb_soft = jax.nn.softmax(hgrn_lower_bounds.astype(jnp.float32), axis=0)
    lower = jnp.cumsum(lb_soft, axis=0) - lb_soft[0]
    for l in range(DEPTH):
        h = rms_norm(x, norm_mix[l])
        x = x + hybrid_mixer(h, w_in[l], nsa_w_cmp_k[l], nsa_w_cmp_v[l], nsa_cmp_pe[l], hgrn_norm[l], lower[l], w_branch_sb[l], w_branch_nsa[l], w_branch_hgrn[l], w_out[l])
        h = rms_norm(x, norm_ffn[l])
        x = x + peer_ffn(h, peer_w_q[l], peer_sub_keys[l], peer_u[l], peer_v[l])
    return rms_norm(x, norm_final)
```

```python
import functools

import jax
import jax.numpy as jnp
from jax import lax
from jax.experimental import pallas as pl
from jax.experimental.pallas import tpu as pltpu

F32 = jnp.float32
BF16 = jnp.bfloat16

D_MODEL = 1024
HEAD_DIM = 64
EPS = 1e-6
NEG = -1e30
FORCE_SCORE = 1e4
Q_BLOCK = 128

SB_HEADS = 8
NSA_HEADS = 8
NSA_GROUPS = 2
NSA_HPG = NSA_HEADS // NSA_GROUPS
CMP_LEN = 32
CMP_STRIDE = 16
SEL_BLOCK = 64
SEL_TOPN = 4
WINDOW = 256
HGRN_HEADS = 4
HGRN_DK = 128
HGRN_CHUNK = 64
HGRN_SUB = 16
PEER_HEADS = 8
PEER_NKEYS = 128
PEER_TOPK = 16
PEER_QDIM = 128

C_MG = 0
C_SBQ = 3072
C_SBK = 3584
C_SBV = 4096
C_NQ = 4608
C_HQ = 5120
C_HF = 5632
C_HI = 6144
C_HG = 6656
C_NKV = 7168
C_NG = 7936
IN_WIDTH = 7960
IN_PAD = 8064
LANE = 128

VMEM_LIMIT = 56 * 1024 * 1024
SEL_CHUNK = 512


def _cparams(sem):
    return pltpu.CompilerParams(dimension_semantics=sem, vmem_limit_bytes=VMEM_LIMIT)


def _dot(a, b):
    return jnp.dot(a, b, preferred_element_type=F32)


def _dot_nt(a, b):
    return lax.dot_general(a, b, (((1,), (1,)), ((), ())), preferred_element_type=F32)


def _dot_tn(a, b):
    return lax.dot_general(a, b, (((0,), (0,)), ((), ())), preferred_element_type=F32)


def _split2(x):
    hi = x.astype(BF16)
    lo = (x - hi.astype(F32)).astype(BF16)
    return hi, lo


def _split3(x):
    h1 = x.astype(BF16)
    r1 = x - h1.astype(F32)
    h2 = r1.astype(BF16)
    h3 = (r1 - h2.astype(F32)).astype(BF16)
    return h1, h2, h3


def _sigmoid(x):
    return 1.0 / (1.0 + jnp.exp(-x))


def _inproj_kernel(x_ref, g_ref, w_ref, o_ref):
    x = x_ref[...]
    y = x * lax.rsqrt(jnp.mean(x * x, axis=-1, keepdims=True) + EPS) * g_ref[...]
    o_ref[...] = _dot(y.astype(BF16), w_ref[...])


def _inproj(x2, g, w):
    n = x2.shape[0]
    tm, tn = 512, IN_PAD // 3
    return pl.pallas_call(
        _inproj_kernel,
        out_shape=jax.ShapeDtypeStruct((n, IN_PAD), F32),
        grid=(IN_PAD // tn, n // tm),
        in_specs=[
            pl.BlockSpec((tm, D_MODEL), lambda c, i: (i, 0)),
            pl.BlockSpec((1, D_MODEL), lambda c, i: (0, 0)),
            pl.BlockSpec((D_MODEL, tn), lambda c, i: (0, c)),
        ],
        out_specs=pl.BlockSpec((tm, tn), lambda c, i: (i, c)),
        compiler_params=_cparams(("arbitrary", "arbitrary")),
        name="inproj",
    )(x2, g, w)


def _sb_kernel(q_ref, k_ref, v_ref, o_ref):
    i = pl.program_id(2)
    scale = HEAD_DIM ** -0.5
    r = lax.broadcasted_iota(jnp.int32, (Q_BLOCK, Q_BLOCK), 0)
    c = lax.broadcasted_iota(jnp.int32, (Q_BLOCK, Q_BLOCK), 1)
    before = c < r
    upper = (r > c).astype(BF16)

    for h in range(2):
        sl = slice(h * HEAD_DIM, (h + 1) * HEAD_DIM)
        qb = q_ref[0, :, sl].astype(BF16)

        def step(kb, vb, c_run, acc, diag):
            z = _dot_nt(qb, kb) * scale
            soft = jnp.log1p(jnp.exp(-jnp.abs(z)))
            lsp = jnp.minimum(z, 0.0) - soft
            lsn = -jnp.maximum(z, 0.0) - soft
            if diag:
                lsn = jnp.where(before, lsn, 0.0)
            hi, lo = _split2(lsn)
            after = c_run + (_dot(hi, upper) + _dot(lo, upper))
            a = jnp.exp(lsp + after)
            if diag:
                a = jnp.where(before, a, 0.0)
            acc = acc + _dot(a.astype(BF16), vb)
            c_run = c_run + jnp.sum(lsn, axis=1, keepdims=True)
            return c_run, acc

        def load(j):
            k0 = pl.multiple_of(j * Q_BLOCK, Q_BLOCK)
            return (k_ref[0, pl.ds(k0, Q_BLOCK), sl].astype(BF16),
                    v_ref[0, pl.ds(k0, Q_BLOCK), sl].astype(BF16))

        kb, vb = load(i)
        c_run, acc = step(kb, vb, jnp.zeros((Q_BLOCK, 1), F32),
                          jnp.zeros((Q_BLOCK, HEAD_DIM), F32), True)

        def body(n, carry):
            kb, vb = load(i - 1 - n)
            return step(kb, vb, carry[0], carry[1], False)

        c_run, acc = lax.fori_loop(0, i, body, (c_run, acc))
        o_ref[0, :, sl] = acc


def _sb_attention(proj3):
    b, t, _ = proj3.shape
    qb, kb, vb = C_SBQ // LANE, C_SBK // LANE, C_SBV // LANE
    return pl.pallas_call(
        _sb_kernel,
        out_shape=jax.ShapeDtypeStruct((b, t, SB_HEADS * HEAD_DIM), F32),
        grid=(b, SB_HEADS // 2, t // Q_BLOCK),
        in_specs=[
            pl.BlockSpec((1, Q_BLOCK, LANE), lambda bi, hp, i: (bi, i, qb + hp)),
            pl.BlockSpec((1, t, LANE), lambda bi, hp, i: (bi, 0, kb + hp)),
            pl.BlockSpec((1, t, LANE), lambda bi, hp, i: (bi, 0, vb + hp)),
        ],
        out_specs=pl.BlockSpec((1, Q_BLOCK, LANE), lambda bi, hp, i: (bi, i, hp)),
        compiler_params=_cparams(("arbitrary", "arbitrary", "arbitrary")),
        name="sb_attn",
    )(proj3, proj3, proj3)


def _hgrn_kernel(q_ref, f_ref, i_ref, g_ref, lb_ref, nw_ref, o_ref, st_ref, *, n_chunks):
    ch, sub = HGRN_CHUNK, HGRN_SUB
    st_ref[...] = jnp.zeros_like(st_ref)
    lb = lb_ref[...]
    nw = nw_ref[...]
    r = lax.broadcasted_iota(jnp.int32, (ch, ch), 0)
    c = lax.broadcasted_iota(jnp.int32, (ch, ch), 1)
    lower = (r >= c).astype(BF16)
    srow = lax.broadcasted_iota(jnp.int32, (sub, HGRN_DK), 0)

    def chunk(ci, carry):
        t0 = pl.multiple_of(ci * ch, ch)
        fz = f_ref[0, pl.ds(t0, ch), :]
        qz = q_ref[0, pl.ds(t0, ch), :]
        iv = i_ref[0, pl.ds(t0, ch), :]
        gz = g_ref[0, pl.ds(t0, ch), :]
        f = lb + (1.0 - lb) * _sigmoid(fz)
        lf = jnp.log(f)
        kk = 1.0 - f
        qh = qz * _sigmoid(qz)
        hi, lo = _split2(lf)
        bcum = _dot(lower, hi) + _dot(lower, lo)
        st = st_ref[...]
        o = _dot_nt((qh * jnp.exp(bcum)).astype(BF16), st.astype(BF16))
        ivb = iv.astype(BF16)

        rows = []
        for s in range(ch // sub):
            lo_r, hi_r = s * sub, (s + 1) * sub
            qs, ks, bs, vs = qh[lo_r:hi_r], kk[lo_r:hi_r], bcum[lo_r:hi_r], iv[lo_r:hi_r]
            o_s = o[lo_r:hi_r]
            if s > 0:
                bref = bcum[lo_r - 1:lo_r]
                qd = (qs * jnp.exp(bs - bref)).astype(BF16)
                kd = (kk[:lo_r] * jnp.exp(bref - bcum[:lo_r])).astype(BF16)
                att = _dot_nt(qd, kd)
                o_s = o_s + _dot(att.astype(BF16), ivb[:lo_r])
            diag_rows = []
            for t in range(sub):
                dlt = jnp.where(srow <= t, bs[t:t + 1] - bs, NEG)
                w = (qs[t:t + 1] * ks) * jnp.exp(dlt)
                att_col = jnp.sum(w, axis=1, keepdims=True)
                diag_rows.append(jnp.sum(att_col * vs, axis=0, keepdims=True))
            rows.append(o_s + jnp.concatenate(diag_rows, axis=0))
        o = jnp.concatenate(rows, axis=0)

        b_last = bcum[ch - 1:ch]
        kd = (kk * jnp.exp(b_last - bcum)).astype(BF16)
        st_ref[...] = st * jnp.exp(b_last) + _dot_tn(ivb, kd)

        y = o * lax.rsqrt(jnp.mean(o * o, axis=-1, keepdims=True) + EPS) * nw
        o_ref[0, pl.ds(t0, ch), :] = y * (gz * _sigmoid(gz))
        return carry

    lax.fori_loop(0, n_chunks, chunk, 0)


def _hgrn(proj3, lower_l, norm_l):
    b, t, _ = proj3.shape
    cq, cf, ci, cg = C_HQ // LANE, C_HF // LANE, C_HI // LANE, C_HG // LANE
    seq = lambda col: pl.BlockSpec((1, t, LANE), lambda bi, h: (bi, 0, col + h))
    vec = pl.BlockSpec((1, LANE), lambda bi, h: (0, h))
    return pl.pallas_call(
        functools.partial(_hgrn_kernel, n_chunks=t // HGRN_CHUNK),
        out_shape=jax.ShapeDtypeStruct((b, t, HGRN_HEADS * HGRN_DK), F32),
        grid=(b, HGRN_HEADS),
        in_specs=[seq(cq), seq(cf), seq(ci), seq(cg), vec, vec],
        out_specs=pl.BlockSpec((1, t, LANE), lambda bi, h: (bi, 0, h)),
        scratch_shapes=[pltpu.VMEM((HGRN_DK, HGRN_DK), F32)],
        compiler_params=_cparams(("arbitrary", "arbitrary")),
        name="hgrn2",
    )(proj3, proj3, proj3, proj3, lower_l, norm_l)


def _cmp_kernel(p_ref, w_ref, pe_ref, o_ref):
    half = CMP_STRIDE * HEAD_DIM
    p = p_ref[0, 0, 0].astype(BF16)
    w = w_ref[0].astype(BF16)
    a = _dot(p, w[:half])
    bm = _dot(p, w[half:])
    pe = jnp.broadcast_to(pe_ref[...], (8, 2 * half)).astype(BF16)
    const = _dot(pe, w)[0:1]
    o_ref[0, 0, 0] = a + pltpu.roll(bm, bm.shape[0] - 1, 0) + const


def _nsa_compress(pieces, w_cmp, pe_flat):
    b, _, g, n_piece, width = pieces.shape
    return pl.pallas_call(
        _cmp_kernel,
        out_shape=jax.ShapeDtypeStruct((b, 2, g, n_piece, HEAD_DIM), F32),
        grid=(b, 2, g),
        in_specs=[
            pl.BlockSpec((1, 1, 1, n_piece, width), lambda bi, kv, gi: (bi, kv, gi, 0, 0)),
            pl.BlockSpec((1, CMP_LEN * HEAD_DIM, HEAD_DIM), lambda bi, kv, gi: (kv, 0, 0)),
            pl.BlockSpec((1, CMP_LEN * HEAD_DIM), lambda bi, kv, gi: (0, 0)),
        ],
        out_specs=pl.BlockSpec((1, 1, 1, n_piece, HEAD_DIM), lambda bi, kv, gi: (bi, kv, gi, 0, 0)),
        compiler_params=_cparams(("arbitrary", "arbitrary", "arbitrary")),
        name="nsa_compress",
    )(pieces, w_cmp, pe_flat)


def _nsa_kernel(q_ref, cmp_ref, ks_ref, vs_ref, kw_ref, vw_ref, g_ref, o_ref, *, seq_len):
    i = pl.program_id(1)
    qn = Q_BLOCK
    scale = HEAD_DIM ** -0.5
    n_blk = seq_len // SEL_BLOCK
    t0 = i * qn
    trow = t0 + lax.broadcasted_iota(jnp.int32, (qn, 1), 0)
    lane = lax.broadcasted_iota(jnp.int32, (qn, LANE), 1)
    lane_f = lane.astype(F32)

    dist_c = trow - (lane * CMP_STRIDE + (CMP_LEN - 1))
    valid_c = dist_c >= 0
    dist_cf = dist_c.astype(F32)
    cr = lax.broadcasted_iota(jnp.int32, (LANE, LANE), 0) * CMP_STRIDE
    nb = lax.broadcasted_iota(jnp.int32, (LANE, LANE), 1)
    overlap = ((cr < nb * SEL_BLOCK + SEL_BLOCK) & (cr + CMP_LEN > nb * SEL_BLOCK)
               & (nb < n_blk)).astype(BF16)
    forced = (lane == trow // SEL_BLOCK) | (lane == 0)
    causal_b = lane * SEL_BLOCK <= trow

    gsig = _sigmoid(g_ref[0])
    trow4 = jnp.concatenate([trow] * NSA_HPG, axis=0)

    span = WINDOW + qn
    kstart = pl.multiple_of(jnp.maximum(i - WINDOW // qn, 0) * qn, qn)
    wpos = kstart + lax.broadcasted_iota(jnp.int32, (NSA_HPG * qn, span), 1)
    dist_w = trow4 - wpos
    valid_w = (dist_w >= 0) & (dist_w < WINDOW)
    dist_wf = dist_w.astype(F32)

    for g in range(NSA_GROUPS):
        gs = slice(g * HEAD_DIM, (g + 1) * HEAD_DIM)
        kc = cmp_ref[0, 0, g].astype(BF16)
        vc = cmp_ref[0, 1, g].astype(BF16)
        q_heads = [q_ref[0, :, (g * NSA_HPG + p) * HEAD_DIM:(g * NSA_HPG + p + 1) * HEAD_DIM].astype(BF16)
                   for p in range(NSA_HPG)]
        slopes = [2.0 ** (-(g * NSA_HPG + p + 1)) for p in range(NSA_HPG)]
        slope_col = jnp.concatenate(
            [jnp.full((qn, 1), s, F32) for s in slopes], axis=0)

        psum = jnp.zeros((qn, LANE), F32)
        o_cmp = []
        for p in range(NSA_HPG):
            s = _dot_nt(q_heads[p], kc) * scale - slopes[p] * dist_cf
            s = jnp.where(valid_c, s, NEG)
            m = jnp.max(s, axis=1, keepdims=True)
            e = jnp.where(valid_c, jnp.exp(s - m), 0.0)
            den = jnp.sum(e, axis=1, keepdims=True)
            pc = e / jnp.where(den > 0.0, den, 1.0)
            psum = psum + pc
            o_cmp.append(_dot(pc.astype(BF16), vc))
        hi, lo = _split2(psum)
        imp = _dot(hi, overlap) + _dot(lo, overlap)
        imp = jnp.where(forced, FORCE_SCORE, jnp.where(causal_b, imp, NEG))
        imp = jnp.where(lane < n_blk, imp, -jnp.inf)
        sel = jnp.zeros((qn, LANE), jnp.bool_)
        for _ in range(SEL_TOPN):
            mx = jnp.max(imp, axis=1, keepdims=True)
            idx = jnp.min(jnp.where(imp == mx, lane_f, float(LANE)), axis=1, keepdims=True)
            onehot = lane_f == idx
            sel = sel | onehot
            imp = jnp.where(onehot, -jnp.inf, imp)
        sel_b = jnp.where(sel, 1.0, 0.0).astype(BF16)

        q4 = jnp.concatenate(q_heads, axis=0)

        def sel_chunk(ci, carry):
            m_run, l_run, acc = carry
            k0 = pl.multiple_of(ci * SEL_CHUNK, SEL_CHUNK)
            kk = ks_ref[0, pl.ds(k0, SEL_CHUNK), gs].astype(BF16)
            vv = vs_ref[0, pl.ds(k0, SEL_CHUNK), gs].astype(BF16)
            er = lax.broadcasted_iota(jnp.int32, (LANE, SEL_CHUNK), 0)
            ec = lax.broadcasted_iota(jnp.int32, (LANE, SEL_CHUNK), 1)
            expand = (er == ci * (SEL_CHUNK // SEL_BLOCK) + ec // SEL_BLOCK).astype(BF16)
            kpos = k0 + lax.broadcasted_iota(jnp.int32, (qn, SEL_CHUNK), 1)
            dist = trow - kpos
            mask = (_dot(sel_b, expand) > 0.5) & (dist >= 0)
            mask4 = jnp.concatenate([mask] * NSA_HPG, axis=0)
            dist4 = jnp.concatenate([dist.astype(F32)] * NSA_HPG, axis=0)
            s = _dot_nt(q4, kk) * scale - slope_col * dist4
            s = jnp.where(mask4, s, NEG)
            m_new = jnp.maximum(m_run, jnp.max(s, axis=1, keepdims=True))
            alpha = jnp.exp(m_run - m_new)
            pm = jnp.exp(s - m_new)
            l_new = alpha * l_run + jnp.sum(pm, axis=1, keepdims=True)
            acc = alpha * acc + _dot(pm.astype(BF16), vv)
            return m_new, l_new, acc

        n_sel_chunks = (t0 + qn + SEL_CHUNK - 1) // SEL_CHUNK
        init = (jnp.full((NSA_HPG * qn, 1), NEG, F32), jnp.zeros((NSA_HPG * qn, 1), F32),
                jnp.zeros((NSA_HPG * qn, HEAD_DIM), F32))
        _, l_sel, acc_sel = lax.fori_loop(0, n_sel_chunks, sel_chunk, init)
        o_sel = acc_sel / l_sel

        kw = kw_ref[0, pl.ds(kstart, span), gs].astype(BF16)
        vw = vw_ref[0, pl.ds(kstart, span), gs].astype(BF16)
        s = _dot_nt(q4, kw) * scale - slope_col * dist_wf
        s = jnp.where(valid_w, s, NEG)
        m = jnp.max(s, axis=1, keepdims=True)
        e = jnp.exp(s - m)
        pw = e / jnp.sum(e, axis=1, keepdims=True)
        o_win = _dot(pw.astype(BF16), vw)

        for p in range(NSA_HPG):
            hh = g * NSA_HPG + p
            rows = slice(p * qn, (p + 1) * qn)
            o = (gsig[:, 3 * hh:3 * hh + 1] * o_cmp[p]
                 + gsig[:, 3 * hh + 1:3 * hh + 2] * o_sel[rows]
                 + gsig[:, 3 * hh + 2:3 * hh + 3] * o_win[rows])
            o_ref[0, :, hh * HEAD_DIM:(hh + 1) * HEAD_DIM] = o


def _nsa_attention(proj3, cmp_kv):
    b, t, _ = proj3.shape
    kvb = C_NKV // LANE
    seq = lambda col: pl.BlockSpec((1, t, LANE), lambda bi, i: (bi, 0, col))
    n_piece = cmp_kv.shape[3]
    return pl.pallas_call(
        functools.partial(_nsa_kernel, seq_len=t),
        out_shape=jax.ShapeDtypeStruct((b, t, NSA_HEADS * HEAD_DIM), F32),
        grid=(b, t // Q_BLOCK),
        in_specs=[
            pl.BlockSpec((1, Q_BLOCK, NSA_HEADS * HEAD_DIM), lambda bi, i: (bi, i, C_NQ // 512)),
            pl.BlockSpec((1, 2, NSA_GROUPS, n_piece, HEAD_DIM), lambda bi, i: (bi, 0, 0, 0, 0)),
            seq(kvb + 2), seq(kvb + 3), seq(kvb + 4), seq(kvb + 5),
            pl.BlockSpec((1, Q_BLOCK, LANE), lambda bi, i: (bi, i, C_NG // LANE)),
        ],
        out_specs=pl.BlockSpec((1, Q_BLOCK, NSA_HEADS * HEAD_DIM), lambda bi, i: (bi, i, 0)),
        compiler_params=_cparams(("arbitrary", "arbitrary")),
        name="nsa_attn",
    )(proj3, cmp_kv, proj3, proj3, proj3, proj3, proj3)


def _merge_kernel(x_ref, osb_ref, onsa_ref, ohg_ref, gsb_ref, gnsa_ref, ghg_ref,
                  wsb_ref, wnsa_ref, whg_ref, wo_ref, o_ref):
    m = (_sigmoid(gsb_ref[...]) * _dot(osb_ref[...].astype(BF16), wsb_ref[...])
         + _sigmoid(gnsa_ref[...]) * _dot(onsa_ref[...].astype(BF16), wnsa_ref[...])
         + _sigmoid(ghg_ref[...]) * _dot(ohg_ref[...].astype(BF16), whg_ref[...]))
    o_ref[...] = x_ref[...] + _dot(m.astype(BF16), wo_ref[...])


def _merge(x2, o_sb, o_nsa, o_hg, proj2, w_sb, w_nsa, w_hg, w_o):
    n = x2.shape[0]
    tm = 512
    row = lambda w: pl.BlockSpec((tm, w), lambda i: (i, 0))
    gate = lambda j: pl.BlockSpec((tm, D_MODEL), lambda i: (i, j))
    full = lambda a: pl.BlockSpec(a.shape, lambda i: (0, 0))
    return pl.pallas_call(
        _merge_kernel,
        out_shape=jax.ShapeDtypeStruct((n, D_MODEL), F32),
        grid=(n // tm,),
        in_specs=[row(D_MODEL), row(512), row(512), row(512), gate(0), gate(1), gate(2),
                  full(w_sb), full(w_nsa), full(w_hg), full(w_o)],
        out_specs=row(D_MODEL),
        compiler_params=_cparams(("arbitrary",)),
        name="merge_out",
    )(x2, o_sb, o_nsa, o_hg, proj2, proj2, proj2, w_sb, w_nsa, w_hg, w_o)


def _topk_lanes(s, k, payload=None):
    rows, width = s.shape
    lane = lax.broadcasted_iota(jnp.int32, (rows, width), 1).astype(F32)
    out_lane = lax.broadcasted_iota(jnp.int32, (rows, LANE), 1)
    vals = jnp.zeros((rows, LANE), F32)
    tags = jnp.zeros((rows, LANE), F32)
    for j in range(k):
        mx = jnp.max(s, axis=1, keepdims=True)
        idx = jnp.min(jnp.where(s == mx, lane, float(width)), axis=1, keepdims=True)
        onehot = lane == idx
        s = jnp.where(onehot, -jnp.inf, s)
        tag = idx if payload is None else jnp.sum(jnp.where(onehot, payload, 0.0), axis=1, keepdims=True)
        vals = jnp.where(out_lane == j, mx, vals)
        tags = jnp.where(out_lane == j, tag, tags)
    return vals, tags


def _route_kernel(x_ref, g_ref, wq_ref, keys_ref, h_ref, eidx_ref, gate_ref):
    k = PEER_TOPK
    half = PEER_QDIM // 2
    x = x_ref[...]
    h = (x * lax.rsqrt(jnp.mean(x * x, axis=-1, keepdims=True) + EPS) * g_ref[...]).astype(BF16)
    h_ref[...] = h
    q = _dot(h, wq_ref[...])
    er = lax.broadcasted_iota(jnp.int32, (LANE, k * k), 0)
    ec = lax.broadcasted_iota(jnp.int32, (LANE, k * k), 1)
    rep_i = (er == ec // k).astype(BF16)
    rep_j = (er == ec % k).astype(BF16)
    out_lane = lax.broadcasted_iota(jnp.int32, (x.shape[0], LANE), 1)

    def expand(v, mat):
        a, b, c = _split3(v)
        return (_dot(a, mat) + _dot(b, mat)) + _dot(c, mat)

    for hd in range(PEER_HEADS):
        tops = []
        for a in range(2):
            col = (hd * 2 + a) * half
            s = _dot_nt(q[:, col:col + half].astype(BF16), keys_ref[a].astype(BF16))
            tops.append(_topk_lanes(s, k))
        (s0, i0), (s1, i1) = tops
        cand = expand(s0, rep_i) + expand(s1, rep_j)
        cidx = _dot(i0.astype(BF16), rep_i) * float(PEER_NKEYS) + _dot(i1.astype(BF16), rep_j)
        best, eidx = _topk_lanes(cand, k, payload=cidx)
        mx = jnp.max(jnp.where(out_lane < k, best, -jnp.inf), axis=1, keepdims=True)
        e = jnp.where(out_lane < k, jnp.exp(best - mx), 0.0)
        gate_ref[:, hd * LANE:(hd + 1) * LANE] = e / jnp.sum(e, axis=1, keepdims=True)
        eidx_ref[:, hd * LANE:(hd + 1) * LANE] = eidx.astype(jnp.int32)


def _peer_route(x2, g, w_q, sub_keys):
    n = x2.shape[0]
    tm = 256
    wide = PEER_HEADS * LANE
    row = lambda w: pl.BlockSpec((tm, w), lambda i: (i, 0))
    return pl.pallas_call(
        _route_kernel,
        out_shape=(jax.ShapeDtypeStruct((n, D_MODEL), BF16),
                   jax.ShapeDtypeStruct((n, wide), jnp.int32),
                   jax.ShapeDtypeStruct((n, wide), F32)),
        grid=(n // tm,),
        in_specs=[row(D_MODEL),
                  pl.BlockSpec((1, D_MODEL), lambda i: (0, 0)),
                  pl.BlockSpec(w_q.shape, lambda i: (0, 0)),
                  pl.BlockSpec(sub_keys.shape, lambda i: (0, 0, 0))],
        out_specs=(row(D_MODEL), row(wide), row(wide)),
        compiler_params=_cparams(("arbitrary",)),
        name="peer_route",
    )(x2, g, w_q, sub_keys)


PEER_TOK = 8
PEER_ROWS = PEER_HEADS * PEER_TOPK


def _erf(x):
    return lax.erf(x)


def _expert_kernel(x_ref, h_ref, gt_ref, ug_ref, vg_ref, o_ref):
    nt, nr = PEER_TOK, PEER_ROWS
    ug = ug_ref[...].reshape(nt * nr, D_MODEL)
    hall = _dot_nt(ug, h_ref[...])
    lane = lax.broadcasted_iota(jnp.int32, (nr, nt), 1)
    hpre = jnp.zeros((nr, nt), F32)
    for n in range(nt):
        hpre = hpre + jnp.where(lane == n, hall[n * nr:(n + 1) * nr], 0.0)
    act = gt_ref[0] * (0.5 * hpre * (1.0 + _erf(hpre * (2.0 ** -0.5))))
    rows = []
    for n in range(nt):
        vg = vg_ref[n].astype(F32)
        rows.append(jnp.sum(act[:, n:n + 1] * vg, axis=0, keepdims=True))
    o_ref[...] = x_ref[...] + jnp.concatenate(rows, axis=0)


def _peer_experts(x2, h2, gate_t, ug, vg):
    n = x2.shape[0]
    nt, nr = PEER_TOK, PEER_ROWS
    return pl.pallas_call(
        _expert_kernel,
        out_shape=jax.ShapeDtypeStruct((n, D_MODEL), F32),
        grid=(n // nt,),
        in_specs=[pl.BlockSpec((nt, D_MODEL), lambda i: (i, 0)),
                  pl.BlockSpec((nt, D_MODEL), lambda i: (i, 0)),
                  pl.BlockSpec((1, nr, nt), lambda i: (i, 0, 0)),
                  pl.BlockSpec((nt, nr, D_MODEL), lambda i: (i, 0, 0)),
                  pl.BlockSpec((nt, nr, D_MODEL), lambda i: (i, 0, 0))],
        out_specs=pl.BlockSpec((nt, D_MODEL), lambda i: (i, 0)),
        compiler_params=_cparams(("arbitrary",)),
        name="peer_experts",
    )(x2, h2, gate_t, ug, vg)


def _norm_kernel(x_ref, g_ref, o_ref):
    x = x_ref[...]
    o_ref[...] = x * lax.rsqrt(jnp.mean(x * x, axis=-1, keepdims=True) + EPS) * g_ref[...]


def _final_norm(x2, g):
    n = x2.shape[0]
    tm = 1024
    return pl.pallas_call(
        _norm_kernel,
        out_shape=jax.ShapeDtypeStruct((n, D_MODEL), F32),
        grid=(n // tm,),
        in_specs=[pl.BlockSpec((tm, D_MODEL), lambda i: (i, 0)),
                  pl.BlockSpec((1, D_MODEL), lambda i: (0, 0))],
        out_specs=pl.BlockSpec((tm, D_MODEL), lambda i: (i, 0)),
        compiler_params=_cparams(("arbitrary",)),
        name="final_norm",
    )(x2, g)


def _permute_w_in(w_in):
    o = [0, 512, 1024, 1536, 2048, 2816, 2840, 3352, 3864, 4376, 4888, IN_WIDTH]
    sb_q, sb_k, sb_v, nsa_q, nsa_kv, nsa_g, hg_q, hg_f, hg_i, hg_g, merge_g = [
        w_in[..., o[j]:o[j + 1]] for j in range(11)]
    pad = jnp.zeros(w_in.shape[:-1] + (IN_PAD - IN_WIDTH,), w_in.dtype)
    return jnp.concatenate(
        [merge_g, sb_q, sb_k, sb_v, nsa_q, hg_q, hg_f, hg_i, hg_g, nsa_kv, nsa_g, pad], axis=-1)


def _mixer_layer(x2, bsz, seq, norm_g, w_in_p, w_cmp, pe_flat, hgrn_norm_l, lower_l,
                 w_sb, w_nsa, w_hg, w_o):
    n = bsz * seq
    proj2 = _inproj(x2, norm_g, w_in_p)
    proj3 = proj2.reshape(bsz, seq, IN_PAD)
    o_sb = _sb_attention(proj3)
    o_hg = _hgrn(proj3, lower_l, hgrn_norm_l)
    kv = proj3[:, :, C_NKV:C_NKV + 2 * LANE].reshape(bsz, seq, 2, NSA_GROUPS, HEAD_DIM)
    pieces = kv.transpose(0, 2, 3, 1, 4).reshape(
        bsz, 2, NSA_GROUPS, seq // CMP_STRIDE, CMP_STRIDE * HEAD_DIM)
    cmp_kv = _nsa_compress(pieces, w_cmp, pe_flat)
    o_nsa = _nsa_attention(proj3, cmp_kv)
    return _merge(x2, o_sb.reshape(n, -1), o_nsa.reshape(n, -1), o_hg.reshape(n, -1),
                  proj2, w_sb, w_nsa, w_hg, w_o)


def _peer_layer(x2, norm_g, w_q, sub_keys, u_tab, v_tab):
    n = x2.shape[0]
    h2, eidx_w, gate_w = _peer_route(x2, norm_g, w_q, sub_keys)
    eidx = eidx_w.reshape(n, PEER_HEADS, LANE)[:, :, :PEER_TOPK].reshape(n, PEER_ROWS)
    gate = gate_w.reshape(n, PEER_HEADS, LANE)[:, :, :PEER_TOPK].reshape(n, PEER_ROWS)
    gate_t = gate.reshape(n // PEER_TOK, PEER_TOK, PEER_ROWS).transpose(0, 2, 1)
    ug = u_tab[eidx]
    vg = v_tab[eidx]
    return _peer_experts(x2, h2, gate_t, ug, vg)


def kernel(x, norm_mix, norm_ffn, w_in, nsa_w_cmp_k, nsa_w_cmp_v, nsa_cmp_pe, hgrn_norm, hgrn_lower_bounds, w_branch_sb, w_branch_nsa, w_branch_hgrn, w_out, peer_w_q, peer_sub_keys, peer_u, peer_v, norm_final):
    bsz, seq, d = x.shape
    depth = w_in.shape[0]
    n = bsz * seq
    lb_soft = jax.nn.softmax(hgrn_lower_bounds.astype(F32), axis=0)
    lower = jnp.cumsum(lb_soft, axis=0) - lb_soft[0]
    w_in_p = _permute_w_in(w_in).astype(BF16)
    x2 = x.reshape(n, d)
    for l in range(depth):
        w_cmp = jnp.stack([nsa_w_cmp_k[l], nsa_w_cmp_v[l]])
        x2 = _mixer_layer(
            x2, bsz, seq, norm_mix[l][None], w_in_p[l], w_cmp,
            nsa_cmp_pe[l].reshape(1, CMP_LEN * HEAD_DIM), hgrn_norm[l][None], lower[l][None],
            w_branch_sb[l].astype(BF16), w_branch_nsa[l].astype(BF16),
            w_branch_hgrn[l].astype(BF16), w_out[l].astype(BF16))
        x2 = _peer_layer(x2, norm_ffn[l][None], peer_w_q[l].astype(BF16), peer_sub_keys[l],
                         peer_u[l].astype(BF16), peer_v[l].astype(BF16))
    return _final_norm(x2, norm_final[None]).reshape(bsz, seq, d)
```

```python
import functools

import jax
import jax.numpy as jnp
from jax import lax
from jax.experimental import pallas as pl
from jax.experimental.pallas import tpu as pltpu
from jax.experimental.pallas import tpu_sc as plsc

F32 = jnp.float32
BF16 = jnp.bfloat16

D_MODEL = 1024
HEAD_DIM = 64
EPS = 1e-6
NEG = -1e30
FORCE_SCORE = 1e4
Q_BLOCK = 128

SB_HEADS = 8
NSA_HEADS = 8
NSA_GROUPS = 2
NSA_HPG = NSA_HEADS // NSA_GROUPS
CMP_LEN = 32
CMP_STRIDE = 16
SEL_BLOCK = 64
SEL_TOPN = 4
WINDOW = 256
HGRN_HEADS = 4
HGRN_DK = 128
HGRN_CHUNK = 64
HGRN_SUB = 16
PEER_HEADS = 8
PEER_NKEYS = 128
PEER_TOPK = 16
PEER_QDIM = 128

C_MG = 0
C_SBQ = 3072
C_SBK = 3584
C_SBV = 4096
C_NQ = 4608
C_HQ = 5120
C_HF = 5632
C_HI = 6144
C_HG = 6656
C_NKV = 7168
C_NG = 7936
IN_WIDTH = 7960
IN_PAD = 8064
LANE = 128

VMEM_LIMIT = 56 * 1024 * 1024
SEL_CHUNK = 512


def _cparams(sem):
    return pltpu.CompilerParams(dimension_semantics=sem, vmem_limit_bytes=VMEM_LIMIT)


def _dot(a, b):
    return jnp.dot(a, b, preferred_element_type=F32)


def _dot_nt(a, b):
    return lax.dot_general(a, b, (((1,), (1,)), ((), ())), preferred_element_type=F32)


def _dot_tn(a, b):
    return lax.dot_general(a, b, (((0,), (0,)), ((), ())), preferred_element_type=F32)


def _split2(x):
    hi = x.astype(BF16)
    lo = (x - hi.astype(F32)).astype(BF16)
    return hi, lo


def _split3(x):
    h1 = x.astype(BF16)
    r1 = x - h1.astype(F32)
    h2 = r1.astype(BF16)
    h3 = (r1 - h2.astype(F32)).astype(BF16)
    return h1, h2, h3


def _sigmoid(x):
    return 1.0 / (1.0 + jnp.exp(-x))


def _inproj_kernel(x_ref, g_ref, w_ref, o_ref):
    x = x_ref[...]
    y = x * lax.rsqrt(jnp.mean(x * x, axis=-1, keepdims=True) + EPS) * g_ref[...]
    o_ref[...] = _dot(y.astype(BF16), w_ref[...])


def _inproj(x2, g, w):
    n = x2.shape[0]
    tm, tn = 512, IN_PAD // 3
    return pl.pallas_call(
        _inproj_kernel,
        out_shape=jax.ShapeDtypeStruct((n, IN_PAD), F32),
        grid=(IN_PAD // tn, n // tm),
        in_specs=[
            pl.BlockSpec((tm, D_MODEL), lambda c, i: (i, 0)),
            pl.BlockSpec((1, D_MODEL), lambda c, i: (0, 0)),
            pl.BlockSpec((D_MODEL, tn), lambda c, i: (0, c)),
        ],
        out_specs=pl.BlockSpec((tm, tn), lambda c, i: (i, c)),
        compiler_params=_cparams(("arbitrary", "arbitrary")),
        name="inproj",
    )(x2, g, w)


def _sb_kernel(q_ref, k_ref, v_ref, o_ref):
    i = pl.program_id(2)
    scale = HEAD_DIM ** -0.5
    r = lax.broadcasted_iota(jnp.int32, (Q_BLOCK, Q_BLOCK), 0)
    c = lax.broadcasted_iota(jnp.int32, (Q_BLOCK, Q_BLOCK), 1)
    before = c < r
    upper = (r > c).astype(BF16)

    for h in range(2):
        sl = slice(h * HEAD_DIM, (h + 1) * HEAD_DIM)
        qb = q_ref[0, :, sl].astype(BF16)

        def step(kb, vb, c_run, acc, diag):
            z = _dot_nt(qb, kb) * scale
            soft = jnp.log1p(jnp.exp(-jnp.abs(z)))
            lsp = jnp.minimum(z, 0.0) - soft
            lsn = -jnp.maximum(z, 0.0) - soft
            if diag:
                lsn = jnp.where(before, lsn, 0.0)
            hi, lo = _split2(lsn)
            after = c_run + (_dot(hi, upper) + _dot(lo, upper))
            a = jnp.exp(lsp + after)
            if diag:
                a = jnp.where(before, a, 0.0)
            acc = acc + _dot(a.astype(BF16), vb)
            c_run = c_run + jnp.sum(lsn, axis=1, keepdims=True)
            return c_run, acc

        def load(j):
            k0 = pl.multiple_of(j * Q_BLOCK, Q_BLOCK)
            return (k_ref[0, pl.ds(k0, Q_BLOCK), sl].astype(BF16),
                    v_ref[0, pl.ds(k0, Q_BLOCK), sl].astype(BF16))

        kb, vb = load(i)
        c_run, acc = step(kb, vb, jnp.zeros((Q_BLOCK, 1), F32),
                          jnp.zeros((Q_BLOCK, HEAD_DIM), F32), True)

        def body(n, carry):
            kb, vb = load(i - 1 - n)
            return step(kb, vb, carry[0], carry[1], False)

        c_run, acc = lax.fori_loop(0, i, body, (c_run, acc))
        o_ref[0, :, sl] = acc


def _sb_attention(proj3):
    b, t, _ = proj3.shape
    qb, kb, vb = C_SBQ // LANE, C_SBK // LANE, C_SBV // LANE
    return pl.pallas_call(
        _sb_kernel,
        out_shape=jax.ShapeDtypeStruct((b, t, SB_HEADS * HEAD_DIM), F32),
        grid=(b, SB_HEADS // 2, t // Q_BLOCK),
        in_specs=[
            pl.BlockSpec((1, Q_BLOCK, LANE), lambda bi, hp, i: (bi, i, qb + hp)),
            pl.BlockSpec((1, t, LANE), lambda bi, hp, i: (bi, 0, kb + hp)),
            pl.BlockSpec((1, t, LANE), lambda bi, hp, i: (bi, 0, vb + hp)),
        ],
        out_specs=pl.BlockSpec((1, Q_BLOCK, LANE), lambda bi, hp, i: (bi, i, hp)),
        compiler_params=_cparams(("arbitrary", "arbitrary", "arbitrary")),
        name="sb_attn",
    )(proj3, proj3, proj3)


def _hgrn_kernel(q_ref, f_ref, i_ref, g_ref, lb_ref, nw_ref, o_ref, st_ref, *, n_chunks):
    ch, sub = HGRN_CHUNK, HGRN_SUB
    st_ref[...] = jnp.zeros_like(st_ref)
    lb = lb_ref[...]
    nw = nw_ref[...]
    r = lax.broadcasted_iota(jnp.int32, (ch, ch), 0)
    c = lax.broadcasted_iota(jnp.int32, (ch, ch), 1)
    lower = (r >= c).astype(BF16)
    srow = lax.broadcasted_iota(jnp.int32, (sub, HGRN_DK), 0)

    def chunk(ci, carry):
        t0 = pl.multiple_of(ci * ch, ch)
        fz = f_ref[0, pl.ds(t0, ch), :]
        qz = q_ref[0, pl.ds(t0, ch), :]
        iv = i_ref[0, pl.ds(t0, ch), :]
        gz = g_ref[0, pl.ds(t0, ch), :]
        f = lb + (1.0 - lb) * _sigmoid(fz)
        lf = jnp.log(f)
        kk = 1.0 - f
        qh = qz * _sigmoid(qz)
        hi, lo = _split2(lf)
        bcum = _dot(lower, hi) + _dot(lower, lo)
        st = st_ref[...]
        o = _dot_nt((qh * jnp.exp(bcum)).astype(BF16), st.astype(BF16))
        ivb = iv.astype(BF16)

        rows = []
        for s in range(ch // sub):
            lo_r, hi_r = s * sub, (s + 1) * sub
            qs, ks, bs, vs = qh[lo_r:hi_r], kk[lo_r:hi_r], bcum[lo_r:hi_r], iv[lo_r:hi_r]
            o_s = o[lo_r:hi_r]
            if s > 0:
                bref = bcum[lo_r - 1:lo_r]
                qd = (qs * jnp.exp(bs - bref)).astype(BF16)
                kd = (kk[:lo_r] * jnp.exp(bref - bcum[:lo_r])).astype(BF16)
                att = _dot_nt(qd, kd)
                o_s = o_s + _dot(att.astype(BF16), ivb[:lo_r])
            diag_rows = []
            for t in range(sub):
                dlt = jnp.where(srow <= t, bs[t:t + 1] - bs, NEG)
                w = (qs[t:t + 1] * ks) * jnp.exp(dlt)
                att_col = jnp.sum(w, axis=1, keepdims=True)
                diag_rows.append(jnp.sum(att_col * vs, axis=0, keepdims=True))
            rows.append(o_s + jnp.concatenate(diag_rows, axis=0))
        o = jnp.concatenate(rows, axis=0)

        b_last = bcum[ch - 1:ch]
        kd = (kk * jnp.exp(b_last - bcum)).astype(BF16)
        st_ref[...] = st * jnp.exp(b_last) + _dot_tn(ivb, kd)

        y = o * lax.rsqrt(jnp.mean(o * o, axis=-1, keepdims=True) + EPS) * nw
        o_ref[0, pl.ds(t0, ch), :] = y * (gz * _sigmoid(gz))
        return carry

    lax.fori_loop(0, n_chunks, chunk, 0)


def _hgrn(proj3, lower_l, norm_l):
    b, t, _ = proj3.shape
    cq, cf, ci, cg = C_HQ // LANE, C_HF // LANE, C_HI // LANE, C_HG // LANE
    seq = lambda col: pl.BlockSpec((1, t, LANE), lambda bi, h: (bi, 0, col + h))
    vec = pl.BlockSpec((1, LANE), lambda bi, h: (0, h))
    return pl.pallas_call(
        functools.partial(_hgrn_kernel, n_chunks=t // HGRN_CHUNK),
        out_shape=jax.ShapeDtypeStruct((b, t, HGRN_HEADS * HGRN_DK), F32),
        grid=(b, HGRN_HEADS),
        in_specs=[seq(cq), seq(cf), seq(ci), seq(cg), vec, vec],
        out_specs=pl.BlockSpec((1, t, LANE), lambda bi, h: (bi, 0, h)),
        scratch_shapes=[pltpu.VMEM((HGRN_DK, HGRN_DK), F32)],
        compiler_params=_cparams(("arbitrary", "arbitrary")),
        name="hgrn2",
    )(proj3, proj3, proj3, proj3, lower_l, norm_l)


def _cmp_kernel(p_ref, w_ref, pe_ref, o_ref):
    half = CMP_STRIDE * HEAD_DIM
    p = p_ref[0, 0, 0].astype(BF16)
    w = w_ref[0].astype(BF16)
    a = _dot(p, w[:half])
    bm = _dot(p, w[half:])
    pe = jnp.broadcast_to(pe_ref[...], (8, 2 * half)).astype(BF16)
    const = _dot(pe, w)[0:1]
    o_ref[0, 0, 0] = a + pltpu.roll(bm, bm.shape[0] - 1, 0) + const


def _nsa_compress(pieces, w_cmp, pe_flat):
    b, _, g, n_piece, width = pieces.shape
    return pl.pallas_call(
        _cmp_kernel,
        out_shape=jax.ShapeDtypeStruct((b, 2, g, n_piece, HEAD_DIM), F32),
        grid=(b, 2, g),
        in_specs=[
            pl.BlockSpec((1, 1, 1, n_piece, width), lambda bi, kv, gi: (bi, kv, gi, 0, 0)),
            pl.BlockSpec((1, CMP_LEN * HEAD_DIM, HEAD_DIM), lambda bi, kv, gi: (kv, 0, 0)),
            pl.BlockSpec((1, CMP_LEN * HEAD_DIM), lambda bi, kv, gi: (0, 0)),
        ],
        out_specs=pl.BlockSpec((1, 1, 1, n_piece, HEAD_DIM), lambda bi, kv, gi: (bi, kv, gi, 0, 0)),
        compiler_params=_cparams(("arbitrary", "arbitrary", "arbitrary")),
        name="nsa_compress",
    )(pieces, w_cmp, pe_flat)


def _nsa_kernel(q_ref, cmp_ref, ks_ref, vs_ref, kw_ref, vw_ref, g_ref, o_ref, *, seq_len):
    i = pl.program_id(1)
    qn = Q_BLOCK
    scale = HEAD_DIM ** -0.5
    n_blk = seq_len // SEL_BLOCK
    t0 = i * qn
    trow = t0 + lax.broadcasted_iota(jnp.int32, (qn, 1), 0)
    lane = lax.broadcasted_iota(jnp.int32, (qn, LANE), 1)
    lane_f = lane.astype(F32)

    dist_c = trow - (lane * CMP_STRIDE + (CMP_LEN - 1))
    valid_c = dist_c >= 0
    dist_cf = dist_c.astype(F32)
    cr = lax.broadcasted_iota(jnp.int32, (LANE, LANE), 0) * CMP_STRIDE
    nb = lax.broadcasted_iota(jnp.int32, (LANE, LANE), 1)
    overlap = ((cr < nb * SEL_BLOCK + SEL_BLOCK) & (cr + CMP_LEN > nb * SEL_BLOCK)
               & (nb < n_blk)).astype(BF16)
    forced = (lane == trow // SEL_BLOCK) | (lane == 0)
    causal_b = lane * SEL_BLOCK <= trow

    gsig = _sigmoid(g_ref[0])
    trow4 = jnp.concatenate([trow] * NSA_HPG, axis=0)

    span = WINDOW + qn
    kstart = pl.multiple_of(jnp.maximum(i - WINDOW // qn, 0) * qn, qn)
    wpos = kstart + lax.broadcasted_iota(jnp.int32, (NSA_HPG * qn, span), 1)
    dist_w = trow4 - wpos
    valid_w = (dist_w >= 0) & (dist_w < WINDOW)
    dist_wf = dist_w.astype(F32)

    for g in range(NSA_GROUPS):
        gs = slice(g * HEAD_DIM, (g + 1) * HEAD_DIM)
        kc = cmp_ref[0, 0, g].astype(BF16)
        vc = cmp_ref[0, 1, g].astype(BF16)
        q_heads = [q_ref[0, :, (g * NSA_HPG + p) * HEAD_DIM:(g * NSA_HPG + p + 1) * HEAD_DIM].astype(BF16)
                   for p in range(NSA_HPG)]
        slopes = [2.0 ** (-(g * NSA_HPG + p + 1)) for p in range(NSA_HPG)]
        slope_col = jnp.concatenate(
            [jnp.full((qn, 1), s, F32) for s in slopes], axis=0)

        psum = jnp.zeros((qn, LANE), F32)
        o_cmp = []
        for p in range(NSA_HPG):
            s = _dot_nt(q_heads[p], kc) * scale - slopes[p] * dist_cf
            s = jnp.where(valid_c, s, NEG)
            m = jnp.max(s, axis=1, keepdims=True)
            e = jnp.where(valid_c, jnp.exp(s - m), 0.0)
            den = jnp.sum(e, axis=1, keepdims=True)
            pc = e / jnp.where(den > 0.0, den, 1.0)
            psum = psum + pc
            o_cmp.append(_dot(pc.astype(BF16), vc))
        hi, lo = _split2(psum)
        imp = _dot(hi, overlap) + _dot(lo, overlap)
        imp = jnp.where(forced, FORCE_SCORE, jnp.where(causal_b, imp, NEG))
        imp = jnp.where(lane < n_blk, imp, -jnp.inf)
        sel = jnp.zeros((qn, LANE), jnp.bool_)
        for _ in range(SEL_TOPN):
            mx = jnp.max(imp, axis=1, keepdims=True)
            idx = jnp.min(jnp.where(imp == mx, lane_f, float(LANE)), axis=1, keepdims=True)
            onehot = lane_f == idx
            sel = sel | onehot
            imp = jnp.where(onehot, -jnp.inf, imp)
        sel_b = jnp.where(sel, 1.0, 0.0).astype(BF16)

        q4 = jnp.concatenate(q_heads, axis=0)

        def sel_chunk(ci, carry):
            m_run, l_run, acc = carry
            k0 = pl.multiple_of(ci * SEL_CHUNK, SEL_CHUNK)
            kk = ks_ref[0, pl.ds(k0, SEL_CHUNK), gs].astype(BF16)
            vv = vs_ref[0, pl.ds(k0, SEL_CHUNK), gs].astype(BF16)
            er = lax.broadcasted_iota(jnp.int32, (LANE, SEL_CHUNK), 0)
            ec = lax.broadcasted_iota(jnp.int32, (LANE, SEL_CHUNK), 1)
            expand = (er == ci * (SEL_CHUNK // SEL_BLOCK) + ec // SEL_BLOCK).astype(BF16)
            kpos = k0 + lax.broadcasted_iota(jnp.int32, (qn, SEL_CHUNK), 1)
            dist = trow - kpos
            mask = (_dot(sel_b, expand) > 0.5) & (dist >= 0)
            mask4 = jnp.concatenate([mask] * NSA_HPG, axis=0)
            dist4 = jnp.concatenate([dist.astype(F32)] * NSA_HPG, axis=0)
            s = _dot_nt(q4, kk) * scale - slope_col * dist4
            s = jnp.where(mask4, s, NEG)
            m_new = jnp.maximum(m_run, jnp.max(s, axis=1, keepdims=True))
            alpha = jnp.exp(m_run - m_new)
            pm = jnp.exp(s - m_new)
            l_new = alpha * l_run + jnp.sum(pm, axis=1, keepdims=True)
            acc = alpha * acc + _dot(pm.astype(BF16), vv)
            return m_new, l_new, acc

        n_sel_chunks = (t0 + qn + SEL_CHUNK - 1) // SEL_CHUNK
        init = (jnp.full((NSA_HPG * qn, 1), NEG, F32), jnp.zeros((NSA_HPG * qn, 1), F32),
                jnp.zeros((NSA_HPG * qn, HEAD_DIM), F32))
        _, l_sel, acc_sel = lax.fori_loop(0, n_sel_chunks, sel_chunk, init)
        o_sel = acc_sel / l_sel

        kw = kw_ref[0, pl.ds(kstart, span), gs].astype(BF16)
        vw = vw_ref[0, pl.ds(kstart, span), gs].astype(BF16)
        s = _dot_nt(q4, kw) * scale - slope_col * dist_wf
        s = jnp.where(valid_w, s, NEG)
        m = jnp.max(s, axis=1, keepdims=True)
        e = jnp.exp(s - m)
        pw = e / jnp.sum(e, axis=1, keepdims=True)
        o_win = _dot(pw.astype(BF16), vw)

        for p in range(NSA_HPG):
            hh = g * NSA_HPG + p
            rows = slice(p * qn, (p + 1) * qn)
            o = (gsig[:, 3 * hh:3 * hh + 1] * o_cmp[p]
                 + gsig[:, 3 * hh + 1:3 * hh + 2] * o_sel[rows]
                 + gsig[:, 3 * hh + 2:3 * hh + 3] * o_win[rows])
            o_ref[0, :, hh * HEAD_DIM:(hh + 1) * HEAD_DIM] = o


def _nsa_attention(proj3, cmp_kv):
    b, t, _ = proj3.shape
    kvb = C_NKV // LANE
    seq = lambda col: pl.BlockSpec((1, t, LANE), lambda bi, i: (bi, 0, col))
    n_piece = cmp_kv.shape[3]
    return pl.pallas_call(
        functools.partial(_nsa_kernel, seq_len=t),
        out_shape=jax.ShapeDtypeStruct((b, t, NSA_HEADS * HEAD_DIM), F32),
        grid=(b, t // Q_BLOCK),
        in_specs=[
            pl.BlockSpec((1, Q_BLOCK, NSA_HEADS * HEAD_DIM), lambda bi, i: (bi, i, C_NQ // 512)),
            pl.BlockSpec((1, 2, NSA_GROUPS, n_piece, HEAD_DIM), lambda bi, i: (bi, 0, 0, 0, 0)),
            seq(kvb + 2), seq(kvb + 3), seq(kvb + 4), seq(kvb + 5),
            pl.BlockSpec((1, Q_BLOCK, LANE), lambda bi, i: (bi, i, C_NG // LANE)),
        ],
        out_specs=pl.BlockSpec((1, Q_BLOCK, NSA_HEADS * HEAD_DIM), lambda bi, i: (bi, i, 0)),
        compiler_params=_cparams(("arbitrary", "arbitrary")),
        name="nsa_attn",
    )(proj3, cmp_kv, proj3, proj3, proj3, proj3, proj3)


def _merge_kernel(x_ref, osb_ref, onsa_ref, ohg_ref, gsb_ref, gnsa_ref, ghg_ref,
                  wsb_ref, wnsa_ref, whg_ref, wo_ref, o_ref):
    m = (_sigmoid(gsb_ref[...]) * _dot(osb_ref[...].astype(BF16), wsb_ref[...])
         + _sigmoid(gnsa_ref[...]) * _dot(onsa_ref[...].astype(BF16), wnsa_ref[...])
         + _sigmoid(ghg_ref[...]) * _dot(ohg_ref[...].astype(BF16), whg_ref[...]))
    o_ref[...] = x_ref[...] + _dot(m.astype(BF16), wo_ref[...])


def _merge(x2, o_sb, o_nsa, o_hg, proj2, w_sb, w_nsa, w_hg, w_o):
    n = x2.shape[0]
    tm = 512
    row = lambda w: pl.BlockSpec((tm, w), lambda i: (i, 0))
    gate = lambda j: pl.BlockSpec((tm, D_MODEL), lambda i: (i, j))
    full = lambda a: pl.BlockSpec(a.shape, lambda i: (0, 0))
    return pl.pallas_call(
        _merge_kernel,
        out_shape=jax.ShapeDtypeStruct((n, D_MODEL), F32),
        grid=(n // tm,),
        in_specs=[row(D_MODEL), row(512), row(512), row(512), gate(0), gate(1), gate(2),
                  full(w_sb), full(w_nsa), full(w_hg), full(w_o)],
        out_specs=row(D_MODEL),
        compiler_params=_cparams(("arbitrary",)),
        name="merge_out",
    )(x2, o_sb, o_nsa, o_hg, proj2, proj2, proj2, w_sb, w_nsa, w_hg, w_o)


def _topk_lanes(s, k, payload=None):
    rows, width = s.shape
    lane = lax.broadcasted_iota(jnp.int32, (rows, width), 1).astype(F32)
    out_lane = lax.broadcasted_iota(jnp.int32, (rows, LANE), 1)
    vals = jnp.zeros((rows, LANE), F32)
    tags = jnp.zeros((rows, LANE), F32)
    for j in range(k):
        mx = jnp.max(s, axis=1, keepdims=True)
        idx = jnp.min(jnp.where(s == mx, lane, float(width)), axis=1, keepdims=True)
        onehot = lane == idx
        s = jnp.where(onehot, -jnp.inf, s)
        tag = idx if payload is None else jnp.sum(jnp.where(onehot, payload, 0.0), axis=1, keepdims=True)
        vals = jnp.where(out_lane == j, mx, vals)
        tags = jnp.where(out_lane == j, tag, tags)
    return vals, tags


def _route_kernel(x_ref, g_ref, wq_ref, keys_ref, h_ref, eidx_ref, gate_ref):
    k = PEER_TOPK
    half = PEER_QDIM // 2
    x = x_ref[...]
    h = (x * lax.rsqrt(jnp.mean(x * x, axis=-1, keepdims=True) + EPS) * g_ref[...]).astype(BF16)
    h_ref[...] = h
    q = _dot(h, wq_ref[...])
    er = lax.broadcasted_iota(jnp.int32, (LANE, k * k), 0)
    ec = lax.broadcasted_iota(jnp.int32, (LANE, k * k), 1)
    rep_i = (er == ec // k).astype(BF16)
    rep_j = (er == ec % k).astype(BF16)
    out_lane = lax.broadcasted_iota(jnp.int32, (x.shape[0], LANE), 1)

    def expand(v, mat):
        a, b, c = _split3(v)
        return (_dot(a, mat) + _dot(b, mat)) + _dot(c, mat)

    for hd in range(PEER_HEADS):
        tops = []
        for a in range(2):
            col = (hd * 2 + a) * half
            s = _dot_nt(q[:, col:col + half].astype(BF16), keys_ref[a].astype(BF16))
            tops.append(_topk_lanes(s, k))
        (s0, i0), (s1, i1) = tops
        cand = expand(s0, rep_i) + expand(s1, rep_j)
        cidx = _dot(i0.astype(BF16), rep_i) * float(PEER_NKEYS) + _dot(i1.astype(BF16), rep_j)
        best, eidx = _topk_lanes(cand, k, payload=cidx)
        mx = jnp.max(jnp.where(out_lane < k, best, -jnp.inf), axis=1, keepdims=True)
        e = jnp.where(out_lane < k, jnp.exp(best - mx), 0.0)
        gate_ref[:, hd * LANE:(hd + 1) * LANE] = e / jnp.sum(e, axis=1, keepdims=True)
        eidx_ref[:, hd * LANE:(hd + 1) * LANE] = eidx.astype(jnp.int32)


def _peer_route(x2, g, w_q, sub_keys):
    n = x2.shape[0]
    tm = 256
    wide = PEER_HEADS * LANE
    row = lambda w: pl.BlockSpec((tm, w), lambda i: (i, 0))
    return pl.pallas_call(
        _route_kernel,
        out_shape=(jax.ShapeDtypeStruct((n, D_MODEL), BF16),
                   jax.ShapeDtypeStruct((n, wide), jnp.int32),
                   jax.ShapeDtypeStruct((n, wide), F32)),
        grid=(n // tm,),
        in_specs=[row(D_MODEL),
                  pl.BlockSpec((1, D_MODEL), lambda i: (0, 0)),
                  pl.BlockSpec(w_q.shape, lambda i: (0, 0)),
                  pl.BlockSpec(sub_keys.shape, lambda i: (0, 0, 0))],
        out_specs=(row(D_MODEL), row(wide), row(wide)),
        compiler_params=_cparams(("arbitrary",)),
        name="peer_route",
    )(x2, g, w_q, sub_keys)


ROW_WORDS = D_MODEL // 2
ROW_PARTS = ROW_WORDS // LANE
GATHER_WIN = 128


def _pack_table(tab):
    bits = lax.bitcast_convert_type(tab.astype(BF16), jnp.uint16).astype(jnp.uint32)
    words = (bits[:, ROW_WORDS:] << 16) | bits[:, :ROW_WORDS]
    return lax.bitcast_convert_type(words, jnp.int32).reshape(-1, LANE)


def _sc_gather(table, idx):
    m = idx.shape[0]
    mesh = plsc.VectorSubcoreMesh(core_axis_name="core", subcore_axis_name="subcore")

    @pl.kernel(out_type=jax.ShapeDtypeStruct((m, LANE), table.dtype), mesh=mesh)
    def gather(tab_hbm, idx_hbm, out_hbm):
        def body(idx_vmem, out_vmem):
            pltpu.sync_copy(tab_hbm.at[idx_vmem.at[0]], out_vmem)

        pltpu.emit_pipeline(
            body,
            grid=(m // GATHER_WIN,),
            in_specs=[pl.BlockSpec((1, GATHER_WIN), lambda i: (0, i))],
            out_specs=[pl.BlockSpec((GATHER_WIN, LANE), lambda i: (i, 0))],
            core_axis_name=("core", "subcore"),
            dimension_semantics=(pltpu.PARALLEL,),
        )(idx_hbm, out_hbm)

    return gather(table, idx.reshape(1, m))


PEER_TOK = 16
PEER_ROWS = PEER_HEADS * PEER_TOPK
HIGH_MASK = -65536


def _unpack(words):
    lo = lax.bitcast_convert_type(lax.shift_left(words, 16), F32)
    hi = lax.bitcast_convert_type(words & HIGH_MASK, F32)
    return lo, hi


def _expert_kernel(x_ref, h_ref, gt_ref, ug_ref, vg_ref, o_ref):
    nt, nr = PEER_TOK, PEER_ROWS
    h = h_ref[...].astype(F32)
    lane = lax.broadcasted_iota(jnp.int32, (nr, nt), 1)
    hpre = jnp.zeros((nr, nt), F32)
    for n in range(nt):
        s = jnp.zeros((nr, LANE), F32)
        for j in range(ROW_PARTS):
            lo, hi = _unpack(ug_ref[pl.ds((n * ROW_PARTS + j) * nr, nr), :])
            s = s + lo * h[n:n + 1, j * LANE:(j + 1) * LANE]
            s = s + hi * h[n:n + 1, ROW_WORDS + j * LANE:ROW_WORDS + (j + 1) * LANE]
        hpre = hpre + jnp.where(lane == n, jnp.sum(s, axis=1, keepdims=True), 0.0)
    act = gt_ref[0] * (0.5 * hpre * (1.0 + lax.erf(hpre * (2.0 ** -0.5))))
    rows = []
    for n in range(nt):
        a = act[:, n:n + 1]
        los, his = [], []
        for j in range(ROW_PARTS):
            lo, hi = _unpack(vg_ref[pl.ds((n * ROW_PARTS + j) * nr, nr), :])
            los.append(jnp.sum(a * lo, axis=0, keepdims=True))
            his.append(jnp.sum(a * hi, axis=0, keepdims=True))
        rows.append(jnp.concatenate(los + his, axis=1))
    o_ref[...] = x_ref[...] + jnp.concatenate(rows, axis=0)


def _peer_experts(x2, h2, gate_t, ug, vg):
    n = x2.shape[0]
    nt, nr = PEER_TOK, PEER_ROWS
    blk = nt * ROW_PARTS * nr
    return pl.pallas_call(
        _expert_kernel,
        out_shape=jax.ShapeDtypeStruct((n, D_MODEL), F32),
        grid=(n // nt,),
        in_specs=[pl.BlockSpec((nt, D_MODEL), lambda i: (i, 0)),
                  pl.BlockSpec((nt, D_MODEL), lambda i: (i, 0)),
                  pl.BlockSpec((1, nr, nt), lambda i: (i, 0, 0)),
                  pl.BlockSpec((blk, LANE), lambda i: (i, 0)),
                  pl.BlockSpec((blk, LANE), lambda i: (i, 0))],
        out_specs=pl.BlockSpec((nt, D_MODEL), lambda i: (i, 0)),
        compiler_params=_cparams(("arbitrary",)),
        name="peer_experts",
    )(x2, h2, gate_t, ug, vg)


def _norm_kernel(x_ref, g_ref, o_ref):
    x = x_ref[...]
    o_ref[...] = x * lax.rsqrt(jnp.mean(x * x, axis=-1, keepdims=True) + EPS) * g_ref[...]


def _final_norm(x2, g):
    n = x2.shape[0]
    tm = 1024
    return pl.pallas_call(
        _norm_kernel,
        out_shape=jax.ShapeDtypeStruct((n, D_MODEL), F32),
        grid=(n // tm,),
        in_specs=[pl.BlockSpec((tm, D_MODEL), lambda i: (i, 0)),
                  pl.BlockSpec((1, D_MODEL), lambda i: (0, 0))],
        out_specs=pl.BlockSpec((tm, D_MODEL), lambda i: (i, 0)),
        compiler_params=_cparams(("arbitrary",)),
        name="final_norm",
    )(x2, g)


def _permute_w_in(w_in):
    o = [0, 512, 1024, 1536, 2048, 2816, 2840, 3352, 3864, 4376, 4888, IN_WIDTH]
    sb_q, sb_k, sb_v, nsa_q, nsa_kv, nsa_g, hg_q, hg_f, hg_i, hg_g, merge_g = [
        w_in[..., o[j]:o[j + 1]] for j in range(11)]
    pad = jnp.zeros(w_in.shape[:-1] + (IN_PAD - IN_WIDTH,), w_in.dtype)
    return jnp.concatenate(
        [merge_g, sb_q, sb_k, sb_v, nsa_q, hg_q, hg_f, hg_i, hg_g, nsa_kv, nsa_g, pad], axis=-1)


def _mixer_layer(x2, bsz, seq, norm_g, w_in_p, w_cmp, pe_flat, hgrn_norm_l, lower_l,
                 w_sb, w_nsa, w_hg, w_o):
    n = bsz * seq
    proj2 = _inproj(x2, norm_g, w_in_p)
    proj3 = proj2.reshape(bsz, seq, IN_PAD)
    o_sb = _sb_attention(proj3)
    o_hg = _hgrn(proj3, lower_l, hgrn_norm_l)
    kv = proj3[:, :, C_NKV:C_NKV + 2 * LANE].reshape(bsz, seq, 2, NSA_GROUPS, HEAD_DIM)
    pieces = kv.transpose(0, 2, 3, 1, 4).reshape(
        bsz, 2, NSA_GROUPS, seq // CMP_STRIDE, CMP_STRIDE * HEAD_DIM)
    cmp_kv = _nsa_compress(pieces, w_cmp, pe_flat)
    o_nsa = _nsa_attention(proj3, cmp_kv)
    return _merge(x2, o_sb.reshape(n, -1), o_nsa.reshape(n, -1), o_hg.reshape(n, -1),
                  proj2, w_sb, w_nsa, w_hg, w_o)


def _peer_layer(x2, norm_g, w_q, sub_keys, u_words, v_words):
    n = x2.shape[0]
    h2, eidx_w, gate_w = _peer_route(x2, norm_g, w_q, sub_keys)
    eidx = eidx_w.reshape(n, PEER_HEADS, LANE)[:, :, :PEER_TOPK].reshape(n, PEER_ROWS)
    gate = gate_w.reshape(n, PEER_HEADS, LANE)[:, :, :PEER_TOPK].reshape(n, PEER_ROWS)
    gate_t = gate.reshape(n // PEER_TOK, PEER_TOK, PEER_ROWS).transpose(0, 2, 1)
    parts = jnp.arange(ROW_PARTS, dtype=jnp.int32)[None, :, None]
    idx = (eidx[:, None, :] * ROW_PARTS + parts).reshape(-1)
    ug = _sc_gather(u_words, idx)
    vg = _sc_gather(v_words, idx)
    return _peer_experts(x2, h2, gate_t, ug, vg)


def kernel(x, norm_mix, norm_ffn, w_in, nsa_w_cmp_k, nsa_w_cmp_v, nsa_cmp_pe, hgrn_norm, hgrn_lower_bounds, w_branch_sb, w_branch_nsa, w_branch_hgrn, w_out, peer_w_q, peer_sub_keys, peer_u, peer_v, norm_final):
    bsz, seq, d = x.shape
    depth = w_in.shape[0]
    n = bsz * seq
    lb_soft = jax.nn.softmax(hgrn_lower_bounds.astype(F32), axis=0)
    lower = jnp.cumsum(lb_soft, axis=0) - lb_soft[0]
    w_in_p = _permute_w_in(w_in).astype(BF16)
    x2 = x.reshape(n, d)
    for l in range(depth):
        w_cmp = jnp.stack([nsa_w_cmp_k[l], nsa_w_cmp_v[l]])
        x2 = _mixer_layer(
            x2, bsz, seq, norm_mix[l][None], w_in_p[l], w_cmp,
            nsa_cmp_pe[l].reshape(1, CMP_LEN * HEAD_DIM), hgrn_norm[l][None], lower[l][None],
            w_branch_sb[l].astype(BF16), w_branch_nsa[l].astype(BF16),
            w_branch_hgrn[l].astype(BF16), w_out[l].astype(BF16))
        x2 = _peer_layer(x2, norm_ffn[l][None], peer_w_q[l].astype(BF16), peer_sub_keys[l],
                         _pack_table(peer_u[l]), _pack_table(peer_v[l]))
    return _final_norm(x2, norm_final[None]).reshape(bsz, seq, d)
```

```python
import functools

import jax
import jax.numpy as jnp
from jax import lax
from jax.experimental import pallas as pl
from jax.experimental.pallas import tpu as pltpu
from jax.experimental.pallas import tpu_sc as plsc

F32 = jnp.float32
BF16 = jnp.bfloat16

D_MODEL = 1024
HEAD_DIM = 64
EPS = 1e-6
NEG = -1e30
FORCE_SCORE = 1e4
Q_BLOCK = 128

SB_HEADS = 8
NSA_HEADS = 8
NSA_GROUPS = 2
NSA_HPG = NSA_HEADS // NSA_GROUPS
CMP_LEN = 32
CMP_STRIDE = 16
SEL_BLOCK = 64
SEL_TOPN = 4
WINDOW = 256
HGRN_HEADS = 4
HGRN_DK = 128
HGRN_CHUNK = 64
HGRN_SUB = 16
PEER_HEADS = 8
PEER_NKEYS = 128
PEER_TOPK = 16
PEER_QDIM = 128

C_MG = 0
C_SBQ = 3072
C_SBK = 3584
C_SBV = 4096
C_NQ = 4608
C_HQ = 5120
C_HF = 5632
C_HI = 6144
C_HG = 6656
C_NKV = 7168
C_NG = 7936
IN_WIDTH = 7960
IN_PAD = 8064
LANE = 128

VMEM_LIMIT = 56 * 1024 * 1024
SEL_CHUNK = 512


def _cparams(sem):
    return pltpu.CompilerParams(dimension_semantics=sem, vmem_limit_bytes=VMEM_LIMIT)


def _dot(a, b):
    return jnp.dot(a, b, preferred_element_type=F32)


def _dot_nt(a, b):
    return lax.dot_general(a, b, (((1,), (1,)), ((), ())), preferred_element_type=F32)


def _dot_tn(a, b):
    return lax.dot_general(a, b, (((0,), (0,)), ((), ())), preferred_element_type=F32)


def _split2(x):
    hi = x.astype(BF16)
    lo = (x - hi.astype(F32)).astype(BF16)
    return hi, lo


def _split3(x):
    h1 = x.astype(BF16)
    r1 = x - h1.astype(F32)
    h2 = r1.astype(BF16)
    h3 = (r1 - h2.astype(F32)).astype(BF16)
    return h1, h2, h3


def _sigmoid(x):
    return 1.0 / (1.0 + jnp.exp(-x))


def _inproj_kernel(x_ref, g_ref, w_ref, o_ref):
    x = x_ref[...]
    y = x * lax.rsqrt(jnp.mean(x * x, axis=-1, keepdims=True) + EPS) * g_ref[...]
    o_ref[...] = _dot(y.astype(BF16), w_ref[...])


def _inproj(x2, g, w):
    n = x2.shape[0]
    tm, tn = 512, IN_PAD // 3
    return pl.pallas_call(
        _inproj_kernel,
        out_shape=jax.ShapeDtypeStruct((n, IN_PAD), F32),
        grid=(IN_PAD // tn, n // tm),
        in_specs=[
            pl.BlockSpec((tm, D_MODEL), lambda c, i: (i, 0)),
            pl.BlockSpec((1, D_MODEL), lambda c, i: (0, 0)),
            pl.BlockSpec((D_MODEL, tn), lambda c, i: (0, c)),
        ],
        out_specs=pl.BlockSpec((tm, tn), lambda c, i: (i, c)),
        compiler_params=_cparams(("arbitrary", "arbitrary")),
        name="inproj",
    )(x2, g, w)


SB_QROWS = 512


def _sb_kernel(q_ref, k_ref, v_ref, o_ref):
    qi = pl.program_id(2)
    nq, nk = SB_QROWS, Q_BLOCK
    per = nq // nk
    scale = HEAD_DIM ** -0.5
    r = lax.broadcasted_iota(jnp.int32, (nq, nk), 0)
    c = lax.broadcasted_iota(jnp.int32, (nq, nk), 1)
    ur = lax.broadcasted_iota(jnp.int32, (nk, nk), 0)
    uc = lax.broadcasted_iota(jnp.int32, (nk, nk), 1)
    upper = (ur > uc).astype(BF16)

    for h in range(2):
        sl = slice(h * HEAD_DIM, (h + 1) * HEAD_DIM)
        qb = q_ref[0, :, sl].astype(BF16)

        def step(kb, vb, c_run, acc, before):
            z = _dot_nt(qb, kb) * scale
            soft = jnp.log1p(jnp.exp(-jnp.abs(z)))
            lsp = jnp.minimum(z, 0.0) - soft
            lsn = -jnp.maximum(z, 0.0) - soft
            if before is not None:
                lsn = jnp.where(before, lsn, 0.0)
            hi, lo = _split2(lsn)
            after = c_run + (_dot(hi, upper) + _dot(lo, upper))
            a = jnp.exp(lsp + after)
            if before is not None:
                a = jnp.where(before, a, 0.0)
            acc = acc + _dot(a.astype(BF16), vb)
            c_run = c_run + jnp.sum(lsn, axis=1, keepdims=True)
            return c_run, acc

        def load(j):
            k0 = pl.multiple_of(j * nk, nk)
            return (k_ref[0, pl.ds(k0, nk), sl].astype(BF16),
                    v_ref[0, pl.ds(k0, nk), sl].astype(BF16))

        c_run = jnp.zeros((nq, 1), F32)
        acc = jnp.zeros((nq, HEAD_DIM), F32)
        for d in range(per - 1, -1, -1):
            kb, vb = load(qi * per + d)
            c_run, acc = step(kb, vb, c_run, acc, c + d * nk < r)

        def body(n, carry):
            kb, vb = load(qi * per - 1 - n)
            return step(kb, vb, carry[0], carry[1], None)

        c_run, acc = lax.fori_loop(0, qi * per, body, (c_run, acc))
        o_ref[0, :, sl] = acc


def _sb_attention(proj3):
    b, t, _ = proj3.shape
    qb, kb, vb = C_SBQ // LANE, C_SBK // LANE, C_SBV // LANE
    return pl.pallas_call(
        _sb_kernel,
        out_shape=jax.ShapeDtypeStruct((b, t, SB_HEADS * HEAD_DIM), F32),
        grid=(b, SB_HEADS // 2, t // SB_QROWS),
        in_specs=[
            pl.BlockSpec((1, SB_QROWS, LANE), lambda bi, hp, i: (bi, i, qb + hp)),
            pl.BlockSpec((1, t, LANE), lambda bi, hp, i: (bi, 0, kb + hp)),
            pl.BlockSpec((1, t, LANE), lambda bi, hp, i: (bi, 0, vb + hp)),
        ],
        out_specs=pl.BlockSpec((1, SB_QROWS, LANE), lambda bi, hp, i: (bi, i, hp)),
        compiler_params=_cparams(("arbitrary", "arbitrary", "arbitrary")),
        name="sb_attn",
    )(proj3, proj3, proj3)


def _hgrn_kernel(q_ref, f_ref, i_ref, g_ref, lb_ref, nw_ref, o_ref, st_ref, *, n_chunks):
    ch, sub = HGRN_CHUNK, HGRN_SUB
    st_ref[...] = jnp.zeros_like(st_ref)
    lb = lb_ref[...]
    nw = nw_ref[...]
    r = lax.broadcasted_iota(jnp.int32, (ch, ch), 0)
    c = lax.broadcasted_iota(jnp.int32, (ch, ch), 1)
    lower = (r >= c).astype(BF16)
    srow = lax.broadcasted_iota(jnp.int32, (sub, HGRN_DK), 0)

    def chunk(ci, carry):
        t0 = pl.multiple_of(ci * ch, ch)
        fz = f_ref[0, pl.ds(t0, ch), :]
        qz = q_ref[0, pl.ds(t0, ch), :]
        iv = i_ref[0, pl.ds(t0, ch), :]
        gz = g_ref[0, pl.ds(t0, ch), :]
        f = lb + (1.0 - lb) * _sigmoid(fz)
        lf = jnp.log(f)
        kk = 1.0 - f
        qh = qz * _sigmoid(qz)
        hi, lo = _split2(lf)
        bcum = _dot(lower, hi) + _dot(lower, lo)
        st = st_ref[...]
        o = _dot_nt((qh * jnp.exp(bcum)).astype(BF16), st.astype(BF16))
        ivb = iv.astype(BF16)

        rows = []
        for s in range(ch // sub):
            lo_r, hi_r = s * sub, (s + 1) * sub
            qs, ks, bs, vs = qh[lo_r:hi_r], kk[lo_r:hi_r], bcum[lo_r:hi_r], iv[lo_r:hi_r]
            o_s = o[lo_r:hi_r]
            if s > 0:
                bref = bcum[lo_r - 1:lo_r]
                qd = (qs * jnp.exp(bs - bref)).astype(BF16)
                kd = (kk[:lo_r] * jnp.exp(bref - bcum[:lo_r])).astype(BF16)
                att = _dot_nt(qd, kd)
                o_s = o_s + _dot(att.astype(BF16), ivb[:lo_r])
            diag_rows = []
            for t in range(sub):
                dlt = jnp.where(srow <= t, bs[t:t + 1] - bs, NEG)
                w = (qs[t:t + 1] * ks) * jnp.exp(dlt)
                att_col = jnp.sum(w, axis=1, keepdims=True)
                diag_rows.append(jnp.sum(att_col * vs, axis=0, keepdims=True))
            rows.append(o_s + jnp.concatenate(diag_rows, axis=0))
        o = jnp.concatenate(rows, axis=0)

        b_last = bcum[ch - 1:ch]
        kd = (kk * jnp.exp(b_last - bcum)).astype(BF16)
        st_ref[...] = st * jnp.exp(b_last) + _dot_tn(ivb, kd)

        y = o * lax.rsqrt(jnp.mean(o * o, axis=-1, keepdims=True) + EPS) * nw
        o_ref[0, pl.ds(t0, ch), :] = y * (gz * _sigmoid(gz))
        return carry

    lax.fori_loop(0, n_chunks, chunk, 0)


def _hgrn(proj3, lower_l, norm_l):
    b, t, _ = proj3.shape
    cq, cf, ci, cg = C_HQ // LANE, C_HF // LANE, C_HI // LANE, C_HG // LANE
    seq = lambda col: pl.BlockSpec((1, t, LANE), lambda bi, h: (bi, 0, col + h))
    vec = pl.BlockSpec((1, LANE), lambda bi, h: (0, h))
    return pl.pallas_call(
        functools.partial(_hgrn_kernel, n_chunks=t // HGRN_CHUNK),
        out_shape=jax.ShapeDtypeStruct((b, t, HGRN_HEADS * HGRN_DK), F32),
        grid=(b, HGRN_HEADS),
        in_specs=[seq(cq), seq(cf), seq(ci), seq(cg), vec, vec],
        out_specs=pl.BlockSpec((1, t, LANE), lambda bi, h: (bi, 0, h)),
        scratch_shapes=[pltpu.VMEM((HGRN_DK, HGRN_DK), F32)],
        compiler_params=_cparams(("arbitrary", "arbitrary")),
        name="hgrn2",
    )(proj3, proj3, proj3, proj3, lower_l, norm_l)


def _cmp_kernel(p_ref, w_ref, pe_ref, o_ref):
    half = CMP_STRIDE * HEAD_DIM
    p = p_ref[0, 0, 0].astype(BF16)
    w = w_ref[0].astype(BF16)
    a = _dot(p, w[:half])
    bm = _dot(p, w[half:])
    pe = jnp.broadcast_to(pe_ref[...], (8, 2 * half)).astype(BF16)
    const = _dot(pe, w)[0:1]
    o_ref[0, 0, 0] = a + pltpu.roll(bm, bm.shape[0] - 1, 0) + const


def _nsa_compress(pieces, w_cmp, pe_flat):
    b, _, g, n_piece, width = pieces.shape
    return pl.pallas_call(
        _cmp_kernel,
        out_shape=jax.ShapeDtypeStruct((b, 2, g, n_piece, HEAD_DIM), F32),
        grid=(b, 2, g),
        in_specs=[
            pl.BlockSpec((1, 1, 1, n_piece, width), lambda bi, kv, gi: (bi, kv, gi, 0, 0)),
            pl.BlockSpec((1, CMP_LEN * HEAD_DIM, HEAD_DIM), lambda bi, kv, gi: (kv, 0, 0)),
            pl.BlockSpec((1, CMP_LEN * HEAD_DIM), lambda bi, kv, gi: (0, 0)),
        ],
        out_specs=pl.BlockSpec((1, 1, 1, n_piece, HEAD_DIM), lambda bi, kv, gi: (bi, kv, gi, 0, 0)),
        compiler_params=_cparams(("arbitrary", "arbitrary", "arbitrary")),
        name="nsa_compress",
    )(pieces, w_cmp, pe_flat)


def _nsa_kernel(q_ref, cmp_ref, ks_ref, vs_ref, kw_ref, vw_ref, g_ref, o_ref, *, seq_len):
    i = pl.program_id(1)
    qn = Q_BLOCK
    scale = HEAD_DIM ** -0.5
    n_blk = seq_len // SEL_BLOCK
    t0 = i * qn
    trow = t0 + lax.broadcasted_iota(jnp.int32, (qn, 1), 0)
    lane = lax.broadcasted_iota(jnp.int32, (qn, LANE), 1)
    lane_f = lane.astype(F32)

    dist_c = trow - (lane * CMP_STRIDE + (CMP_LEN - 1))
    valid_c = dist_c >= 0
    dist_cf = dist_c.astype(F32)
    cr = lax.broadcasted_iota(jnp.int32, (LANE, LANE), 0) * CMP_STRIDE
    nb = lax.broadcasted_iota(jnp.int32, (LANE, LANE), 1)
    overlap = ((cr < nb * SEL_BLOCK + SEL_BLOCK) & (cr + CMP_LEN > nb * SEL_BLOCK)
               & (nb < n_blk)).astype(BF16)
    forced = (lane == trow // SEL_BLOCK) | (lane == 0)
    causal_b = lane * SEL_BLOCK <= trow

    gsig = _sigmoid(g_ref[0])
    trow4 = jnp.concatenate([trow] * NSA_HPG, axis=0)

    span = WINDOW + qn
    kstart = pl.multiple_of(jnp.maximum(i - WINDOW // qn, 0) * qn, qn)
    wpos = kstart + lax.broadcasted_iota(jnp.int32, (NSA_HPG * qn, span), 1)
    dist_w = trow4 - wpos
    valid_w = (dist_w >= 0) & (dist_w < WINDOW)
    dist_wf = dist_w.astype(F32)

    for g in range(NSA_GROUPS):
        gs = slice(g * HEAD_DIM, (g + 1) * HEAD_DIM)
        kc = cmp_ref[0, 0, g].astype(BF16)
        vc = cmp_ref[0, 1, g].astype(BF16)
        q_heads = [q_ref[0, :, (g * NSA_HPG + p) * HEAD_DIM:(g * NSA_HPG + p + 1) * HEAD_DIM].astype(BF16)
                   for p in range(NSA_HPG)]
        slopes = [2.0 ** (-(g * NSA_HPG + p + 1)) for p in range(NSA_HPG)]
        slope_col = jnp.concatenate(
            [jnp.full((qn, 1), s, F32) for s in slopes], axis=0)

        psum = jnp.zeros((qn, LANE), F32)
        o_cmp = []
        for p in range(NSA_HPG):
            s = _dot_nt(q_heads[p], kc) * scale - slopes[p] * dist_cf
            s = jnp.where(valid_c, s, NEG)
            m = jnp.max(s, axis=1, keepdims=True)
            e = jnp.where(valid_c, jnp.exp(s - m), 0.0)
            den = jnp.sum(e, axis=1, keepdims=True)
            pc = e / jnp.where(den > 0.0, den, 1.0)
            psum = psum + pc
            o_cmp.append(_dot(pc.astype(BF16), vc))
        hi, lo = _split2(psum)
        imp = _dot(hi, overlap) + _dot(lo, overlap)
        imp = jnp.where(forced, FORCE_SCORE, jnp.where(causal_b, imp, NEG))
        imp = jnp.where(lane < n_blk, imp, -jnp.inf)
        sel = jnp.zeros((qn, LANE), jnp.bool_)
        for _ in range(SEL_TOPN):
            mx = jnp.max(imp, axis=1, keepdims=True)
            idx = jnp.min(jnp.where(imp == mx, lane_f, float(LANE)), axis=1, keepdims=True)
            onehot = lane_f == idx
            sel = sel | onehot
            imp = jnp.where(onehot, -jnp.inf, imp)
        sel_b = jnp.where(sel, 1.0, 0.0).astype(BF16)

        q4 = jnp.concatenate(q_heads, axis=0)

        def sel_chunk(ci, carry):
            m_run, l_run, acc = carry
            k0 = pl.multiple_of(ci * SEL_CHUNK, SEL_CHUNK)
            kk = ks_ref[0, pl.ds(k0, SEL_CHUNK), gs].astype(BF16)
            vv = vs_ref[0, pl.ds(k0, SEL_CHUNK), gs].astype(BF16)
            er = lax.broadcasted_iota(jnp.int32, (LANE, SEL_CHUNK), 0)
            ec = lax.broadcasted_iota(jnp.int32, (LANE, SEL_CHUNK), 1)
            expand = (er == ci * (SEL_CHUNK // SEL_BLOCK) + ec // SEL_BLOCK).astype(BF16)
            kpos = k0 + lax.broadcasted_iota(jnp.int32, (qn, SEL_CHUNK), 1)
            dist = trow - kpos
            mask = (_dot(sel_b, expand) > 0.5) & (dist >= 0)
            mask4 = jnp.concatenate([mask] * NSA_HPG, axis=0)
            dist4 = jnp.concatenate([dist.astype(F32)] * NSA_HPG, axis=0)
            s = _dot_nt(q4, kk) * scale - slope_col * dist4
            s = jnp.where(mask4, s, NEG)
            m_new = jnp.maximum(m_run, jnp.max(s, axis=1, keepdims=True))
            alpha = jnp.exp(m_run - m_new)
            pm = jnp.exp(s - m_new)
            l_new = alpha * l_run + jnp.sum(pm, axis=1, keepdims=True)
            acc = alpha * acc + _dot(pm.astype(BF16), vv)
            return m_new, l_new, acc

        n_sel_chunks = (t0 + qn + SEL_CHUNK - 1) // SEL_CHUNK
        init = (jnp.full((NSA_HPG * qn, 1), NEG, F32), jnp.zeros((NSA_HPG * qn, 1), F32),
                jnp.zeros((NSA_HPG * qn, HEAD_DIM), F32))
        _, l_sel, acc_sel = lax.fori_loop(0, n_sel_chunks, sel_chunk, init)
        o_sel = acc_sel / l_sel

        kw = kw_ref[0, pl.ds(kstart, span), gs].astype(BF16)
        vw = vw_ref[0, pl.ds(kstart, span), gs].astype(BF16)
        s = _dot_nt(q4, kw) * scale - slope_col * dist_wf
        s = jnp.where(valid_w, s, NEG)
        m = jnp.max(s, axis=1, keepdims=True)
        e = jnp.exp(s - m)
        pw = e / jnp.sum(e, axis=1, keepdims=True)
        o_win = _dot(pw.astype(BF16), vw)

        for p in range(NSA_HPG):
            hh = g * NSA_HPG + p
            rows = slice(p * qn, (p + 1) * qn)
            o = (gsig[:, 3 * hh:3 * hh + 1] * o_cmp[p]
                 + gsig[:, 3 * hh + 1:3 * hh + 2] * o_sel[rows]
                 + gsig[:, 3 * hh + 2:3 * hh + 3] * o_win[rows])
            o_ref[0, :, hh * HEAD_DIM:(hh + 1) * HEAD_DIM] = o


def _nsa_attention(proj3, cmp_kv):
    b, t, _ = proj3.shape
    kvb = C_NKV // LANE
    seq = lambda col: pl.BlockSpec((1, t, LANE), lambda bi, i: (bi, 0, col))
    n_piece = cmp_kv.shape[3]
    return pl.pallas_call(
        functools.partial(_nsa_kernel, seq_len=t),
        out_shape=jax.ShapeDtypeStruct((b, t, NSA_HEADS * HEAD_DIM), F32),
        grid=(b, t // Q_BLOCK),
        in_specs=[
            pl.BlockSpec((1, Q_BLOCK, NSA_HEADS * HEAD_DIM), lambda bi, i: (bi, i, C_NQ // 512)),
            pl.BlockSpec((1, 2, NSA_GROUPS, n_piece, HEAD_DIM), lambda bi, i: (bi, 0, 0, 0, 0)),
            seq(kvb + 2), seq(kvb + 3), seq(kvb + 4), seq(kvb + 5),
            pl.BlockSpec((1, Q_BLOCK, LANE), lambda bi, i: (bi, i, C_NG // LANE)),
        ],
        out_specs=pl.BlockSpec((1, Q_BLOCK, NSA_HEADS * HEAD_DIM), lambda bi, i: (bi, i, 0)),
        compiler_params=_cparams(("arbitrary", "arbitrary")),
        name="nsa_attn",
    )(proj3, cmp_kv, proj3, proj3, proj3, proj3, proj3)


def _merge_kernel(x_ref, osb_ref, onsa_ref, ohg_ref, gsb_ref, gnsa_ref, ghg_ref,
                  wsb_ref, wnsa_ref, whg_ref, wo_ref, o_ref):
    m = (_sigmoid(gsb_ref[...]) * _dot(osb_ref[...].astype(BF16), wsb_ref[...])
         + _sigmoid(gnsa_ref[...]) * _dot(onsa_ref[...].astype(BF16), wnsa_ref[...])
         + _sigmoid(ghg_ref[...]) * _dot(ohg_ref[...].astype(BF16), whg_ref[...]))
    o_ref[...] = x_ref[...] + _dot(m.astype(BF16), wo_ref[...])


def _merge(x2, o_sb, o_nsa, o_hg, proj2, w_sb, w_nsa, w_hg, w_o):
    n = x2.shape[0]
    tm = 512
    row = lambda w: pl.BlockSpec((tm, w), lambda i: (i, 0))
    gate = lambda j: pl.BlockSpec((tm, D_MODEL), lambda i: (i, j))
    full = lambda a: pl.BlockSpec(a.shape, lambda i: (0, 0))
    return pl.pallas_call(
        _merge_kernel,
        out_shape=jax.ShapeDtypeStruct((n, D_MODEL), F32),
        grid=(n // tm,),
        in_specs=[row(D_MODEL), row(512), row(512), row(512), gate(0), gate(1), gate(2),
                  full(w_sb), full(w_nsa), full(w_hg), full(w_o)],
        out_specs=row(D_MODEL),
        compiler_params=_cparams(("arbitrary",)),
        name="merge_out",
    )(x2, o_sb, o_nsa, o_hg, proj2, proj2, proj2, w_sb, w_nsa, w_hg, w_o)


def _topk_rows(s, k, payload=None):
    rows, cols = s.shape
    rid = lax.broadcasted_iota(jnp.int32, (rows, cols), 0).astype(F32)
    out_row = lax.broadcasted_iota(jnp.int32, (k, cols), 0)
    vals = jnp.zeros((k, cols), F32)
    tags = jnp.zeros((k, cols), F32)
    for j in range(k):
        mx = jnp.max(s, axis=0, keepdims=True)
        idx = jnp.min(jnp.where(s == mx, rid, float(rows)), axis=0, keepdims=True)
        onehot = rid == idx
        s = jnp.where(onehot, -jnp.inf, s)
        tag = idx if payload is None else jnp.sum(jnp.where(onehot, payload, 0.0), axis=0, keepdims=True)
        vals = jnp.where(out_row == j, mx, vals)
        tags = jnp.where(out_row == j, tag, tags)
    return vals, tags


def _route_kernel(x_ref, g_ref, wqt_ref, keys_ref, h_ref, eidx_ref, gate_ref):
    k = PEER_TOPK
    half = PEER_QDIM // 2
    x = x_ref[...]
    h = (x * lax.rsqrt(jnp.mean(x * x, axis=-1, keepdims=True) + EPS) * g_ref[...]).astype(BF16)
    h_ref[...] = h
    qt = _dot_nt(wqt_ref[...], h).astype(BF16)
    for hd in range(PEER_HEADS):
        tops = []
        for a in range(2):
            r0 = (hd * 2 + a) * half
            s = _dot(keys_ref[a].astype(BF16), qt[r0:r0 + half])
            tops.append(_topk_rows(s, k))
        (s0, i0), (s1, i1) = tops
        cand = jnp.concatenate([s0[i:i + 1] + s1 for i in range(k)], axis=0)
        cidx = jnp.concatenate([i0[i:i + 1] * float(PEER_NKEYS) + i1 for i in range(k)], axis=0)
        best, eidx = _topk_rows(cand, k, payload=cidx)
        e = jnp.exp(best - jnp.max(best, axis=0, keepdims=True))
        gate_ref[hd * k:(hd + 1) * k, :] = e / jnp.sum(e, axis=0, keepdims=True)
        eidx_ref[hd * k:(hd + 1) * k, :] = eidx.astype(jnp.int32)


def _peer_route(x2, g, w_q_t, sub_keys):
    n = x2.shape[0]
    tm = 128
    return pl.pallas_call(
        _route_kernel,
        out_shape=(jax.ShapeDtypeStruct((n, D_MODEL), BF16),
                   jax.ShapeDtypeStruct((PEER_HEADS * PEER_TOPK, n), jnp.int32),
                   jax.ShapeDtypeStruct((PEER_HEADS * PEER_TOPK, n), F32)),
        grid=(n // tm,),
        in_specs=[pl.BlockSpec((tm, D_MODEL), lambda i: (i, 0)),
                  pl.BlockSpec((1, D_MODEL), lambda i: (0, 0)),
                  pl.BlockSpec(w_q_t.shape, lambda i: (0, 0)),
                  pl.BlockSpec(sub_keys.shape, lambda i: (0, 0, 0))],
        out_specs=(pl.BlockSpec((tm, D_MODEL), lambda i: (i, 0)),
                   pl.BlockSpec((PEER_HEADS * PEER_TOPK, tm), lambda i: (0, i)),
                   pl.BlockSpec((PEER_HEADS * PEER_TOPK, tm), lambda i: (0, i))),
        compiler_params=_cparams(("arbitrary",)),
        name="peer_route",
    )(x2, g, w_q_t, sub_keys)


ROW_WORDS = D_MODEL // 2
ROW_PARTS = ROW_WORDS // LANE
GATHER_WIN = 128


def _pack_table(tab):
    bits = lax.bitcast_convert_type(tab.astype(BF16), jnp.uint16).astype(jnp.uint32)
    words = (bits[:, ROW_WORDS:] << 16) | bits[:, :ROW_WORDS]
    return lax.bitcast_convert_type(words, jnp.int32).reshape(-1, LANE)


def _sc_gather(table, idx):
    m = idx.shape[0]
    mesh = plsc.VectorSubcoreMesh(core_axis_name="core", subcore_axis_name="subcore")

    @pl.kernel(out_type=jax.ShapeDtypeStruct((m, LANE), table.dtype), mesh=mesh)
    def gather(tab_hbm, idx_hbm, out_hbm):
        def body(idx_vmem, out_vmem):
            pltpu.sync_copy(tab_hbm.at[idx_vmem.at[0]], out_vmem)

        pltpu.emit_pipeline(
            body,
            grid=(m // GATHER_WIN,),
            in_specs=[pl.BlockSpec((1, GATHER_WIN), lambda i: (0, i))],
            out_specs=[pl.BlockSpec((GATHER_WIN, LANE), lambda i: (i, 0))],
            core_axis_name=("core", "subcore"),
            dimension_semantics=(pltpu.PARALLEL,),
            trace_scopes=False,
        )(idx_hbm, out_hbm)

    return gather(table, idx.reshape(1, m))


PEER_TOK = 16
PEER_ROWS = PEER_HEADS * PEER_TOPK
HIGH_MASK = -65536


def _unpack(words):
    lo = lax.bitcast_convert_type(lax.shift_left(words, 16), F32)
    hi = lax.bitcast_convert_type(words & HIGH_MASK, F32)
    return lo, hi


def _expert_kernel(x_ref, h_ref, gt_ref, ug_ref, vg_ref, o_ref):
    nt, nr = PEER_TOK, PEER_ROWS
    h = h_ref[...].astype(F32)
    lane = lax.broadcasted_iota(jnp.int32, (nr, nt), 1)
    hpre = jnp.zeros((nr, nt), F32)
    for n in range(nt):
        s = jnp.zeros((nr, LANE), F32)
        for j in range(ROW_PARTS):
            lo, hi = _unpack(ug_ref[pl.ds((n * ROW_PARTS + j) * nr, nr), :])
            s = s + lo * h[n:n + 1, j * LANE:(j + 1) * LANE]
            s = s + hi * h[n:n + 1, ROW_WORDS + j * LANE:ROW_WORDS + (j + 1) * LANE]
        hpre = hpre + jnp.where(lane == n, jnp.sum(s, axis=1, keepdims=True), 0.0)
    act = gt_ref[0] * (0.5 * hpre * (1.0 + lax.erf(hpre * (2.0 ** -0.5))))
    rows = []
    for n in range(nt):
        a = act[:, n:n + 1]
        los, his = [], []
        for j in range(ROW_PARTS):
            lo, hi = _unpack(vg_ref[pl.ds((n * ROW_PARTS + j) * nr, nr), :])
            los.append(jnp.sum(a * lo, axis=0, keepdims=True))
            his.append(jnp.sum(a * hi, axis=0, keepdims=True))
        rows.append(jnp.concatenate(los + his, axis=1))
    o_ref[...] = x_ref[...] + jnp.concatenate(rows, axis=0)


def _peer_experts(x2, h2, gate_t, ug, vg):
    n = x2.shape[0]
    nt, nr = PEER_TOK, PEER_ROWS
    blk = nt * ROW_PARTS * nr
    return pl.pallas_call(
        _expert_kernel,
        out_shape=jax.ShapeDtypeStruct((n, D_MODEL), F32),
        grid=(n // nt,),
        in_specs=[pl.BlockSpec((nt, D_MODEL), lambda i: (i, 0)),
                  pl.BlockSpec((nt, D_MODEL), lambda i: (i, 0)),
                  pl.BlockSpec((1, nr, nt), lambda i: (i, 0, 0)),
                  pl.BlockSpec((blk, LANE), lambda i: (i, 0)),
                  pl.BlockSpec((blk, LANE), lambda i: (i, 0))],
        out_specs=pl.BlockSpec((nt, D_MODEL), lambda i: (i, 0)),
        compiler_params=_cparams(("arbitrary",)),
        name="peer_experts",
    )(x2, h2, gate_t, ug, vg)


def _norm_kernel(x_ref, g_ref, o_ref):
    x = x_ref[...]
    o_ref[...] = x * lax.rsqrt(jnp.mean(x * x, axis=-1, keepdims=True) + EPS) * g_ref[...]


def _final_norm(x2, g):
    n = x2.shape[0]
    tm = 1024
    return pl.pallas_call(
        _norm_kernel,
        out_shape=jax.ShapeDtypeStruct((n, D_MODEL), F32),
        grid=(n // tm,),
        in_specs=[pl.BlockSpec((tm, D_MODEL), lambda i: (i, 0)),
                  pl.BlockSpec((1, D_MODEL), lambda i: (0, 0))],
        out_specs=pl.BlockSpec((tm, D_MODEL), lambda i: (i, 0)),
        compiler_params=_cparams(("arbitrary",)),
        name="final_norm",
    )(x2, g)


def _permute_w_in(w_in):
    o = [0, 512, 1024, 1536, 2048, 2816, 2840, 3352, 3864, 4376, 4888, IN_WIDTH]
    sb_q, sb_k, sb_v, nsa_q, nsa_kv, nsa_g, hg_q, hg_f, hg_i, hg_g, merge_g = [
        w_in[..., o[j]:o[j + 1]] for j in range(11)]
    pad = jnp.zeros(w_in.shape[:-1] + (IN_PAD - IN_WIDTH,), w_in.dtype)
    return jnp.concatenate(
        [merge_g, sb_q, sb_k, sb_v, nsa_q, hg_q, hg_f, hg_i, hg_g, nsa_kv, nsa_g, pad], axis=-1)


def _mixer_layer(x2, bsz, seq, norm_g, w_in_p, w_cmp, pe_flat, hgrn_norm_l, lower_l,
                 w_sb, w_nsa, w_hg, w_o):
    n = bsz * seq
    proj2 = _inproj(x2, norm_g, w_in_p)
    proj3 = proj2.reshape(bsz, seq, IN_PAD)
    o_sb = _sb_attention(proj3)
    o_hg = _hgrn(proj3, lower_l, hgrn_norm_l)
    kv = proj3[:, :, C_NKV:C_NKV + 2 * LANE].reshape(bsz, seq, 2, NSA_GROUPS, HEAD_DIM)
    pieces = kv.transpose(0, 2, 3, 1, 4).reshape(
        bsz, 2, NSA_GROUPS, seq // CMP_STRIDE, CMP_STRIDE * HEAD_DIM)
    cmp_kv = _nsa_compress(pieces, w_cmp, pe_flat)
    o_nsa = _nsa_attention(proj3, cmp_kv)
    return _merge(x2, o_sb.reshape(n, -1), o_nsa.reshape(n, -1), o_hg.reshape(n, -1),
                  proj2, w_sb, w_nsa, w_hg, w_o)


def _peer_layer(x2, norm_g, w_q_t, sub_keys, u_words, v_words):
    n = x2.shape[0]
    h2, eidx_t, gate_full = _peer_route(x2, norm_g, w_q_t, sub_keys)
    gate_t = gate_full.reshape(PEER_ROWS, n // PEER_TOK, PEER_TOK).transpose(1, 0, 2)
    parts = jnp.arange(ROW_PARTS, dtype=jnp.int32)[None, :, None]
    idx = (eidx_t.T[:, None, :] * ROW_PARTS + parts).reshape(-1)
    ug = _sc_gather(u_words, idx)
    vg = _sc_gather(v_words, idx)
    return _peer_experts(x2, h2, gate_t, ug, vg)


BATCH_STREAMS = 2


def kernel(x, norm_mix, norm_ffn, w_in, nsa_w_cmp_k, nsa_w_cmp_v, nsa_cmp_pe, hgrn_norm, hgrn_lower_bounds, w_branch_sb, w_branch_nsa, w_branch_hgrn, w_out, peer_w_q, peer_sub_keys, peer_u, peer_v, norm_final):
    bsz, seq, d = x.shape
    depth = w_in.shape[0]
    lb_soft = jax.nn.softmax(hgrn_lower_bounds.astype(F32), axis=0)
    lower = jnp.cumsum(lb_soft, axis=0) - lb_soft[0]
    w_in_p = _permute_w_in(w_in).astype(BF16)
    streams = BATCH_STREAMS if bsz % BATCH_STREAMS == 0 else 1
    sb = bsz // streams
    xs = [x[s * sb:(s + 1) * sb].reshape(sb * seq, d) for s in range(streams)]
    for l in range(depth):
        w_cmp = jnp.stack([nsa_w_cmp_k[l], nsa_w_cmp_v[l]])
        pe_flat = nsa_cmp_pe[l].reshape(1, CMP_LEN * HEAD_DIM)
        w_sb, w_nsa, w_hg, w_o = (w_branch_sb[l].astype(BF16), w_branch_nsa[l].astype(BF16),
                                  w_branch_hgrn[l].astype(BF16), w_out[l].astype(BF16))
        w_q_t = peer_w_q[l].T.astype(BF16)
        u_words, v_words = _pack_table(peer_u[l]), _pack_table(peer_v[l])
        xs = [_mixer_layer(xh, sb, seq, norm_mix[l][None], w_in_p[l], w_cmp, pe_flat,
                           hgrn_norm[l][None], lower[l][None], w_sb, w_nsa, w_hg, w_o) for xh in xs]
        xs = [_peer_layer(xh, norm_ffn[l][None], w_q_t, peer_sub_keys[l], u_words, v_words)
              for xh in xs]
    outs = [_final_norm(xh, norm_final[None]).reshape(sb, seq, d) for xh in xs]
    return jnp.concatenate(outs, axis=0)
```

```python
import functools

import jax
import jax.numpy as jnp
from jax import lax
from jax.experimental import pallas as pl
from jax.experimental.pallas import tpu as pltpu
from jax.experimental.pallas import tpu_sc as plsc

F32 = jnp.float32
BF16 = jnp.bfloat16

D_MODEL = 1024
HEAD_DIM = 64
EPS = 1e-6
NEG = -1e30
FORCE_SCORE = 1e4
Q_BLOCK = 128

SB_HEADS = 8
NSA_HEADS = 8
NSA_GROUPS = 2
NSA_HPG = NSA_HEADS // NSA_GROUPS
CMP_LEN = 32
CMP_STRIDE = 16
SEL_BLOCK = 64
SEL_TOPN = 4
WINDOW = 256
HGRN_HEADS = 4
HGRN_DK = 128
HGRN_CHUNK = 64
HGRN_SUB = 16
PEER_HEADS = 8
PEER_NKEYS = 128
PEER_TOPK = 16
PEER_QDIM = 128

C_MG = 0
C_SBQ = 3072
C_SBK = 3584
C_SBV = 4096
C_NQ = 4608
C_HQ = 5120
C_HF = 5632
C_HI = 6144
C_HG = 6656
C_NKV = 7168
C_NG = 7936
IN_WIDTH = 7960
IN_PAD = 8064
LANE = 128

VMEM_LIMIT = 56 * 1024 * 1024
ROW_WORDS = D_MODEL // 2
ROW_PARTS = ROW_WORDS // LANE
PEER_TOK = 16
PEER_ROWS = PEER_HEADS * PEER_TOPK
SEL_CHUNK = 512


def _cparams(sem):
    return pltpu.CompilerParams(dimension_semantics=sem, vmem_limit_bytes=VMEM_LIMIT)


def _dot(a, b):
    return jnp.dot(a, b, preferred_element_type=F32)


def _dot_nt(a, b):
    return lax.dot_general(a, b, (((1,), (1,)), ((), ())), preferred_element_type=F32)


def _dot_tn(a, b):
    return lax.dot_general(a, b, (((0,), (0,)), ((), ())), preferred_element_type=F32)


def _split2(x):
    hi = x.astype(BF16)
    lo = (x - hi.astype(F32)).astype(BF16)
    return hi, lo


def _split3(x):
    h1 = x.astype(BF16)
    r1 = x - h1.astype(F32)
    h2 = r1.astype(BF16)
    h3 = (r1 - h2.astype(F32)).astype(BF16)
    return h1, h2, h3


def _sigmoid(x):
    return 1.0 / (1.0 + jnp.exp(-x))


def _inproj_kernel(x_ref, g_ref, w_ref, o_ref):
    x = x_ref[...]
    y = x * lax.rsqrt(jnp.mean(x * x, axis=-1, keepdims=True) + EPS) * g_ref[...]
    o_ref[...] = _dot(y.astype(BF16), w_ref[...])


def _inproj(x2, g, w):
    n = x2.shape[0]
    tm, tn = 512, IN_PAD // 3
    return pl.pallas_call(
        _inproj_kernel,
        out_shape=jax.ShapeDtypeStruct((n, IN_PAD), F32),
        grid=(IN_PAD // tn, n // tm),
        in_specs=[
            pl.BlockSpec((tm, D_MODEL), lambda c, i: (i, 0)),
            pl.BlockSpec((1, D_MODEL), lambda c, i: (0, 0)),
            pl.BlockSpec((D_MODEL, tn), lambda c, i: (0, c)),
        ],
        out_specs=pl.BlockSpec((tm, tn), lambda c, i: (i, c)),
        compiler_params=_cparams(("arbitrary", "arbitrary")),
        name="inproj",
        cost_estimate=pl.CostEstimate(
            flops=2 * n * D_MODEL * IN_PAD, transcendentals=n * (IN_PAD // tn),
            bytes_accessed=4 * n * D_MODEL * (IN_PAD // tn) + 2 * D_MODEL * IN_PAD + 4 * n * IN_PAD),
    )(x2, g, w)


SB_QROWS = 512


def _sb_kernel(q_ref, k_ref, v_ref, o_ref):
    qi = pl.program_id(2)
    nq, nk = SB_QROWS, Q_BLOCK
    per = nq // nk
    scale = HEAD_DIM ** -0.5
    r = lax.broadcasted_iota(jnp.int32, (nq, nk), 0)
    c = lax.broadcasted_iota(jnp.int32, (nq, nk), 1)
    ur = lax.broadcasted_iota(jnp.int32, (nk, nk), 0)
    uc = lax.broadcasted_iota(jnp.int32, (nk, nk), 1)
    upper = (ur > uc).astype(BF16)

    for h in range(2):
        sl = slice(h * HEAD_DIM, (h + 1) * HEAD_DIM)
        qb = q_ref[0, :, sl].astype(BF16)

        def step(kb, vb, c_run, acc, before):
            z = _dot_nt(qb, kb) * scale
            soft = jnp.log1p(jnp.exp(-jnp.abs(z)))
            lsp = jnp.minimum(z, 0.0) - soft
            lsn = -jnp.maximum(z, 0.0) - soft
            if before is not None:
                lsn = jnp.where(before, lsn, 0.0)
            hi, lo = _split2(lsn)
            after = c_run + (_dot(hi, upper) + _dot(lo, upper))
            a = jnp.exp(lsp + after)
            if before is not None:
                a = jnp.where(before, a, 0.0)
            acc = acc + _dot(a.astype(BF16), vb)
            c_run = c_run + jnp.sum(lsn, axis=1, keepdims=True)
            return c_run, acc

        def load(j):
            k0 = pl.multiple_of(j * nk, nk)
            return (k_ref[0, pl.ds(k0, nk), sl].astype(BF16),
                    v_ref[0, pl.ds(k0, nk), sl].astype(BF16))

        c_run = jnp.zeros((nq, 1), F32)
        acc = jnp.zeros((nq, HEAD_DIM), F32)
        for d in range(per - 1, -1, -1):
            kb, vb = load(qi * per + d)
            c_run, acc = step(kb, vb, c_run, acc, c + d * nk < r)

        def body(n, carry):
            kb, vb = load(qi * per - 1 - n)
            return step(kb, vb, carry[0], carry[1], None)

        c_run, acc = lax.fori_loop(0, qi * per, body, (c_run, acc))
        o_ref[0, :, sl] = acc


def _sb_attention(proj3):
    b, t, _ = proj3.shape
    qb, kb, vb = C_SBQ // LANE, C_SBK // LANE, C_SBV // LANE
    return pl.pallas_call(
        _sb_kernel,
        out_shape=jax.ShapeDtypeStruct((b, t, SB_HEADS * HEAD_DIM), F32),
        grid=(b, SB_HEADS // 2, t // SB_QROWS),
        in_specs=[
            pl.BlockSpec((1, SB_QROWS, LANE), lambda bi, hp, i: (bi, i, qb + hp)),
            pl.BlockSpec((1, t, LANE), lambda bi, hp, i: (bi, 0, kb + hp)),
            pl.BlockSpec((1, t, LANE), lambda bi, hp, i: (bi, 0, vb + hp)),
        ],
        out_specs=pl.BlockSpec((1, SB_QROWS, LANE), lambda bi, hp, i: (bi, i, hp)),
        compiler_params=_cparams(("arbitrary", "arbitrary", "arbitrary")),
        name="sb_attn",
        cost_estimate=pl.CostEstimate(
            flops=b * SB_HEADS * t * t * (2 * HEAD_DIM + 2 * Q_BLOCK),
            transcendentals=b * SB_HEADS * t * t * 3 // 2,
            bytes_accessed=4 * 4 * b * t * SB_HEADS * HEAD_DIM),
    )(proj3, proj3, proj3)


def _hgrn_kernel(q_ref, f_ref, i_ref, g_ref, lb_ref, nw_ref, o_ref, st_ref, *, n_chunks):
    ch, sub = HGRN_CHUNK, HGRN_SUB
    st_ref[...] = jnp.zeros_like(st_ref)
    lb = lb_ref[...]
    nw = nw_ref[...]
    r = lax.broadcasted_iota(jnp.int32, (ch, ch), 0)
    c = lax.broadcasted_iota(jnp.int32, (ch, ch), 1)
    lower = (r >= c).astype(BF16)
    srow = lax.broadcasted_iota(jnp.int32, (sub, HGRN_DK), 0)

    def chunk(ci, carry):
        t0 = pl.multiple_of(ci * ch, ch)
        fz = f_ref[0, pl.ds(t0, ch), :]
        qz = q_ref[0, pl.ds(t0, ch), :]
        iv = i_ref[0, pl.ds(t0, ch), :]
        gz = g_ref[0, pl.ds(t0, ch), :]
        f = lb + (1.0 - lb) * _sigmoid(fz)
        lf = jnp.log(f)
        kk = 1.0 - f
        qh = qz * _sigmoid(qz)
        hi, lo = _split2(lf)
        bcum = _dot(lower, hi) + _dot(lower, lo)
        st = st_ref[...]
        o = _dot_nt((qh * jnp.exp(bcum)).astype(BF16), st.astype(BF16))
        ivb = iv.astype(BF16)

        rows = []
        for s in range(ch // sub):
            lo_r, hi_r = s * sub, (s + 1) * sub
            qs, ks, bs, vs = qh[lo_r:hi_r], kk[lo_r:hi_r], bcum[lo_r:hi_r], iv[lo_r:hi_r]
            o_s = o[lo_r:hi_r]
            if s > 0:
                bref = bcum[lo_r - 1:lo_r]
                qd = (qs * jnp.exp(bs - bref)).astype(BF16)
                kd = (kk[:lo_r] * jnp.exp(bref - bcum[:lo_r])).astype(BF16)
                att = _dot_nt(qd, kd)
                o_s = o_s + _dot(att.astype(BF16), ivb[:lo_r])
            diag_rows = []
            for t in range(sub):
                dlt = jnp.where(srow <= t, bs[t:t + 1] - bs, NEG)
                w = (qs[t:t + 1] * ks) * jnp.exp(dlt)
                att_col = jnp.sum(w, axis=1, keepdims=True)
                diag_rows.append(jnp.sum(att_col * vs, axis=0, keepdims=True))
            rows.append(o_s + jnp.concatenate(diag_rows, axis=0))
        o = jnp.concatenate(rows, axis=0)

        b_last = bcum[ch - 1:ch]
        kd = (kk * jnp.exp(b_last - bcum)).astype(BF16)
        st_ref[...] = st * jnp.exp(b_last) + _dot_tn(ivb, kd)

        y = o * lax.rsqrt(jnp.mean(o * o, axis=-1, keepdims=True) + EPS) * nw
        o_ref[0, pl.ds(t0, ch), :] = y * (gz * _sigmoid(gz))
        return carry

    lax.fori_loop(0, n_chunks, chunk, 0)


def _hgrn(proj3, lower_l, norm_l):
    b, t, _ = proj3.shape
    cq, cf, ci, cg = C_HQ // LANE, C_HF // LANE, C_HI // LANE, C_HG // LANE
    seq = lambda col: pl.BlockSpec((1, t, LANE), lambda bi, h: (bi, 0, col + h))
    vec = pl.BlockSpec((1, LANE), lambda bi, h: (0, h))
    return pl.pallas_call(
        functools.partial(_hgrn_kernel, n_chunks=t // HGRN_CHUNK),
        out_shape=jax.ShapeDtypeStruct((b, t, HGRN_HEADS * HGRN_DK), F32),
        grid=(b, HGRN_HEADS),
        in_specs=[seq(cq), seq(cf), seq(ci), seq(cg), vec, vec],
        out_specs=pl.BlockSpec((1, t, LANE), lambda bi, h: (bi, 0, h)),
        scratch_shapes=[pltpu.VMEM((HGRN_DK, HGRN_DK), F32)],
        compiler_params=_cparams(("arbitrary", "arbitrary")),
        name="hgrn2",
        cost_estimate=pl.CostEstimate(
            flops=b * HGRN_HEADS * t * HGRN_DK * (6 * HGRN_DK + 6 * HGRN_CHUNK + 4 * HGRN_SUB),
            transcendentals=b * HGRN_HEADS * t * HGRN_DK * (8 + HGRN_SUB),
            bytes_accessed=4 * 5 * b * t * HGRN_HEADS * HGRN_DK),
    )(proj3, proj3, proj3, proj3, lower_l, norm_l)


def _cmp_kernel(x_ref, w_ref, pe_ref, o_ref):
    n_blk = o_ref.shape[3]
    w = w_ref[0].astype(BF16)
    pieces = [x_ref[0, pl.ds(r, n_blk, stride=CMP_STRIDE), :] for r in range(CMP_STRIDE)]
    for g in range(NSA_GROUPS):
        gs = slice(g * HEAD_DIM, (g + 1) * HEAD_DIM)
        first = jnp.zeros((n_blk, HEAD_DIM), F32)
        second = jnp.zeros((n_blk, HEAD_DIM), F32)
        for r in range(CMP_STRIDE):
            rows = pieces[r][:, gs]
            lo, hi = r, CMP_STRIDE + r
            first = first + _dot((rows + pe_ref[lo:lo + 1, :]).astype(BF16),
                                 w[lo * HEAD_DIM:(lo + 1) * HEAD_DIM])
            second = second + _dot((rows + pe_ref[hi:hi + 1, :]).astype(BF16),
                                   w[hi * HEAD_DIM:(hi + 1) * HEAD_DIM])
        o_ref[0, 0, g] = first + pltpu.roll(second, n_blk - 1, 0)


def _nsa_compress(proj3, w_cmp, pe):
    b, t, _ = proj3.shape
    n_blk = t // CMP_STRIDE
    kvb = C_NKV // LANE
    return pl.pallas_call(
        _cmp_kernel,
        out_shape=jax.ShapeDtypeStruct((b, 2, NSA_GROUPS, n_blk, HEAD_DIM), F32),
        grid=(b, 2),
        in_specs=[
            pl.BlockSpec((1, t, LANE), lambda bi, kv: (bi, 0, kvb + kv)),
            pl.BlockSpec((1, CMP_LEN * HEAD_DIM, HEAD_DIM), lambda bi, kv: (kv, 0, 0)),
            pl.BlockSpec((CMP_LEN, HEAD_DIM), lambda bi, kv: (0, 0)),
        ],
        out_specs=pl.BlockSpec((1, 1, NSA_GROUPS, n_blk, HEAD_DIM), lambda bi, kv: (bi, kv, 0, 0, 0)),
        compiler_params=_cparams(("arbitrary", "arbitrary")),
        name="nsa_compress",
    )(proj3, w_cmp, pe)


def _nsa_kernel(q_ref, cmp_ref, ks_ref, vs_ref, kw_ref, vw_ref, g_ref, o_ref, *, seq_len):
    i = pl.program_id(1)
    qn = Q_BLOCK
    scale = HEAD_DIM ** -0.5
    n_blk = seq_len // SEL_BLOCK
    t0 = i * qn
    trow = t0 + lax.broadcasted_iota(jnp.int32, (qn, 1), 0)
    lane = lax.broadcasted_iota(jnp.int32, (qn, LANE), 1)
    lane_f = lane.astype(F32)

    dist_c = trow - (lane * CMP_STRIDE + (CMP_LEN - 1))
    valid_c = dist_c >= 0
    dist_cf = dist_c.astype(F32)
    cr = lax.broadcasted_iota(jnp.int32, (LANE, LANE), 0) * CMP_STRIDE
    nb = lax.broadcasted_iota(jnp.int32, (LANE, LANE), 1)
    overlap = ((cr < nb * SEL_BLOCK + SEL_BLOCK) & (cr + CMP_LEN > nb * SEL_BLOCK)
               & (nb < n_blk)).astype(BF16)
    forced = (lane == trow // SEL_BLOCK) | (lane == 0)
    causal_b = lane * SEL_BLOCK <= trow

    gsig = _sigmoid(g_ref[0])
    trow4 = jnp.concatenate([trow] * NSA_HPG, axis=0)

    span = WINDOW + qn
    kstart = pl.multiple_of(jnp.maximum(i - WINDOW // qn, 0) * qn, qn)
    wpos = kstart + lax.broadcasted_iota(jnp.int32, (NSA_HPG * qn, span), 1)
    dist_w = trow4 - wpos
    valid_w = (dist_w >= 0) & (dist_w < WINDOW)
    dist_wf = dist_w.astype(F32)

    for g in range(NSA_GROUPS):
        gs = slice(g * HEAD_DIM, (g + 1) * HEAD_DIM)
        kc = cmp_ref[0, 0, g].astype(BF16)
        vc = cmp_ref[0, 1, g].astype(BF16)
        q_heads = [q_ref[0, :, (g * NSA_HPG + p) * HEAD_DIM:(g * NSA_HPG + p + 1) * HEAD_DIM].astype(BF16)
                   for p in range(NSA_HPG)]
        slopes = [2.0 ** (-(g * NSA_HPG + p + 1)) for p in range(NSA_HPG)]
        slope_col = jnp.concatenate(
            [jnp.full((qn, 1), s, F32) for s in slopes], axis=0)

        psum = jnp.zeros((qn, LANE), F32)
        o_cmp = []
        for p in range(NSA_HPG):
            s = _dot_nt(q_heads[p], kc) * scale - slopes[p] * dist_cf
            s = jnp.where(valid_c, s, NEG)
            m = jnp.max(s, axis=1, keepdims=True)
            e = jnp.where(valid_c, jnp.exp(s - m), 0.0)
            den = jnp.sum(e, axis=1, keepdims=True)
            pc = e / jnp.where(den > 0.0, den, 1.0)
            psum = psum + pc
            o_cmp.append(_dot(pc.astype(BF16), vc))
        hi, lo = _split2(psum)
        imp = _dot(hi, overlap) + _dot(lo, overlap)
        imp = jnp.where(forced, FORCE_SCORE, jnp.where(causal_b, imp, NEG))
        imp = jnp.where(lane < n_blk, imp, -jnp.inf)
        sel = jnp.zeros((qn, LANE), jnp.bool_)
        for _ in range(SEL_TOPN):
            mx = jnp.max(imp, axis=1, keepdims=True)
            idx = jnp.min(jnp.where(imp == mx, lane_f, float(LANE)), axis=1, keepdims=True)
            onehot = lane_f == idx
            sel = sel | onehot
            imp = jnp.where(onehot, -jnp.inf, imp)
        sel_b = jnp.where(sel, 1.0, 0.0).astype(BF16)

        q4 = jnp.concatenate(q_heads, axis=0)

        def sel_chunk(ci, carry):
            m_run, l_run, acc = carry
            k0 = pl.multiple_of(ci * SEL_CHUNK, SEL_CHUNK)
            kk = ks_ref[0, pl.ds(k0, SEL_CHUNK), gs].astype(BF16)
            vv = vs_ref[0, pl.ds(k0, SEL_CHUNK), gs].astype(BF16)
            er = lax.broadcasted_iota(jnp.int32, (LANE, SEL_CHUNK), 0)
            ec = lax.broadcasted_iota(jnp.int32, (LANE, SEL_CHUNK), 1)
            expand = (er == ci * (SEL_CHUNK // SEL_BLOCK) + ec // SEL_BLOCK).astype(BF16)
            kpos = k0 + lax.broadcasted_iota(jnp.int32, (qn, SEL_CHUNK), 1)
            dist = trow - kpos
            mask = (_dot(sel_b, expand) > 0.5) & (dist >= 0)
            mask4 = jnp.concatenate([mask] * NSA_HPG, axis=0)
            dist4 = jnp.concatenate([dist.astype(F32)] * NSA_HPG, axis=0)
            s = _dot_nt(q4, kk) * scale - slope_col * dist4
            s = jnp.where(mask4, s, NEG)
            m_new = jnp.maximum(m_run, jnp.max(s, axis=1, keepdims=True))
            alpha = jnp.exp(m_run - m_new)
            pm = jnp.exp(s - m_new)
            l_new = alpha * l_run + jnp.sum(pm, axis=1, keepdims=True)
            acc = alpha * acc + _dot(pm.astype(BF16), vv)
            return m_new, l_new, acc

        n_sel_chunks = (t0 + qn + SEL_CHUNK - 1) // SEL_CHUNK
        init = (jnp.full((NSA_HPG * qn, 1), NEG, F32), jnp.zeros((NSA_HPG * qn, 1), F32),
                jnp.zeros((NSA_HPG * qn, HEAD_DIM), F32))
        _, l_sel, acc_sel = lax.fori_loop(0, n_sel_chunks, sel_chunk, init)
        o_sel = acc_sel / l_sel

        kw = kw_ref[0, pl.ds(kstart, span), gs].astype(BF16)
        vw = vw_ref[0, pl.ds(kstart, span), gs].astype(BF16)
        s = _dot_nt(q4, kw) * scale - slope_col * dist_wf
        s = jnp.where(valid_w, s, NEG)
        m = jnp.max(s, axis=1, keepdims=True)
        e = jnp.exp(s - m)
        pw = e / jnp.sum(e, axis=1, keepdims=True)
        o_win = _dot(pw.astype(BF16), vw)

        for p in range(NSA_HPG):
            hh = g * NSA_HPG + p
            rows = slice(p * qn, (p + 1) * qn)
            o = (gsig[:, 3 * hh:3 * hh + 1] * o_cmp[p]
                 + gsig[:, 3 * hh + 1:3 * hh + 2] * o_sel[rows]
                 + gsig[:, 3 * hh + 2:3 * hh + 3] * o_win[rows])
            o_ref[0, :, hh * HEAD_DIM:(hh + 1) * HEAD_DIM] = o


def _nsa_attention(proj3, cmp_kv):
    b, t, _ = proj3.shape
    kvb = C_NKV // LANE
    seq = lambda col: pl.BlockSpec((1, t, LANE), lambda bi, i: (bi, 0, col))
    n_piece = cmp_kv.shape[3]
    return pl.pallas_call(
        functools.partial(_nsa_kernel, seq_len=t),
        out_shape=jax.ShapeDtypeStruct((b, t, NSA_HEADS * HEAD_DIM), F32),
        grid=(b, t // Q_BLOCK),
        in_specs=[
            pl.BlockSpec((1, Q_BLOCK, NSA_HEADS * HEAD_DIM), lambda bi, i: (bi, i, C_NQ // 512)),
            pl.BlockSpec((1, 2, NSA_GROUPS, n_piece, HEAD_DIM), lambda bi, i: (bi, 0, 0, 0, 0)),
            seq(kvb + 2), seq(kvb + 3), seq(kvb + 4), seq(kvb + 5),
            pl.BlockSpec((1, Q_BLOCK, LANE), lambda bi, i: (bi, i, C_NG // LANE)),
        ],
        out_specs=pl.BlockSpec((1, Q_BLOCK, NSA_HEADS * HEAD_DIM), lambda bi, i: (bi, i, 0)),
        compiler_params=_cparams(("arbitrary", "arbitrary")),
        name="nsa_attn",
        cost_estimate=pl.CostEstimate(
            flops=b * NSA_HEADS * t * 4 * HEAD_DIM * (t // 2 + WINDOW + Q_BLOCK + n_piece),
            transcendentals=b * NSA_HEADS * t * (t // 2 + WINDOW + Q_BLOCK + n_piece),
            bytes_accessed=4 * b * t * (2 * NSA_HEADS * HEAD_DIM + 5 * LANE)),
    )(proj3, cmp_kv, proj3, proj3, proj3, proj3, proj3)


def _merge_kernel(x_ref, osb_ref, onsa_ref, ohg_ref, gsb_ref, gnsa_ref, ghg_ref,
                  wsb_ref, wnsa_ref, whg_ref, wo_ref, o_ref):
    m = (_sigmoid(gsb_ref[...]) * _dot(osb_ref[...].astype(BF16), wsb_ref[...])
         + _sigmoid(gnsa_ref[...]) * _dot(onsa_ref[...].astype(BF16), wnsa_ref[...])
         + _sigmoid(ghg_ref[...]) * _dot(ohg_ref[...].astype(BF16), whg_ref[...]))
    o_ref[...] = x_ref[...] + _dot(m.astype(BF16), wo_ref[...])


def _merge(x2, o_sb, o_nsa, o_hg, proj2, w_sb, w_nsa, w_hg, w_o):
    n = x2.shape[0]
    tm = 512
    row = lambda w: pl.BlockSpec((tm, w), lambda i: (i, 0))
    gate = lambda j: pl.BlockSpec((tm, D_MODEL), lambda i: (i, j))
    full = lambda a: pl.BlockSpec(a.shape, lambda i: (0, 0))
    return pl.pallas_call(
        _merge_kernel,
        out_shape=jax.ShapeDtypeStruct((n, D_MODEL), F32),
        grid=(n // tm,),
        in_specs=[row(D_MODEL), row(512), row(512), row(512), gate(0), gate(1), gate(2),
                  full(w_sb), full(w_nsa), full(w_hg), full(w_o)],
        out_specs=row(D_MODEL),
        compiler_params=_cparams(("arbitrary",)),
        name="merge_out",
        cost_estimate=pl.CostEstimate(
            flops=2 * n * D_MODEL * (3 * 512 + D_MODEL), transcendentals=3 * n * D_MODEL,
            bytes_accessed=4 * n * (5 * D_MODEL + 3 * 512) + 2 * D_MODEL * (3 * 512 + D_MODEL)),
    )(x2, o_sb, o_nsa, o_hg, proj2, proj2, proj2, w_sb, w_nsa, w_hg, w_o)


def _topk_rows(s, k, payload=None):
    rows, cols = s.shape
    rid = lax.broadcasted_iota(jnp.int32, (rows, cols), 0).astype(F32)
    out_row = lax.broadcasted_iota(jnp.int32, (k, cols), 0)
    vals = jnp.zeros((k, cols), F32)
    tags = jnp.zeros((k, cols), F32)
    for j in range(k):
        mx = jnp.max(s, axis=0, keepdims=True)
        idx = jnp.min(jnp.where(s == mx, rid, float(rows)), axis=0, keepdims=True)
        onehot = rid == idx
        s = jnp.where(onehot, -jnp.inf, s)
        tag = idx if payload is None else jnp.sum(jnp.where(onehot, payload, 0.0), axis=0, keepdims=True)
        vals = jnp.where(out_row == j, mx, vals)
        tags = jnp.where(out_row == j, tag, tags)
    return vals, tags


def _route_kernel(x_ref, g_ref, wqt_ref, keys_ref, h_ref, idx_ref, gate_ref):
    k = PEER_TOPK
    half = PEER_QDIM // 2
    x = x_ref[...]
    h = (x * lax.rsqrt(jnp.mean(x * x, axis=-1, keepdims=True) + EPS) * g_ref[...]).astype(BF16)
    h_ref[...] = h
    qt = _dot_nt(wqt_ref[...], h).astype(BF16)
    gates, ids = [], []
    for hd in range(PEER_HEADS):
        tops = []
        for a in range(2):
            r0 = (hd * 2 + a) * half
            s = _dot(keys_ref[a].astype(BF16), qt[r0:r0 + half])
            tops.append(_topk_rows(s, k))
        (s0, i0), (s1, i1) = tops
        cand = jnp.concatenate([s0[i:i + 1] + s1 for i in range(k)], axis=0)
        cidx = jnp.concatenate([i0[i:i + 1] * float(PEER_NKEYS) + i1 for i in range(k)], axis=0)
        best, eidx = _topk_rows(cand, k, payload=cidx)
        e = jnp.exp(best - jnp.max(best, axis=0, keepdims=True))
        gates.append(e / jnp.sum(e, axis=0, keepdims=True))
        ids.append(eidx)
    gate_ref[...] = jnp.concatenate(gates, axis=0).T
    first_word = jnp.concatenate(ids, axis=0).T * float(ROW_PARTS)
    nt = PEER_TOK
    for t in range(x.shape[0] // nt):
        for j in range(ROW_PARTS):
            r0 = (t * ROW_PARTS + j) * nt
            idx_ref[r0:r0 + nt, :] = (first_word[t * nt:(t + 1) * nt] + float(j)).astype(jnp.int32)


def _peer_route(x2, g, w_q_t, sub_keys):
    n = x2.shape[0]
    tm = 128
    return pl.pallas_call(
        _route_kernel,
        out_shape=(jax.ShapeDtypeStruct((n, D_MODEL), BF16),
                   jax.ShapeDtypeStruct((n * ROW_PARTS, PEER_HEADS * PEER_TOPK), jnp.int32),
                   jax.ShapeDtypeStruct((n, PEER_HEADS * PEER_TOPK), F32)),
        grid=(n // tm,),
        in_specs=[pl.BlockSpec((tm, D_MODEL), lambda i: (i, 0)),
                  pl.BlockSpec((1, D_MODEL), lambda i: (0, 0)),
                  pl.BlockSpec(w_q_t.shape, lambda i: (0, 0)),
                  pl.BlockSpec(sub_keys.shape, lambda i: (0, 0, 0))],
        out_specs=(pl.BlockSpec((tm, D_MODEL), lambda i: (i, 0)),
                   pl.BlockSpec((tm * ROW_PARTS, PEER_HEADS * PEER_TOPK), lambda i: (i, 0)),
                   pl.BlockSpec((tm, PEER_HEADS * PEER_TOPK), lambda i: (i, 0))),
        compiler_params=_cparams(("arbitrary",)),
        name="peer_route",
        cost_estimate=pl.CostEstimate(
            flops=n * (2 * D_MODEL * D_MODEL + 4 * PEER_HEADS * PEER_QDIM * PEER_NKEYS
                       + 6 * PEER_HEADS * PEER_TOPK * (2 * PEER_NKEYS + PEER_TOPK * PEER_TOPK)),
            transcendentals=n * PEER_HEADS * PEER_TOPK,
            bytes_accessed=n * (6 * D_MODEL + 8 * PEER_HEADS * PEER_TOPK) + 2 * D_MODEL * D_MODEL),
    )(x2, g, w_q_t, sub_keys)


GATHER_WIN = 128


def _pack_table(tab):
    bits = lax.bitcast_convert_type(tab.astype(BF16), jnp.uint16).astype(jnp.uint32)
    words = (bits[:, ROW_WORDS:] << 16) | bits[:, :ROW_WORDS]
    return lax.bitcast_convert_type(words, jnp.int32).reshape(-1, LANE)


def _sc_gather(table, idx):
    m = idx.shape[0] * idx.shape[1]
    mesh = plsc.VectorSubcoreMesh(core_axis_name="core", subcore_axis_name="subcore")

    @pl.kernel(out_type=jax.ShapeDtypeStruct((m, LANE), table.dtype), mesh=mesh,
               cost_estimate=pl.CostEstimate(flops=0, transcendentals=0,
                                             bytes_accessed=m * (2 * LANE + 1) * 4))
    def gather(tab_hbm, idx_hbm, out_hbm):
        def body(idx_vmem, out_vmem):
            pltpu.sync_copy(tab_hbm.at[idx_vmem.at[0]], out_vmem)

        pltpu.emit_pipeline(
            body,
            grid=(m // GATHER_WIN,),
            in_specs=[pl.BlockSpec((1, GATHER_WIN), lambda i: (i, 0))],
            out_specs=[pl.BlockSpec((GATHER_WIN, LANE), lambda i: (i, 0))],
            core_axis_name=("core", "subcore"),
            dimension_semantics=(pltpu.PARALLEL,),
            trace_scopes=False,
        )(idx_hbm, out_hbm)

    return gather(table, idx)


HIGH_MASK = -65536


def _unpack(words):
    lo = lax.bitcast_convert_type(lax.shift_left(words, 16), F32)
    hi = lax.bitcast_convert_type(words & HIGH_MASK, F32)
    return lo, hi


def _expert_kernel(x_ref, h_ref, gate_ref, ug_ref, vg_ref, o_ref):
    nt, nr = PEER_TOK, PEER_ROWS
    h = h_ref[...].astype(F32)
    gate_t = jnp.concatenate([gate_ref[...], jnp.zeros((LANE - nt, nr), F32)], axis=0).T
    lane = lax.broadcasted_iota(jnp.int32, (nr, LANE), 1)
    hpre = jnp.zeros((nr, LANE), F32)
    for n in range(nt):
        s = jnp.zeros((nr, LANE), F32)
        for j in range(ROW_PARTS):
            lo, hi = _unpack(ug_ref[pl.ds((j * nt + n) * nr, nr), :])
            s = s + lo * h[n:n + 1, j * LANE:(j + 1) * LANE]
            s = s + hi * h[n:n + 1, ROW_WORDS + j * LANE:ROW_WORDS + (j + 1) * LANE]
        hpre = hpre + jnp.where(lane == n, jnp.sum(s, axis=1, keepdims=True), 0.0)
    act = gate_t * (0.5 * hpre * (1.0 + lax.erf(hpre * (2.0 ** -0.5))))
    rows = []
    for n in range(nt):
        a = act[:, n:n + 1]
        los, his = [], []
        for j in range(ROW_PARTS):
            lo, hi = _unpack(vg_ref[pl.ds((j * nt + n) * nr, nr), :])
            los.append(jnp.sum(a * lo, axis=0, keepdims=True))
            his.append(jnp.sum(a * hi, axis=0, keepdims=True))
        rows.append(jnp.concatenate(los + his, axis=1))
    o_ref[...] = x_ref[...] + jnp.concatenate(rows, axis=0)


def _peer_experts(x2, h2, gate, ug, vg):
    n = x2.shape[0]
    nt, nr = PEER_TOK, PEER_ROWS
    blk = nt * ROW_PARTS * nr
    return pl.pallas_call(
        _expert_kernel,
        out_shape=jax.ShapeDtypeStruct((n, D_MODEL), F32),
        grid=(n // nt,),
        in_specs=[pl.BlockSpec((nt, D_MODEL), lambda i: (i, 0)),
                  pl.BlockSpec((nt, D_MODEL), lambda i: (i, 0)),
                  pl.BlockSpec((nt, nr), lambda i: (i, 0)),
                  pl.BlockSpec((blk, LANE), lambda i: (i, 0)),
                  pl.BlockSpec((blk, LANE), lambda i: (i, 0))],
        out_specs=pl.BlockSpec((nt, D_MODEL), lambda i: (i, 0)),
        compiler_params=_cparams(("arbitrary",)),
        name="peer_experts",
        cost_estimate=pl.CostEstimate(
            flops=n * nr * D_MODEL * 6, transcendentals=n * nr,
            bytes_accessed=n * (2 * nr * ROW_WORDS * 4 + 10 * D_MODEL + 4 * nr)),
    )(x2, h2, gate, ug, vg)


def _norm_kernel(x_ref, g_ref, o_ref):
    x = x_ref[...]
    o_ref[...] = x * lax.rsqrt(jnp.mean(x * x, axis=-1, keepdims=True) + EPS) * g_ref[...]


def _final_norm(x2, g):
    n = x2.shape[0]
    tm = 1024
    return pl.pallas_call(
        _norm_kernel,
        out_shape=jax.ShapeDtypeStruct((n, D_MODEL), F32),
        grid=(n // tm,),
        in_specs=[pl.BlockSpec((tm, D_MODEL), lambda i: (i, 0)),
                  pl.BlockSpec((1, D_MODEL), lambda i: (0, 0))],
        out_specs=pl.BlockSpec((tm, D_MODEL), lambda i: (i, 0)),
        compiler_params=_cparams(("arbitrary",)),
        name="final_norm",
    )(x2, g)


def _permute_w_in(w_in):
    o = [0, 512, 1024, 1536, 2048, 2816, 2840, 3352, 3864, 4376, 4888, IN_WIDTH]
    sb_q, sb_k, sb_v, nsa_q, nsa_kv, nsa_g, hg_q, hg_f, hg_i, hg_g, merge_g = [
        w_in[..., o[j]:o[j + 1]] for j in range(11)]
    pad = jnp.zeros(w_in.shape[:-1] + (IN_PAD - IN_WIDTH,), w_in.dtype)
    return jnp.concatenate(
        [merge_g, sb_q, sb_k, sb_v, nsa_q, hg_q, hg_f, hg_i, hg_g, nsa_kv, nsa_g, pad], axis=-1)


def _mixer_layer(x2, bsz, seq, norm_g, w_in_p, w_cmp, pe, hgrn_norm_l, lower_l,
                 w_sb, w_nsa, w_hg, w_o):
    n = bsz * seq
    proj2 = _inproj(x2, norm_g, w_in_p)
    proj3 = proj2.reshape(bsz, seq, IN_PAD)
    o_sb = _sb_attention(proj3)
    o_hg = _hgrn(proj3, lower_l, hgrn_norm_l)
    cmp_kv = _nsa_compress(proj3, w_cmp, pe)
    o_nsa = _nsa_attention(proj3, cmp_kv)
    return _merge(x2, o_sb.reshape(n, -1), o_nsa.reshape(n, -1), o_hg.reshape(n, -1),
                  proj2, w_sb, w_nsa, w_hg, w_o)


def _peer_layer(x2, norm_g, w_q_t, sub_keys, u_words, v_words):
    h2, idx, gate = _peer_route(x2, norm_g, w_q_t, sub_keys)
    ug = _sc_gather(u_words, idx)
    vg = _sc_gather(v_words, idx)
    return _peer_experts(x2, h2, gate, ug, vg)


BATCH_STREAMS = 2


def kernel(x, norm_mix, norm_ffn, w_in, nsa_w_cmp_k, nsa_w_cmp_v, nsa_cmp_pe, hgrn_norm, hgrn_lower_bounds, w_branch_sb, w_branch_nsa, w_branch_hgrn, w_out, peer_w_q, peer_sub_keys, peer_u, peer_v, norm_final):
    bsz, seq, d = x.shape
    depth = w_in.shape[0]
    lb_soft = jax.nn.softmax(hgrn_lower_bounds.astype(F32), axis=0)
    lower = jnp.cumsum(lb_soft, axis=0) - lb_soft[0]
    w_in_p = _permute_w_in(w_in).astype(BF16)
    streams = BATCH_STREAMS if bsz % BATCH_STREAMS == 0 else 1
    sb = bsz // streams
    xs = [x[s * sb:(s + 1) * sb].reshape(sb * seq, d) for s in range(streams)]
    for l in range(depth):
        w_cmp = jnp.stack([nsa_w_cmp_k[l], nsa_w_cmp_v[l]])
        w_sb, w_nsa, w_hg, w_o = (w_branch_sb[l].astype(BF16), w_branch_nsa[l].astype(BF16),
                                  w_branch_hgrn[l].astype(BF16), w_out[l].astype(BF16))
        w_q_t = peer_w_q[l].T.astype(BF16)
        u_words, v_words = _pack_table(peer_u[l]), _pack_table(peer_v[l])
        xs = [_mixer_layer(xh, sb, seq, norm_mix[l][None], w_in_p[l], w_cmp, nsa_cmp_pe[l],
                           hgrn_norm[l][None], lower[l][None], w_sb, w_nsa, w_hg, w_o) for xh in xs]
        xs = [_peer_layer(xh, norm_ffn[l][None], w_q_t, peer_sub_keys[l], u_words, v_words)
              for xh in xs]
    outs = [_final_norm(xh, norm_final[None]).reshape(sb, seq, d) for xh in xs]
    return jnp.concatenate(outs, axis=0)
```

```python
import functools

import jax
import jax.numpy as jnp
from jax import lax
from jax.experimental import pallas as pl
from jax.experimental.pallas import tpu as pltpu
from jax.experimental.pallas import tpu_sc as plsc

F32 = jnp.float32
BF16 = jnp.bfloat16

D_MODEL = 1024
HEAD_DIM = 64
EPS = 1e-6
NEG = -1e30
FORCE_SCORE = 1e4
Q_BLOCK = 128

SB_HEADS = 8
NSA_HEADS = 8
NSA_GROUPS = 2
NSA_HPG = NSA_HEADS // NSA_GROUPS
CMP_LEN = 32
CMP_STRIDE = 16
SEL_BLOCK = 64
SEL_TOPN = 4
WINDOW = 256
HGRN_HEADS = 4
HGRN_DK = 128
HGRN_CHUNK = 64
HGRN_SUB = 16
PEER_HEADS = 8
PEER_NKEYS = 128
PEER_TOPK = 16
PEER_QDIM = 128

C_MG = 0
C_SBQ = 3072
C_SBK = 3584
C_SBV = 4096
C_NQ = 4608
C_HQ = 5120
C_HF = 5632
C_HI = 6144
C_HG = 6656
C_NKV = 7168
C_NG = 7936
IN_WIDTH = 7960
IN_PAD = 8064
LANE = 128

VMEM_LIMIT = 56 * 1024 * 1024
ROW_WORDS = D_MODEL // 2
ROW_PARTS = ROW_WORDS // LANE
PEER_TOK = 16
PEER_ROWS = PEER_HEADS * PEER_TOPK
SEL_CHUNK = 512


def _cparams(sem):
    return pltpu.CompilerParams(dimension_semantics=sem, vmem_limit_bytes=VMEM_LIMIT)


def _dot(a, b):
    return jnp.dot(a, b, preferred_element_type=F32)


def _dot_nt(a, b):
    return lax.dot_general(a, b, (((1,), (1,)), ((), ())), preferred_element_type=F32)


def _dot_tn(a, b):
    return lax.dot_general(a, b, (((0,), (0,)), ((), ())), preferred_element_type=F32)


def _split2(x):
    hi = x.astype(BF16)
    lo = (x - hi.astype(F32)).astype(BF16)
    return hi, lo


def _split3(x):
    h1 = x.astype(BF16)
    r1 = x - h1.astype(F32)
    h2 = r1.astype(BF16)
    h3 = (r1 - h2.astype(F32)).astype(BF16)
    return h1, h2, h3


def _sigmoid(x):
    return 1.0 / (1.0 + jnp.exp(-x))


def _inproj_kernel(x_ref, g_ref, w_ref, o_ref):
    x = x_ref[...]
    y = x * lax.rsqrt(jnp.mean(x * x, axis=-1, keepdims=True) + EPS) * g_ref[...]
    o_ref[...] = _dot(y.astype(BF16), w_ref[...])


def _inproj(x2, g, w):
    n = x2.shape[0]
    tm, tn = 512, IN_PAD // 3
    return pl.pallas_call(
        _inproj_kernel,
        out_shape=jax.ShapeDtypeStruct((n, IN_PAD), F32),
        grid=(IN_PAD // tn, n // tm),
        in_specs=[
            pl.BlockSpec((tm, D_MODEL), lambda c, i: (i, 0)),
            pl.BlockSpec((1, D_MODEL), lambda c, i: (0, 0)),
            pl.BlockSpec((D_MODEL, tn), lambda c, i: (0, c)),
        ],
        out_specs=pl.BlockSpec((tm, tn), lambda c, i: (i, c)),
        compiler_params=_cparams(("arbitrary", "arbitrary")),
        name="inproj",
        cost_estimate=pl.CostEstimate(
            flops=2 * n * D_MODEL * IN_PAD, transcendentals=n * (IN_PAD // tn),
            bytes_accessed=4 * n * D_MODEL * (IN_PAD // tn) + 2 * D_MODEL * IN_PAD + 4 * n * IN_PAD),
    )(x2, g, w)


SB_QROWS = 512


def _sb_kernel(q_ref, k_ref, v_ref, o_ref):
    qi = pl.program_id(2)
    nq, nk = SB_QROWS, Q_BLOCK
    per = nq // nk
    scale = HEAD_DIM ** -0.5
    r = lax.broadcasted_iota(jnp.int32, (nq, nk), 0)
    c = lax.broadcasted_iota(jnp.int32, (nq, nk), 1)
    ur = lax.broadcasted_iota(jnp.int32, (nk, nk), 0)
    uc = lax.broadcasted_iota(jnp.int32, (nk, nk), 1)
    upper = (ur > uc).astype(BF16)
    heads = [slice(h * HEAD_DIM, (h + 1) * HEAD_DIM) for h in range(LANE // HEAD_DIM)]
    qs = [q_ref[0, :, sl].astype(BF16) for sl in heads]

    def step(qb, kb, vb, c_run, acc, before):
        z = _dot_nt(qb, kb) * scale
        soft = jnp.log1p(jnp.exp(-jnp.abs(z)))
        lsp = jnp.minimum(z, 0.0) - soft
        lsn = -jnp.maximum(z, 0.0) - soft
        if before is not None:
            lsn = jnp.where(before, lsn, 0.0)
        hi, lo = _split2(lsn)
        after = c_run + (_dot(hi, upper) + _dot(lo, upper))
        a = jnp.exp(lsp + after)
        if before is not None:
            a = jnp.where(before, a, 0.0)
        acc = acc + _dot(a.astype(BF16), vb)
        c_run = c_run + jnp.sum(lsn, axis=1, keepdims=True)
        return c_run, acc

    def block(j, carry, before):
        k0 = pl.multiple_of(j * nk, nk)
        kb = k_ref[0, pl.ds(k0, nk), :].astype(BF16)
        vb = v_ref[0, pl.ds(k0, nk), :].astype(BF16)
        out = []
        for h, sl in enumerate(heads):
            out.extend(step(qs[h], kb[:, sl], vb[:, sl], carry[2 * h], carry[2 * h + 1], before))
        return tuple(out)

    carry = tuple(jnp.zeros((nq, w), F32) for _ in heads for w in (1, HEAD_DIM))
    for d in range(per - 1, -1, -1):
        carry = block(qi * per + d, carry, c + d * nk < r)
    carry = lax.fori_loop(0, qi * per, lambda n, cr: block(qi * per - 1 - n, cr, None), carry)
    for h, sl in enumerate(heads):
        o_ref[0, :, sl] = carry[2 * h + 1]


def _sb_attention(proj3):
    b, t, _ = proj3.shape
    qb, kb, vb = C_SBQ // LANE, C_SBK // LANE, C_SBV // LANE
    return pl.pallas_call(
        _sb_kernel,
        out_shape=jax.ShapeDtypeStruct((b, t, SB_HEADS * HEAD_DIM), F32),
        grid=(b, SB_HEADS // 2, t // SB_QROWS),
        in_specs=[
            pl.BlockSpec((1, SB_QROWS, LANE), lambda bi, hp, i: (bi, i, qb + hp)),
            pl.BlockSpec((1, t, LANE), lambda bi, hp, i: (bi, 0, kb + hp)),
            pl.BlockSpec((1, t, LANE), lambda bi, hp, i: (bi, 0, vb + hp)),
        ],
        out_specs=pl.BlockSpec((1, SB_QROWS, LANE), lambda bi, hp, i: (bi, i, hp)),
        compiler_params=_cparams(("arbitrary", "arbitrary", "arbitrary")),
        name="sb_attn",
        cost_estimate=pl.CostEstimate(
            flops=b * SB_HEADS * t * t * (2 * HEAD_DIM + 2 * Q_BLOCK),
            transcendentals=b * SB_HEADS * t * t * 3 // 2,
            bytes_accessed=4 * 4 * b * t * SB_HEADS * HEAD_DIM),
    )(proj3, proj3, proj3)


HGRN_PAIR = 2


def _hgrn_kernel(q_ref, f_ref, i_ref, g_ref, lb_ref, nw_ref, o_ref, st_ref, *, n_chunks):
    ch, sub = HGRN_CHUNK, HGRN_SUB
    st_ref[...] = jnp.zeros_like(st_ref)
    r = lax.broadcasted_iota(jnp.int32, (ch, ch), 0)
    c = lax.broadcasted_iota(jnp.int32, (ch, ch), 1)
    lower = (r >= c).astype(BF16)
    srow = lax.broadcasted_iota(jnp.int32, (sub, HGRN_DK), 0)

    def head_chunk(t0, hh):
        hs = slice(hh * HGRN_DK, (hh + 1) * HGRN_DK)
        lb = lb_ref[:, hs]
        nw = nw_ref[:, hs]
        fz = f_ref[0, pl.ds(t0, ch), hs]
        qz = q_ref[0, pl.ds(t0, ch), hs]
        iv = i_ref[0, pl.ds(t0, ch), hs]
        gz = g_ref[0, pl.ds(t0, ch), hs]
        f = lb + (1.0 - lb) * _sigmoid(fz)
        lf = jnp.log(f)
        kk = 1.0 - f
        qh = qz * _sigmoid(qz)
        hi, lo = _split2(lf)
        bcum = _dot(lower, hi) + _dot(lower, lo)
        st = st_ref[hh]
        o = _dot_nt((qh * jnp.exp(bcum)).astype(BF16), st.astype(BF16))
        ivb = iv.astype(BF16)

        rows = []
        for s in range(ch // sub):
            lo_r, hi_r = s * sub, (s + 1) * sub
            qs, ks, bs, vs = qh[lo_r:hi_r], kk[lo_r:hi_r], bcum[lo_r:hi_r], iv[lo_r:hi_r]
            o_s = o[lo_r:hi_r]
            if s > 0:
                bref = bcum[lo_r - 1:lo_r]
                qd = (qs * jnp.exp(bs - bref)).astype(BF16)
                kd = (kk[:lo_r] * jnp.exp(bref - bcum[:lo_r])).astype(BF16)
                att = _dot_nt(qd, kd)
                o_s = o_s + _dot(att.astype(BF16), ivb[:lo_r])
            diag_rows = []
            for t in range(sub):
                dlt = jnp.where(srow <= t, bs[t:t + 1] - bs, NEG)
                w = (qs[t:t + 1] * ks) * jnp.exp(dlt)
                att_col = jnp.sum(w, axis=1, keepdims=True)
                diag_rows.append(jnp.sum(att_col * vs, axis=0, keepdims=True))
            rows.append(o_s + jnp.concatenate(diag_rows, axis=0))
        o = jnp.concatenate(rows, axis=0)

        b_last = bcum[ch - 1:ch]
        kd = (kk * jnp.exp(b_last - bcum)).astype(BF16)
        st_ref[hh] = st * jnp.exp(b_last) + _dot_tn(ivb, kd)

        y = o * lax.rsqrt(jnp.mean(o * o, axis=-1, keepdims=True) + EPS) * nw
        o_ref[0, pl.ds(t0, ch), hs] = y * (gz * _sigmoid(gz))

    def chunk(ci, carry):
        t0 = pl.multiple_of(ci * ch, ch)
        for hh in range(HGRN_PAIR):
            head_chunk(t0, hh)
        return carry

    lax.fori_loop(0, n_chunks, chunk, 0)


def _hgrn(proj3, lower_l, norm_l):
    b, t, _ = proj3.shape
    wide = HGRN_PAIR * HGRN_DK
    cq, cf, ci, cg = C_HQ // wide, C_HF // wide, C_HI // wide, C_HG // wide
    seq = lambda col: pl.BlockSpec((1, t, wide), lambda bi, h: (bi, 0, col + h))
    vec = pl.BlockSpec((1, wide), lambda bi, h: (0, h))
    return pl.pallas_call(
        functools.partial(_hgrn_kernel, n_chunks=t // HGRN_CHUNK),
        out_shape=jax.ShapeDtypeStruct((b, t, HGRN_HEADS * HGRN_DK), F32),
        grid=(b, HGRN_HEADS // HGRN_PAIR),
        in_specs=[seq(cq), seq(cf), seq(ci), seq(cg), vec, vec],
        out_specs=pl.BlockSpec((1, t, wide), lambda bi, h: (bi, 0, h)),
        scratch_shapes=[pltpu.VMEM((HGRN_PAIR, HGRN_DK, HGRN_DK), F32)],
        compiler_params=_cparams(("arbitrary", "arbitrary")),
        name="hgrn2",
        cost_estimate=pl.CostEstimate(
            flops=b * HGRN_HEADS * t * HGRN_DK * (6 * HGRN_DK + 6 * HGRN_CHUNK + 4 * HGRN_SUB),
            transcendentals=b * HGRN_HEADS * t * HGRN_DK * (8 + HGRN_SUB),
            bytes_accessed=4 * 5 * b * t * HGRN_HEADS * HGRN_DK),
    )(proj3, proj3, proj3, proj3, lower_l, norm_l)


def _cmp_kernel(x_ref, w_ref, pe_ref, o_ref):
    n_blk = o_ref.shape[3]
    w = w_ref[0].astype(BF16)
    pieces = [x_ref[0, pl.ds(r, n_blk, stride=CMP_STRIDE), :] for r in range(CMP_STRIDE)]
    for g in range(NSA_GROUPS):
        gs = slice(g * HEAD_DIM, (g + 1) * HEAD_DIM)
        first = jnp.zeros((n_blk, HEAD_DIM), F32)
        second = jnp.zeros((n_blk, HEAD_DIM), F32)
        for r in range(CMP_STRIDE):
            rows = pieces[r][:, gs]
            lo, hi = r, CMP_STRIDE + r
            first = first + _dot((rows + pe_ref[lo:lo + 1, :]).astype(BF16),
                                 w[lo * HEAD_DIM:(lo + 1) * HEAD_DIM])
            second = second + _dot((rows + pe_ref[hi:hi + 1, :]).astype(BF16),
                                   w[hi * HEAD_DIM:(hi + 1) * HEAD_DIM])
        o_ref[0, 0, g] = first + pltpu.roll(second, n_blk - 1, 0)


def _nsa_compress(proj3, w_cmp, pe):
    b, t, _ = proj3.shape
    n_blk = t // CMP_STRIDE
    kvb = C_NKV // LANE
    return pl.pallas_call(
        _cmp_kernel,
        out_shape=jax.ShapeDtypeStruct((b, 2, NSA_GROUPS, n_blk, HEAD_DIM), F32),
        grid=(b, 2),
        in_specs=[
            pl.BlockSpec((1, t, LANE), lambda bi, kv: (bi, 0, kvb + kv)),
            pl.BlockSpec((1, CMP_LEN * HEAD_DIM, HEAD_DIM), lambda bi, kv: (kv, 0, 0)),
            pl.BlockSpec((CMP_LEN, HEAD_DIM), lambda bi, kv: (0, 0)),
        ],
        out_specs=pl.BlockSpec((1, 1, NSA_GROUPS, n_blk, HEAD_DIM), lambda bi, kv: (bi, kv, 0, 0, 0)),
        compiler_params=_cparams(("arbitrary", "arbitrary")),
        name="nsa_compress",
    )(proj3, w_cmp, pe)


def _nsa_kernel(q_ref, cmp_ref, ks_ref, vs_ref, kw_ref, vw_ref, g_ref, o_ref, *, seq_len):
    i = pl.program_id(1)
    qn = Q_BLOCK
    scale = HEAD_DIM ** -0.5
    n_blk = seq_len // SEL_BLOCK
    t0 = i * qn
    trow = t0 + lax.broadcasted_iota(jnp.int32, (qn, 1), 0)
    lane = lax.broadcasted_iota(jnp.int32, (qn, LANE), 1)
    lane_f = lane.astype(F32)

    dist_c = trow - (lane * CMP_STRIDE + (CMP_LEN - 1))
    valid_c = dist_c >= 0
    dist_cf = dist_c.astype(F32)
    cr = lax.broadcasted_iota(jnp.int32, (LANE, LANE), 0) * CMP_STRIDE
    nb = lax.broadcasted_iota(jnp.int32, (LANE, LANE), 1)
    overlap = ((cr < nb * SEL_BLOCK + SEL_BLOCK) & (cr + CMP_LEN > nb * SEL_BLOCK)
               & (nb < n_blk)).astype(BF16)
    forced = (lane == trow // SEL_BLOCK) | (lane == 0)
    causal_b = lane * SEL_BLOCK <= trow

    gsig = _sigmoid(g_ref[0])
    trow4 = jnp.concatenate([trow] * NSA_HPG, axis=0)

    span = WINDOW + qn
    kstart = pl.multiple_of(jnp.maximum(i - WINDOW // qn, 0) * qn, qn)
    wpos = kstart + lax.broadcasted_iota(jnp.int32, (NSA_HPG * qn, span), 1)
    dist_w = trow4 - wpos
    valid_w = (dist_w >= 0) & (dist_w < WINDOW)
    dist_wf = dist_w.astype(F32)

    for g in range(NSA_GROUPS):
        gs = slice(g * HEAD_DIM, (g + 1) * HEAD_DIM)
        kc = cmp_ref[0, 0, g].astype(BF16)
        vc = cmp_ref[0, 1, g].astype(BF16)
        q_heads = [q_ref[0, :, (g * NSA_HPG + p) * HEAD_DIM:(g * NSA_HPG + p + 1) * HEAD_DIM].astype(BF16)
                   for p in range(NSA_HPG)]
        slopes = [2.0 ** (-(g * NSA_HPG + p + 1)) for p in range(NSA_HPG)]
        slope_col = jnp.concatenate(
            [jnp.full((qn, 1), s, F32) for s in slopes], axis=0)

        psum = jnp.zeros((qn, LANE), F32)
        o_cmp = []
        for p in range(NSA_HPG):
            s = _dot_nt(q_heads[p], kc) * scale - slopes[p] * dist_cf
            s = jnp.where(valid_c, s, NEG)
            m = jnp.max(s, axis=1, keepdims=True)
            e = jnp.where(valid_c, jnp.exp(s - m), 0.0)
            den = jnp.sum(e, axis=1, keepdims=True)
            pc = e / jnp.where(den > 0.0, den, 1.0)
            psum = psum + pc
            o_cmp.append(_dot(pc.astype(BF16), vc))
        hi, lo = _split2(psum)
        imp = _dot(hi, overlap) + _dot(lo, overlap)
        imp = jnp.where(forced, FORCE_SCORE, jnp.where(causal_b, imp, NEG))
        imp = jnp.where(lane < n_blk, imp, -jnp.inf)
        sel = jnp.zeros((qn, LANE), jnp.bool_)
        for _ in range(SEL_TOPN):
            mx = jnp.max(imp, axis=1, keepdims=True)
            idx = jnp.min(jnp.where(imp == mx, lane_f, float(LANE)), axis=1, keepdims=True)
            onehot = lane_f == idx
            sel = sel | onehot
            imp = jnp.where(onehot, -jnp.inf, imp)
        sel_b = jnp.where(sel, 1.0, 0.0).astype(BF16)

        q4 = jnp.concatenate(q_heads, axis=0)

        def sel_chunk(ci, carry):
            m_run, l_run, acc = carry
            k0 = pl.multiple_of(ci * SEL_CHUNK, SEL_CHUNK)
            kk = ks_ref[0, pl.ds(k0, SEL_CHUNK), gs].astype(BF16)
            vv = vs_ref[0, pl.ds(k0, SEL_CHUNK), gs].astype(BF16)
            er = lax.broadcasted_iota(jnp.int32, (LANE, SEL_CHUNK), 0)
            ec = lax.broadcasted_iota(jnp.int32, (LANE, SEL_CHUNK), 1)
            expand = (er == ci * (SEL_CHUNK // SEL_BLOCK) + ec // SEL_BLOCK).astype(BF16)
            kpos = k0 + lax.broadcasted_iota(jnp.int32, (qn, SEL_CHUNK), 1)
            dist = trow - kpos
            mask = (_dot(sel_b, expand) > 0.5) & (dist >= 0)
            mask4 = jnp.concatenate([mask] * NSA_HPG, axis=0)
            dist4 = jnp.concatenate([dist.astype(F32)] * NSA_HPG, axis=0)
            s = _dot_nt(q4, kk) * scale - slope_col * dist4
            s = jnp.where(mask4, s, NEG)
            m_new = jnp.maximum(m_run, jnp.max(s, axis=1, keepdims=True))
            alpha = jnp.exp(m_run - m_new)
            pm = jnp.exp(s - m_new)
            l_new = alpha * l_run + jnp.sum(pm, axis=1, keepdims=True)
            acc = alpha * acc + _dot(pm.astype(BF16), vv)
            return m_new, l_new, acc

        n_sel_chunks = (t0 + qn + SEL_CHUNK - 1) // SEL_CHUNK
        init = (jnp.full((NSA_HPG * qn, 1), NEG, F32), jnp.zeros((NSA_HPG * qn, 1), F32),
                jnp.zeros((NSA_HPG * qn, HEAD_DIM), F32))
        _, l_sel, acc_sel = lax.fori_loop(0, n_sel_chunks, sel_chunk, init)
        o_sel = acc_sel / l_sel

        kw = kw_ref[0, pl.ds(kstart, span), gs].astype(BF16)
        vw = vw_ref[0, pl.ds(kstart, span), gs].astype(BF16)
        s = _dot_nt(q4, kw) * scale - slope_col * dist_wf
        s = jnp.where(valid_w, s, NEG)
        m = jnp.max(s, axis=1, keepdims=True)
        e = jnp.exp(s - m)
        pw = e / jnp.sum(e, axis=1, keepdims=True)
        o_win = _dot(pw.astype(BF16), vw)

        for p in range(NSA_HPG):
            hh = g * NSA_HPG + p
            rows = slice(p * qn, (p + 1) * qn)
            o = (gsig[:, 3 * hh:3 * hh + 1] * o_cmp[p]
                 + gsig[:, 3 * hh + 1:3 * hh + 2] * o_sel[rows]
                 + gsig[:, 3 * hh + 2:3 * hh + 3] * o_win[rows])
            o_ref[0, :, hh * HEAD_DIM:(hh + 1) * HEAD_DIM] = o


def _nsa_attention(proj3, cmp_kv):
    b, t, _ = proj3.shape
    kvb = C_NKV // LANE
    seq = lambda col: pl.BlockSpec((1, t, LANE), lambda bi, i: (bi, 0, col))
    n_piece = cmp_kv.shape[3]
    return pl.pallas_call(
        functools.partial(_nsa_kernel, seq_len=t),
        out_shape=jax.ShapeDtypeStruct((b, t, NSA_HEADS * HEAD_DIM), F32),
        grid=(b, t // Q_BLOCK),
        in_specs=[
            pl.BlockSpec((1, Q_BLOCK, NSA_HEADS * HEAD_DIM), lambda bi, i: (bi, i, C_NQ // 512)),
            pl.BlockSpec((1, 2, NSA_GROUPS, n_piece, HEAD_DIM), lambda bi, i: (bi, 0, 0, 0, 0)),
            seq(kvb + 2), seq(kvb + 3), seq(kvb + 4), seq(kvb + 5),
            pl.BlockSpec((1, Q_BLOCK, LANE), lambda bi, i: (bi, i, C_NG // LANE)),
        ],
        out_specs=pl.BlockSpec((1, Q_BLOCK, NSA_HEADS * HEAD_DIM), lambda bi, i: (bi, i, 0)),
        compiler_params=_cparams(("arbitrary", "arbitrary")),
        name="nsa_attn",
        cost_estimate=pl.CostEstimate(
            flops=b * NSA_HEADS * t * 4 * HEAD_DIM * (t // 2 + WINDOW + Q_BLOCK + n_piece),
            transcendentals=b * NSA_HEADS * t * (t // 2 + WINDOW + Q_BLOCK + n_piece),
            bytes_accessed=4 * b * t * (2 * NSA_HEADS * HEAD_DIM + 5 * LANE)),
    )(proj3, cmp_kv, proj3, proj3, proj3, proj3, proj3)


def _merge_kernel(x_ref, osb_ref, onsa_ref, ohg_ref, gsb_ref, gnsa_ref, ghg_ref,
                  wsb_ref, wnsa_ref, whg_ref, wo_ref, o_ref):
    m = (_sigmoid(gsb_ref[...]) * _dot(osb_ref[...].astype(BF16), wsb_ref[...])
         + _sigmoid(gnsa_ref[...]) * _dot(onsa_ref[...].astype(BF16), wnsa_ref[...])
         + _sigmoid(ghg_ref[...]) * _dot(ohg_ref[...].astype(BF16), whg_ref[...]))
    o_ref[...] = x_ref[...] + _dot(m.astype(BF16), wo_ref[...])


def _merge(x2, o_sb, o_nsa, o_hg, proj2, w_sb, w_nsa, w_hg, w_o):
    n = x2.shape[0]
    tm = 512
    row = lambda w: pl.BlockSpec((tm, w), lambda i: (i, 0))
    gate = lambda j: pl.BlockSpec((tm, D_MODEL), lambda i: (i, j))
    full = lambda a: pl.BlockSpec(a.shape, lambda i: (0, 0))
    return pl.pallas_call(
        _merge_kernel,
        out_shape=jax.ShapeDtypeStruct((n, D_MODEL), F32),
        grid=(n // tm,),
        in_specs=[row(D_MODEL), row(512), row(512), row(512), gate(0), gate(1), gate(2),
                  full(w_sb), full(w_nsa), full(w_hg), full(w_o)],
        out_specs=row(D_MODEL),
        compiler_params=_cparams(("arbitrary",)),
        name="merge_out",
        cost_estimate=pl.CostEstimate(
            flops=2 * n * D_MODEL * (3 * 512 + D_MODEL), transcendentals=3 * n * D_MODEL,
            bytes_accessed=4 * n * (5 * D_MODEL + 3 * 512) + 2 * D_MODEL * (3 * 512 + D_MODEL)),
    )(x2, o_sb, o_nsa, o_hg, proj2, proj2, proj2, w_sb, w_nsa, w_hg, w_o)


def _topk_rows(s, k, payload=None, order=None):
    rows, cols = s.shape
    if order is None:
        order = lax.broadcasted_iota(jnp.int32, (rows, cols), 0).astype(F32)
    out_row = lax.broadcasted_iota(jnp.int32, (k, cols), 0)
    vals = jnp.zeros((k, cols), F32)
    tags = jnp.zeros((k, cols), F32)
    for j in range(k):
        mx = jnp.max(s, axis=0, keepdims=True)
        idx = jnp.min(jnp.where(s == mx, order, jnp.inf), axis=0, keepdims=True)
        onehot = order == idx
        s = jnp.where(onehot, -jnp.inf, s)
        tag = idx if payload is None else jnp.sum(jnp.where(onehot, payload, 0.0), axis=0, keepdims=True)
        vals = jnp.where(out_row == j, mx, vals)
        tags = jnp.where(out_row == j, tag, tags)
    return vals, tags


def _candidate_pairs(k):
    return [(i, j) for i in range(k) for j in range(k) if (i + 1) * (j + 1) <= k]


def _route_kernel(x_ref, g_ref, wqt_ref, keys_ref, h_ref, idx_ref, gate_ref):
    k = PEER_TOPK
    half = PEER_QDIM // 2
    x = x_ref[...]
    h = (x * lax.rsqrt(jnp.mean(x * x, axis=-1, keepdims=True) + EPS) * g_ref[...]).astype(BF16)
    h_ref[...] = h
    qt = _dot_nt(wqt_ref[...], h).astype(BF16)
    gates, ids = [], []
    pairs = _candidate_pairs(k)
    n_pad = -len(pairs) % 8
    tokens = x.shape[0]
    pad_val = jnp.full((n_pad, tokens), -jnp.inf, F32)
    pad_idx = jnp.zeros((n_pad, tokens), F32)
    flat = jnp.concatenate(
        [jnp.full((1, tokens), float(i * k + j), F32) for i, j in pairs]
        + [jnp.full((n_pad, tokens), float(k * k), F32)], axis=0)
    for hd in range(PEER_HEADS):
        tops = []
        for a in range(2):
            r0 = (hd * 2 + a) * half
            s = _dot(keys_ref[a].astype(BF16), qt[r0:r0 + half])
            tops.append(_topk_rows(s, k))
        (s0, i0), (s1, i1) = tops
        cand = jnp.concatenate([s0[i:i + 1] + s1[j:j + 1] for i, j in pairs] + [pad_val], axis=0)
        cidx = jnp.concatenate([i0[i:i + 1] * float(PEER_NKEYS) + i1[j:j + 1] for i, j in pairs]
                               + [pad_idx], axis=0)
        best, eidx = _topk_rows(cand, k, payload=cidx, order=flat)
        e = jnp.exp(best - jnp.max(best, axis=0, keepdims=True))
        gates.append(e / jnp.sum(e, axis=0, keepdims=True))
        ids.append(eidx)
    gate_ref[...] = jnp.concatenate(gates, axis=0).T
    first_word = jnp.concatenate(ids, axis=0).T * float(ROW_PARTS)
    nt = PEER_TOK
    for t in range(x.shape[0] // nt):
        for j in range(ROW_PARTS):
            r0 = (t * ROW_PARTS + j) * nt
            idx_ref[r0:r0 + nt, :] = (first_word[t * nt:(t + 1) * nt] + float(j)).astype(jnp.int32)


def _peer_route(x2, g, w_q_t, sub_keys):
    n = x2.shape[0]
    tm = 128
    return pl.pallas_call(
        _route_kernel,
        out_shape=(jax.ShapeDtypeStruct((n, D_MODEL), BF16),
                   jax.ShapeDtypeStruct((n * ROW_PARTS, PEER_HEADS * PEER_TOPK), jnp.int32),
                   jax.ShapeDtypeStruct((n, PEER_HEADS * PEER_TOPK), F32)),
        grid=(n // tm,),
        in_specs=[pl.BlockSpec((tm, D_MODEL), lambda i: (i, 0)),
                  pl.BlockSpec((1, D_MODEL), lambda i: (0, 0)),
                  pl.BlockSpec(w_q_t.shape, lambda i: (0, 0)),
                  pl.BlockSpec(sub_keys.shape, lambda i: (0, 0, 0))],
        out_specs=(pl.BlockSpec((tm, D_MODEL), lambda i: (i, 0)),
                   pl.BlockSpec((tm * ROW_PARTS, PEER_HEADS * PEER_TOPK), lambda i: (i, 0)),
                   pl.BlockSpec((tm, PEER_HEADS * PEER_TOPK), lambda i: (i, 0))),
        compiler_params=_cparams(("arbitrary",)),
        name="peer_route",
        cost_estimate=pl.CostEstimate(
            flops=n * (2 * D_MODEL * D_MODEL + 4 * PEER_HEADS * PEER_QDIM * PEER_NKEYS
                       + 6 * PEER_HEADS * PEER_TOPK * (2 * PEER_NKEYS + PEER_TOPK * PEER_TOPK)),
            transcendentals=n * PEER_HEADS * PEER_TOPK,
            bytes_accessed=n * (6 * D_MODEL + 8 * PEER_HEADS * PEER_TOPK) + 2 * D_MODEL * D_MODEL),
    )(x2, g, w_q_t, sub_keys)


GATHER_WIN = 128


def _pack_table(tab):
    bits = lax.bitcast_convert_type(tab.astype(BF16), jnp.uint16).astype(jnp.uint32)
    words = (bits[:, ROW_WORDS:] << 16) | bits[:, :ROW_WORDS]
    return lax.bitcast_convert_type(words, jnp.int32).reshape(-1, LANE)


def _sc_gather(table, idx):
    m = idx.shape[0] * idx.shape[1]
    mesh = plsc.VectorSubcoreMesh(core_axis_name="core", subcore_axis_name="subcore")

    @pl.kernel(out_type=jax.ShapeDtypeStruct((m, LANE), table.dtype), mesh=mesh,
               cost_estimate=pl.CostEstimate(flops=0, transcendentals=0,
                                             bytes_accessed=m * (2 * LANE + 1) * 4))
    def gather(tab_hbm, idx_hbm, out_hbm):
        def body(idx_vmem, out_vmem):
            pltpu.sync_copy(tab_hbm.at[idx_vmem.at[0]], out_vmem)

        pltpu.emit_pipeline(
            body,
            grid=(m // GATHER_WIN,),
            in_specs=[pl.BlockSpec((1, GATHER_WIN), lambda i: (i, 0))],
            out_specs=[pl.BlockSpec((GATHER_WIN, LANE), lambda i: (i, 0))],
            core_axis_name=("core", "subcore"),
            dimension_semantics=(pltpu.PARALLEL,),
            trace_scopes=False,
        )(idx_hbm, out_hbm)

    return gather(table, idx)


HIGH_MASK = -65536


def _unpack(words):
    lo = lax.bitcast_convert_type(lax.shift_left(words, 16), F32)
    hi = lax.bitcast_convert_type(words & HIGH_MASK, F32)
    return lo, hi


def _expert_kernel(x_ref, h_ref, gate_ref, ug_ref, vg_ref, o_ref):
    nt, nr = PEER_TOK, PEER_ROWS
    h = h_ref[...].astype(F32)
    gate_t = jnp.concatenate([gate_ref[...], jnp.zeros((LANE - nt, nr), F32)], axis=0).T
    lane = lax.broadcasted_iota(jnp.int32, (nr, LANE), 1)
    hpre = jnp.zeros((nr, LANE), F32)
    for n in range(nt):
        s = jnp.zeros((nr, LANE), F32)
        for j in range(ROW_PARTS):
            lo, hi = _unpack(ug_ref[pl.ds((j * nt + n) * nr, nr), :])
            s = s + lo * h[n:n + 1, j * LANE:(j + 1) * LANE]
            s = s + hi * h[n:n + 1, ROW_WORDS + j * LANE:ROW_WORDS + (j + 1) * LANE]
        hpre = hpre + jnp.where(lane == n, jnp.sum(s, axis=1, keepdims=True), 0.0)
    act = gate_t * (0.5 * hpre * (1.0 + lax.erf(hpre * (2.0 ** -0.5))))
    rows = []
    for n in range(nt):
        a = act[:, n:n + 1]
        los, his = [], []
        for j in range(ROW_PARTS):
            lo, hi = _unpack(vg_ref[pl.ds((j * nt + n) * nr, nr), :])
            los.append(jnp.sum(a * lo, axis=0, keepdims=True))
            his.append(jnp.sum(a * hi, axis=0, keepdims=True))
        rows.append(jnp.concatenate(los + his, axis=1))
    o_ref[...] = x_ref[...] + jnp.concatenate(rows, axis=0)


def _peer_experts(x2, h2, gate, ug, vg):
    n = x2.shape[0]
    nt, nr = PEER_TOK, PEER_ROWS
    blk = nt * ROW_PARTS * nr
    return pl.pallas_call(
        _expert_kernel,
        out_shape=jax.ShapeDtypeStruct((n, D_MODEL), F32),
        grid=(n // nt,),
        in_specs=[pl.BlockSpec((nt, D_MODEL), lambda i: (i, 0)),
                  pl.BlockSpec((nt, D_MODEL), lambda i: (i, 0)),
                  pl.BlockSpec((nt, nr), lambda i: (i, 0)),
                  pl.BlockSpec((blk, LANE), lambda i: (i, 0)),
                  pl.BlockSpec((blk, LANE), lambda i: (i, 0))],
        out_specs=pl.BlockSpec((nt, D_MODEL), lambda i: (i, 0)),
        compiler_params=_cparams(("arbitrary",)),
        name="peer_experts",
        cost_estimate=pl.CostEstimate(
            flops=n * nr * D_MODEL * 6, transcendentals=n * nr,
            bytes_accessed=n * (2 * nr * ROW_WORDS * 4 + 10 * D_MODEL + 4 * nr)),
    )(x2, h2, gate, ug, vg)


def _norm_kernel(x_ref, g_ref, o_ref):
    x = x_ref[...]
    o_ref[...] = x * lax.rsqrt(jnp.mean(x * x, axis=-1, keepdims=True) + EPS) * g_ref[...]


def _final_norm(x2, g):
    n = x2.shape[0]
    tm = 1024
    return pl.pallas_call(
        _norm_kernel,
        out_shape=jax.ShapeDtypeStruct((n, D_MODEL), F32),
        grid=(n // tm,),
        in_specs=[pl.BlockSpec((tm, D_MODEL), lambda i: (i, 0)),
                  pl.BlockSpec((1, D_MODEL), lambda i: (0, 0))],
        out_specs=pl.BlockSpec((tm, D_MODEL), lambda i: (i, 0)),
        compiler_params=_cparams(("arbitrary",)),
        name="final_norm",
    )(x2, g)


def _permute_w_in(w_in):
    o = [0, 512, 1024, 1536, 2048, 2816, 2840, 3352, 3864, 4376, 4888, IN_WIDTH]
    sb_q, sb_k, sb_v, nsa_q, nsa_kv, nsa_g, hg_q, hg_f, hg_i, hg_g, merge_g = [
        w_in[..., o[j]:o[j + 1]] for j in range(11)]
    pad = jnp.zeros(w_in.shape[:-1] + (IN_PAD - IN_WIDTH,), w_in.dtype)
    return jnp.concatenate(
        [merge_g, sb_q, sb_k, sb_v, nsa_q, hg_q, hg_f, hg_i, hg_g, nsa_kv, nsa_g, pad], axis=-1)


def _mixer_layer(x2, bsz, seq, norm_g, w_in_p, w_cmp, pe, hgrn_norm_l, lower_l,
                 w_sb, w_nsa, w_hg, w_o):
    n = bsz * seq
    proj2 = _inproj(x2, norm_g, w_in_p)
    proj3 = proj2.reshape(bsz, seq, IN_PAD)
    o_sb = _sb_attention(proj3)
    o_hg = _hgrn(proj3, lower_l, hgrn_norm_l)
    cmp_kv = _nsa_compress(proj3, w_cmp, pe)
    o_nsa = _nsa_attention(proj3, cmp_kv)
    return _merge(x2, o_sb.reshape(n, -1), o_nsa.reshape(n, -1), o_hg.reshape(n, -1),
                  proj2, w_sb, w_nsa, w_hg, w_o)


def _peer_layer(x2, norm_g, w_q_t, sub_keys, u_words, v_words):
    h2, idx, gate = _peer_route(x2, norm_g, w_q_t, sub_keys)
    ug = _sc_gather(u_words, idx)
    vg = _sc_gather(v_words, idx)
    return _peer_experts(x2, h2, gate, ug, vg)


BATCH_STREAMS = 2


def kernel(x, norm_mix, norm_ffn, w_in, nsa_w_cmp_k, nsa_w_cmp_v, nsa_cmp_pe, hgrn_norm, hgrn_lower_bounds, w_branch_sb, w_branch_nsa, w_branch_hgrn, w_out, peer_w_q, peer_sub_keys, peer_u, peer_v, norm_final):
    bsz, seq, d = x.shape
    depth = w_in.shape[0]
    lb_soft = jax.nn.softmax(hgrn_lower_bounds.astype(F32), axis=0)
    lower = jnp.cumsum(lb_soft, axis=0) - lb_soft[0]
    w_in_p = _permute_w_in(w_in).astype(BF16)
    streams = BATCH_STREAMS if bsz % BATCH_STREAMS == 0 else 1
    sb = bsz // streams
    xs = [x[s * sb:(s + 1) * sb].reshape(sb * seq, d) for s in range(streams)]
    for l in range(depth):
        w_cmp = jnp.stack([nsa_w_cmp_k[l], nsa_w_cmp_v[l]])
        w_sb, w_nsa, w_hg, w_o = (w_branch_sb[l].astype(BF16), w_branch_nsa[l].astype(BF16),
                                  w_branch_hgrn[l].astype(BF16), w_out[l].astype(BF16))
        w_q_t = peer_w_q[l].T.astype(BF16)
        u_words, v_words = _pack_table(peer_u[l]), _pack_table(peer_v[l])
        xs = [_mixer_layer(xh, sb, seq, norm_mix[l][None], w_in_p[l], w_cmp, nsa_cmp_pe[l],
                           hgrn_norm[l][None], lower[l][None], w_sb, w_nsa, w_hg, w_o) for xh in xs]
        xs = [_peer_layer(xh, norm_ffn[l][None], w_q_t, peer_sub_keys[l], u_words, v_words)
              for xh in xs]
    outs = [_final_norm(xh, norm_final[None]).reshape(sb, seq, d) for xh in xs]
    return jnp.concatenate(outs, axis=0)
```

```python
import functools

import jax
import jax.numpy as jnp
from jax import lax
from jax.experimental import pallas as pl
from jax.experimental.pallas import tpu as pltpu
from jax.experimental.pallas import tpu_sc as plsc

F32 = jnp.float32
BF16 = jnp.bfloat16

D_MODEL = 1024
HEAD_DIM = 64
EPS = 1e-6
NEG = -1e30
FORCE_SCORE = 1e4
Q_BLOCK = 128

SB_HEADS = 8
NSA_HEADS = 8
NSA_GROUPS = 2
NSA_HPG = NSA_HEADS // NSA_GROUPS
CMP_LEN = 32
CMP_STRIDE = 16
SEL_BLOCK = 64
SEL_TOPN = 4
WINDOW = 256
HGRN_HEADS = 4
HGRN_DK = 128
HGRN_CHUNK = 64
HGRN_SUB = 16
PEER_HEADS = 8
PEER_NKEYS = 128
PEER_TOPK = 16
PEER_QDIM = 128

C_MG = 0
C_SBQ = 3072
C_SBK = 3584
C_SBV = 4096
C_NQ = 4608
C_HQ = 5120
C_HF = 5632
C_HI = 6144
C_HG = 6656
C_NKV = 7168
C_NG = 7936
IN_WIDTH = 7960
IN_PAD = 8064
LANE = 128

VMEM_LIMIT = 56 * 1024 * 1024
ROW_WORDS = D_MODEL // 2
ROW_PARTS = ROW_WORDS // LANE
PEER_TOK = 32
PEER_ROWS = PEER_HEADS * PEER_TOPK
SEL_CHUNK = 512


def _cparams(sem):
    return pltpu.CompilerParams(dimension_semantics=sem, vmem_limit_bytes=VMEM_LIMIT)


def _dot(a, b):
    return jnp.dot(a, b, preferred_element_type=F32)


def _dot_nt(a, b):
    return lax.dot_general(a, b, (((1,), (1,)), ((), ())), preferred_element_type=F32)


def _dot_tn(a, b):
    return lax.dot_general(a, b, (((0,), (0,)), ((), ())), preferred_element_type=F32)


def _split2(x):
    hi = x.astype(BF16)
    lo = (x - hi.astype(F32)).astype(BF16)
    return hi, lo


def _split3(x):
    h1 = x.astype(BF16)
    r1 = x - h1.astype(F32)
    h2 = r1.astype(BF16)
    h3 = (r1 - h2.astype(F32)).astype(BF16)
    return h1, h2, h3


def _sigmoid(x):
    return 1.0 / (1.0 + jnp.exp(-x))


def _inproj_kernel(x_ref, g_ref, w_ref, o_ref):
    x = x_ref[...]
    y = x * lax.rsqrt(jnp.mean(x * x, axis=-1, keepdims=True) + EPS) * g_ref[...]
    o_ref[...] = _dot(y.astype(BF16), w_ref[...])


def _inproj(x2, g, w):
    n = x2.shape[0]
    tm, tn = 512, IN_PAD // 3
    return pl.pallas_call(
        _inproj_kernel,
        out_shape=jax.ShapeDtypeStruct((n, IN_PAD), F32),
        grid=(IN_PAD // tn, n // tm),
        in_specs=[
            pl.BlockSpec((tm, D_MODEL), lambda c, i: (i, 0)),
            pl.BlockSpec((1, D_MODEL), lambda c, i: (0, 0)),
            pl.BlockSpec((D_MODEL, tn), lambda c, i: (0, c)),
        ],
        out_specs=pl.BlockSpec((tm, tn), lambda c, i: (i, c)),
        compiler_params=_cparams(("arbitrary", "arbitrary")),
        name="inproj",
        cost_estimate=pl.CostEstimate(
            flops=2 * n * D_MODEL * IN_PAD, transcendentals=n * (IN_PAD // tn),
            bytes_accessed=4 * n * D_MODEL * (IN_PAD // tn) + 2 * D_MODEL * IN_PAD + 4 * n * IN_PAD),
    )(x2, g, w)


SB_QROWS = 512


def _sb_kernel(q_ref, k_ref, v_ref, o_ref):
    qi = pl.program_id(2)
    nq, nk = SB_QROWS, Q_BLOCK
    per = nq // nk
    scale = HEAD_DIM ** -0.5
    r = lax.broadcasted_iota(jnp.int32, (nq, nk), 0)
    c = lax.broadcasted_iota(jnp.int32, (nq, nk), 1)
    ur = lax.broadcasted_iota(jnp.int32, (nk, nk), 0)
    uc = lax.broadcasted_iota(jnp.int32, (nk, nk), 1)
    upper = (ur > uc).astype(BF16)
    heads = [slice(h * HEAD_DIM, (h + 1) * HEAD_DIM) for h in range(LANE // HEAD_DIM)]
    qs = [q_ref[0, :, sl].astype(BF16) for sl in heads]

    def step(qb, kb, vb, c_run, acc, before):
        z = _dot_nt(qb, kb) * scale
        soft = jnp.log1p(jnp.exp(-jnp.abs(z)))
        lsp = jnp.minimum(z, 0.0) - soft
        lsn = -jnp.maximum(z, 0.0) - soft
        if before is not None:
            lsn = jnp.where(before, lsn, 0.0)
        hi, lo = _split2(lsn)
        after = c_run + (_dot(hi, upper) + _dot(lo, upper))
        a = jnp.exp(lsp + after)
        if before is not None:
            a = jnp.where(before, a, 0.0)
        acc = acc + _dot(a.astype(BF16), vb)
        c_run = c_run + jnp.sum(lsn, axis=1, keepdims=True)
        return c_run, acc

    def block(j, carry, before):
        k0 = pl.multiple_of(j * nk, nk)
        kb = k_ref[0, pl.ds(k0, nk), :].astype(BF16)
        vb = v_ref[0, pl.ds(k0, nk), :].astype(BF16)
        out = []
        for h, sl in enumerate(heads):
            out.extend(step(qs[h], kb[:, sl], vb[:, sl], carry[2 * h], carry[2 * h + 1], before))
        return tuple(out)

    carry = tuple(jnp.zeros((nq, w), F32) for _ in heads for w in (1, HEAD_DIM))
    for d in range(per - 1, -1, -1):
        carry = block(qi * per + d, carry, c + d * nk < r)
    carry = lax.fori_loop(0, qi * per, lambda n, cr: block(qi * per - 1 - n, cr, None), carry)
    for h, sl in enumerate(heads):
        o_ref[0, :, sl] = carry[2 * h + 1]


def _sb_attention(proj3):
    b, t, _ = proj3.shape
    qb, kb, vb = C_SBQ // LANE, C_SBK // LANE, C_SBV // LANE
    return pl.pallas_call(
        _sb_kernel,
        out_shape=jax.ShapeDtypeStruct((b, t, SB_HEADS * HEAD_DIM), F32),
        grid=(b, SB_HEADS // 2, t // SB_QROWS),
        in_specs=[
            pl.BlockSpec((1, SB_QROWS, LANE), lambda bi, hp, i: (bi, i, qb + hp)),
            pl.BlockSpec((1, t, LANE), lambda bi, hp, i: (bi, 0, kb + hp)),
            pl.BlockSpec((1, t, LANE), lambda bi, hp, i: (bi, 0, vb + hp)),
        ],
        out_specs=pl.BlockSpec((1, SB_QROWS, LANE), lambda bi, hp, i: (bi, i, hp)),
        compiler_params=_cparams(("arbitrary", "arbitrary", "arbitrary")),
        name="sb_attn",
        cost_estimate=pl.CostEstimate(
            flops=b * SB_HEADS * t * t * (2 * HEAD_DIM + 2 * Q_BLOCK),
            transcendentals=b * SB_HEADS * t * t * 3 // 2,
            bytes_accessed=4 * 4 * b * t * SB_HEADS * HEAD_DIM),
    )(proj3, proj3, proj3)


HGRN_PAIR = 2


def _hgrn_kernel(q_ref, f_ref, i_ref, g_ref, lb_ref, nw_ref, o_ref, st_ref, *, n_chunks):
    ch, sub = HGRN_CHUNK, HGRN_SUB
    st_ref[...] = jnp.zeros_like(st_ref)
    r = lax.broadcasted_iota(jnp.int32, (ch, ch), 0)
    c = lax.broadcasted_iota(jnp.int32, (ch, ch), 1)
    lower = (r >= c).astype(BF16)
    srow = lax.broadcasted_iota(jnp.int32, (sub, HGRN_DK), 0)

    def head_chunk(t0, hh):
        hs = slice(hh * HGRN_DK, (hh + 1) * HGRN_DK)
        lb = lb_ref[:, hs]
        nw = nw_ref[:, hs]
        fz = f_ref[0, pl.ds(t0, ch), hs]
        qz = q_ref[0, pl.ds(t0, ch), hs]
        iv = i_ref[0, pl.ds(t0, ch), hs]
        gz = g_ref[0, pl.ds(t0, ch), hs]
        f = lb + (1.0 - lb) * _sigmoid(fz)
        lf = jnp.log(f)
        kk = 1.0 - f
        qh = qz * _sigmoid(qz)
        hi, lo = _split2(lf)
        bcum = _dot(lower, hi) + _dot(lower, lo)
        st = st_ref[hh]
        o = _dot_nt((qh * jnp.exp(bcum)).astype(BF16), st.astype(BF16))
        ivb = iv.astype(BF16)

        rows = []
        for s in range(ch // sub):
            lo_r, hi_r = s * sub, (s + 1) * sub
            qs, ks, bs, vs = qh[lo_r:hi_r], kk[lo_r:hi_r], bcum[lo_r:hi_r], iv[lo_r:hi_r]
            o_s = o[lo_r:hi_r]
            if s > 0:
                bref = bcum[lo_r - 1:lo_r]
                qd = (qs * jnp.exp(bs - bref)).astype(BF16)
                kd = (kk[:lo_r] * jnp.exp(bref - bcum[:lo_r])).astype(BF16)
                att = _dot_nt(qd, kd)
                o_s = o_s + _dot(att.astype(BF16), ivb[:lo_r])
            diag_rows = []
            for t in range(sub):
                dlt = jnp.where(srow <= t, bs[t:t + 1] - bs, NEG)
                w = (qs[t:t + 1] * ks) * jnp.exp(dlt)
                att_col = jnp.sum(w, axis=1, keepdims=True)
                diag_rows.append(jnp.sum(att_col * vs, axis=0, keepdims=True))
            rows.append(o_s + jnp.concatenate(diag_rows, axis=0))
        o = jnp.concatenate(rows, axis=0)

        b_last = bcum[ch - 1:ch]
        kd = (kk * jnp.exp(b_last - bcum)).astype(BF16)
        st_ref[hh] = st * jnp.exp(b_last) + _dot_tn(ivb, kd)

        y = o * lax.rsqrt(jnp.mean(o * o, axis=-1, keepdims=True) + EPS) * nw
        o_ref[0, pl.ds(t0, ch), hs] = y * (gz * _sigmoid(gz))

    def chunk(ci, carry):
        t0 = pl.multiple_of(ci * ch, ch)
        for hh in range(HGRN_PAIR):
            head_chunk(t0, hh)
        return carry

    lax.fori_loop(0, n_chunks, chunk, 0)


def _hgrn(proj3, lower_l, norm_l):
    b, t, _ = proj3.shape
    wide = HGRN_PAIR * HGRN_DK
    cq, cf, ci, cg = C_HQ // wide, C_HF // wide, C_HI // wide, C_HG // wide
    seq = lambda col: pl.BlockSpec((1, t, wide), lambda bi, h: (bi, 0, col + h))
    vec = pl.BlockSpec((1, wide), lambda bi, h: (0, h))
    return pl.pallas_call(
        functools.partial(_hgrn_kernel, n_chunks=t // HGRN_CHUNK),
        out_shape=jax.ShapeDtypeStruct((b, t, HGRN_HEADS * HGRN_DK), F32),
        grid=(b, HGRN_HEADS // HGRN_PAIR),
        in_specs=[seq(cq), seq(cf), seq(ci), seq(cg), vec, vec],
        out_specs=pl.BlockSpec((1, t, wide), lambda bi, h: (bi, 0, h)),
        scratch_shapes=[pltpu.VMEM((HGRN_PAIR, HGRN_DK, HGRN_DK), F32)],
        compiler_params=_cparams(("arbitrary", "arbitrary")),
        name="hgrn2",
        cost_estimate=pl.CostEstimate(
            flops=b * HGRN_HEADS * t * HGRN_DK * (6 * HGRN_DK + 6 * HGRN_CHUNK + 4 * HGRN_SUB),
            transcendentals=b * HGRN_HEADS * t * HGRN_DK * (8 + HGRN_SUB),
            bytes_accessed=4 * 5 * b * t * HGRN_HEADS * HGRN_DK),
    )(proj3, proj3, proj3, proj3, lower_l, norm_l)


def _cmp_kernel(x_ref, w_ref, pe_ref, o_ref):
    n_blk = o_ref.shape[3]
    w = w_ref[0].astype(BF16)
    pieces = [x_ref[0, pl.ds(r, n_blk, stride=CMP_STRIDE), :] for r in range(CMP_STRIDE)]
    for g in range(NSA_GROUPS):
        gs = slice(g * HEAD_DIM, (g + 1) * HEAD_DIM)
        first = jnp.zeros((n_blk, HEAD_DIM), F32)
        second = jnp.zeros((n_blk, HEAD_DIM), F32)
        for r in range(CMP_STRIDE):
            rows = pieces[r][:, gs]
            lo, hi = r, CMP_STRIDE + r
            first = first + _dot((rows + pe_ref[lo:lo + 1, :]).astype(BF16),
                                 w[lo * HEAD_DIM:(lo + 1) * HEAD_DIM])
            second = second + _dot((rows + pe_ref[hi:hi + 1, :]).astype(BF16),
                                   w[hi * HEAD_DIM:(hi + 1) * HEAD_DIM])
        o_ref[0, 0, g] = first + pltpu.roll(second, n_blk - 1, 0)


def _nsa_compress(proj3, w_cmp, pe):
    b, t, _ = proj3.shape
    n_blk = t // CMP_STRIDE
    kvb = C_NKV // LANE
    return pl.pallas_call(
        _cmp_kernel,
        out_shape=jax.ShapeDtypeStruct((b, 2, NSA_GROUPS, n_blk, HEAD_DIM), F32),
        grid=(b, 2),
        in_specs=[
            pl.BlockSpec((1, t, LANE), lambda bi, kv: (bi, 0, kvb + kv)),
            pl.BlockSpec((1, CMP_LEN * HEAD_DIM, HEAD_DIM), lambda bi, kv: (kv, 0, 0)),
            pl.BlockSpec((CMP_LEN, HEAD_DIM), lambda bi, kv: (0, 0)),
        ],
        out_specs=pl.BlockSpec((1, 1, NSA_GROUPS, n_blk, HEAD_DIM), lambda bi, kv: (bi, kv, 0, 0, 0)),
        compiler_params=_cparams(("arbitrary", "arbitrary")),
        name="nsa_compress",
    )(proj3, w_cmp, pe)


def _nsa_kernel(q_ref, cmp_ref, ks_ref, vs_ref, kw_ref, vw_ref, g_ref, o_ref, *, seq_len):
    i = pl.program_id(1)
    qn = Q_BLOCK
    scale = HEAD_DIM ** -0.5
    n_blk = seq_len // SEL_BLOCK
    t0 = i * qn
    trow = t0 + lax.broadcasted_iota(jnp.int32, (qn, 1), 0)
    lane = lax.broadcasted_iota(jnp.int32, (qn, LANE), 1)
    lane_f = lane.astype(F32)

    dist_c = trow - (lane * CMP_STRIDE + (CMP_LEN - 1))
    valid_c = dist_c >= 0
    dist_cf = dist_c.astype(F32)
    cr = lax.broadcasted_iota(jnp.int32, (LANE, LANE), 0) * CMP_STRIDE
    nb = lax.broadcasted_iota(jnp.int32, (LANE, LANE), 1)
    overlap = ((cr < nb * SEL_BLOCK + SEL_BLOCK) & (cr + CMP_LEN > nb * SEL_BLOCK)
               & (nb < n_blk)).astype(BF16)
    forced = (lane == trow // SEL_BLOCK) | (lane == 0)
    causal_b = lane * SEL_BLOCK <= trow

    gsig = _sigmoid(g_ref[0])
    trow4 = jnp.concatenate([trow] * NSA_HPG, axis=0)

    span = WINDOW + qn
    kstart = pl.multiple_of(jnp.maximum(i - WINDOW // qn, 0) * qn, qn)
    wpos = kstart + lax.broadcasted_iota(jnp.int32, (NSA_HPG * qn, span), 1)
    dist_w = trow4 - wpos
    valid_w = (dist_w >= 0) & (dist_w < WINDOW)
    dist_wf = dist_w.astype(F32)

    for g in range(NSA_GROUPS):
        gs = slice(g * HEAD_DIM, (g + 1) * HEAD_DIM)
        kc = cmp_ref[0, 0, g].astype(BF16)
        vc = cmp_ref[0, 1, g].astype(BF16)
        q_heads = [q_ref[0, :, (g * NSA_HPG + p) * HEAD_DIM:(g * NSA_HPG + p + 1) * HEAD_DIM].astype(BF16)
                   for p in range(NSA_HPG)]
        slopes = [2.0 ** (-(g * NSA_HPG + p + 1)) for p in range(NSA_HPG)]
        slope_col = jnp.concatenate(
            [jnp.full((qn, 1), s, F32) for s in slopes], axis=0)

        psum = jnp.zeros((qn, LANE), F32)
        o_cmp = []
        for p in range(NSA_HPG):
            s = _dot_nt(q_heads[p], kc) * scale - slopes[p] * dist_cf
            s = jnp.where(valid_c, s, NEG)
            m = jnp.max(s, axis=1, keepdims=True)
            e = jnp.where(valid_c, jnp.exp(s - m), 0.0)
            den = jnp.sum(e, axis=1, keepdims=True)
            pc = e / jnp.where(den > 0.0, den, 1.0)
            psum = psum + pc
            o_cmp.append(_dot(pc.astype(BF16), vc))
        hi, lo = _split2(psum)
        imp = _dot(hi, overlap) + _dot(lo, overlap)
        imp = jnp.where(forced, FORCE_SCORE, jnp.where(causal_b, imp, NEG))
        imp = jnp.where(lane < n_blk, imp, -jnp.inf)
        sel = jnp.zeros((qn, LANE), jnp.bool_)
        for _ in range(SEL_TOPN):
            mx = jnp.max(imp, axis=1, keepdims=True)
            idx = jnp.min(jnp.where(imp == mx, lane_f, float(LANE)), axis=1, keepdims=True)
            onehot = lane_f == idx
            sel = sel | onehot
            imp = jnp.where(onehot, -jnp.inf, imp)
        sel_b = jnp.where(sel, 1.0, 0.0).astype(BF16)

        q4 = jnp.concatenate(q_heads, axis=0)

        def sel_chunk(ci, carry):
            m_run, l_run, acc = carry
            k0 = pl.multiple_of(ci * SEL_CHUNK, SEL_CHUNK)
            kk = ks_ref[0, pl.ds(k0, SEL_CHUNK), gs].astype(BF16)
            vv = vs_ref[0, pl.ds(k0, SEL_CHUNK), gs].astype(BF16)
            er = lax.broadcasted_iota(jnp.int32, (LANE, SEL_CHUNK), 0)
            ec = lax.broadcasted_iota(jnp.int32, (LANE, SEL_CHUNK), 1)
            expand = (er == ci * (SEL_CHUNK // SEL_BLOCK) + ec // SEL_BLOCK).astype(BF16)
            kpos = k0 + lax.broadcasted_iota(jnp.int32, (qn, SEL_CHUNK), 1)
            dist = trow - kpos
            mask = (_dot(sel_b, expand) > 0.5) & (dist >= 0)
            mask4 = jnp.concatenate([mask] * NSA_HPG, axis=0)
            dist4 = jnp.concatenate([dist.astype(F32)] * NSA_HPG, axis=0)
            s = _dot_nt(q4, kk) * scale - slope_col * dist4
            s = jnp.where(mask4, s, NEG)
            m_new = jnp.maximum(m_run, jnp.max(s, axis=1, keepdims=True))
            alpha = jnp.exp(m_run - m_new)
            pm = jnp.exp(s - m_new)
            l_new = alpha * l_run + jnp.sum(pm, axis=1, keepdims=True)
            acc = alpha * acc + _dot(pm.astype(BF16), vv)
            return m_new, l_new, acc

        n_sel_chunks = (t0 + qn + SEL_CHUNK - 1) // SEL_CHUNK
        init = (jnp.full((NSA_HPG * qn, 1), NEG, F32), jnp.zeros((NSA_HPG * qn, 1), F32),
                jnp.zeros((NSA_HPG * qn, HEAD_DIM), F32))
        _, l_sel, acc_sel = lax.fori_loop(0, n_sel_chunks, sel_chunk, init)
        o_sel = acc_sel / l_sel

        kw = kw_ref[0, pl.ds(kstart, span), gs].astype(BF16)
        vw = vw_ref[0, pl.ds(kstart, span), gs].astype(BF16)
        s = _dot_nt(q4, kw) * scale - slope_col * dist_wf
        s = jnp.where(valid_w, s, NEG)
        m = jnp.max(s, axis=1, keepdims=True)
        e = jnp.exp(s - m)
        pw = e / jnp.sum(e, axis=1, keepdims=True)
        o_win = _dot(pw.astype(BF16), vw)

        for p in range(NSA_HPG):
            hh = g * NSA_HPG + p
            rows = slice(p * qn, (p + 1) * qn)
            o = (gsig[:, 3 * hh:3 * hh + 1] * o_cmp[p]
                 + gsig[:, 3 * hh + 1:3 * hh + 2] * o_sel[rows]
                 + gsig[:, 3 * hh + 2:3 * hh + 3] * o_win[rows])
            o_ref[0, :, hh * HEAD_DIM:(hh + 1) * HEAD_DIM] = o


def _nsa_attention(proj3, cmp_kv):
    b, t, _ = proj3.shape
    kvb = C_NKV // LANE
    seq = lambda col: pl.BlockSpec((1, t, LANE), lambda bi, i: (bi, 0, col))
    n_piece = cmp_kv.shape[3]
    return pl.pallas_call(
        functools.partial(_nsa_kernel, seq_len=t),
        out_shape=jax.ShapeDtypeStruct((b, t, NSA_HEADS * HEAD_DIM), F32),
        grid=(b, t // Q_BLOCK),
        in_specs=[
            pl.BlockSpec((1, Q_BLOCK, NSA_HEADS * HEAD_DIM), lambda bi, i: (bi, i, C_NQ // 512)),
            pl.BlockSpec((1, 2, NSA_GROUPS, n_piece, HEAD_DIM), lambda bi, i: (bi, 0, 0, 0, 0)),
            seq(kvb + 2), seq(kvb + 3), seq(kvb + 4), seq(kvb + 5),
            pl.BlockSpec((1, Q_BLOCK, LANE), lambda bi, i: (bi, i, C_NG // LANE)),
        ],
        out_specs=pl.BlockSpec((1, Q_BLOCK, NSA_HEADS * HEAD_DIM), lambda bi, i: (bi, i, 0)),
        compiler_params=_cparams(("arbitrary", "arbitrary")),
        name="nsa_attn",
        cost_estimate=pl.CostEstimate(
            flops=b * NSA_HEADS * t * 4 * HEAD_DIM * (t // 2 + WINDOW + Q_BLOCK + n_piece),
            transcendentals=b * NSA_HEADS * t * (t // 2 + WINDOW + Q_BLOCK + n_piece),
            bytes_accessed=4 * b * t * (2 * NSA_HEADS * HEAD_DIM + 5 * LANE)),
    )(proj3, cmp_kv, proj3, proj3, proj3, proj3, proj3)


def _merge_kernel(x_ref, osb_ref, onsa_ref, ohg_ref, gsb_ref, gnsa_ref, ghg_ref,
                  wsb_ref, wnsa_ref, whg_ref, wo_ref, o_ref):
    m = (_sigmoid(gsb_ref[...]) * _dot(osb_ref[...].astype(BF16), wsb_ref[...])
         + _sigmoid(gnsa_ref[...]) * _dot(onsa_ref[...].astype(BF16), wnsa_ref[...])
         + _sigmoid(ghg_ref[...]) * _dot(ohg_ref[...].astype(BF16), whg_ref[...]))
    o_ref[...] = x_ref[...] + _dot(m.astype(BF16), wo_ref[...])


def _merge(x2, o_sb, o_nsa, o_hg, proj2, w_sb, w_nsa, w_hg, w_o):
    n = x2.shape[0]
    tm = 512
    row = lambda w: pl.BlockSpec((tm, w), lambda i: (i, 0))
    gate = lambda j: pl.BlockSpec((tm, D_MODEL), lambda i: (i, j))
    full = lambda a: pl.BlockSpec(a.shape, lambda i: (0, 0))
    return pl.pallas_call(
        _merge_kernel,
        out_shape=jax.ShapeDtypeStruct((n, D_MODEL), F32),
        grid=(n // tm,),
        in_specs=[row(D_MODEL), row(512), row(512), row(512), gate(0), gate(1), gate(2),
                  full(w_sb), full(w_nsa), full(w_hg), full(w_o)],
        out_specs=row(D_MODEL),
        compiler_params=_cparams(("arbitrary",)),
        name="merge_out",
        cost_estimate=pl.CostEstimate(
            flops=2 * n * D_MODEL * (3 * 512 + D_MODEL), transcendentals=3 * n * D_MODEL,
            bytes_accessed=4 * n * (5 * D_MODEL + 3 * 512) + 2 * D_MODEL * (3 * 512 + D_MODEL)),
    )(x2, o_sb, o_nsa, o_hg, proj2, proj2, proj2, w_sb, w_nsa, w_hg, w_o)


def _topk_rows(s, k, payload=None, order=None):
    rows, cols = s.shape
    if order is None:
        order = lax.broadcasted_iota(jnp.int32, (rows, cols), 0).astype(F32)
    out_row = lax.broadcasted_iota(jnp.int32, (k, cols), 0)
    vals = jnp.zeros((k, cols), F32)
    tags = jnp.zeros((k, cols), F32)
    for j in range(k):
        mx = jnp.max(s, axis=0, keepdims=True)
        idx = jnp.min(jnp.where(s == mx, order, jnp.inf), axis=0, keepdims=True)
        onehot = order == idx
        s = jnp.where(onehot, -jnp.inf, s)
        tag = idx if payload is None else jnp.sum(jnp.where(onehot, payload, 0.0), axis=0, keepdims=True)
        vals = jnp.where(out_row == j, mx, vals)
        tags = jnp.where(out_row == j, tag, tags)
    return vals, tags


def _candidate_pairs(k):
    return [(i, j) for i in range(k) for j in range(k) if (i + 1) * (j + 1) <= k]


def _route_kernel(x_ref, g_ref, wqt_ref, keys_ref, h_ref, idx_ref, gate_ref):
    k = PEER_TOPK
    half = PEER_QDIM // 2
    x = x_ref[...]
    h = (x * lax.rsqrt(jnp.mean(x * x, axis=-1, keepdims=True) + EPS) * g_ref[...]).astype(BF16)
    h_ref[...] = h
    qt = _dot_nt(wqt_ref[...], h).astype(BF16)
    gates, ids = [], []
    pairs = _candidate_pairs(k)
    n_pad = -len(pairs) % 8
    tokens = x.shape[0]
    pad_val = jnp.full((n_pad, tokens), -jnp.inf, F32)
    pad_idx = jnp.zeros((n_pad, tokens), F32)
    flat = jnp.concatenate(
        [jnp.full((1, tokens), float(i * k + j), F32) for i, j in pairs]
        + [jnp.full((n_pad, tokens), float(k * k), F32)], axis=0)
    for hd in range(PEER_HEADS):
        tops = []
        for a in range(2):
            r0 = (hd * 2 + a) * half
            s = _dot(keys_ref[a].astype(BF16), qt[r0:r0 + half])
            tops.append(_topk_rows(s, k))
        (s0, i0), (s1, i1) = tops
        cand = jnp.concatenate([s0[i:i + 1] + s1[j:j + 1] for i, j in pairs] + [pad_val], axis=0)
        cidx = jnp.concatenate([i0[i:i + 1] * float(PEER_NKEYS) + i1[j:j + 1] for i, j in pairs]
                               + [pad_idx], axis=0)
        best, eidx = _topk_rows(cand, k, payload=cidx, order=flat)
        e = jnp.exp(best - jnp.max(best, axis=0, keepdims=True))
        gates.append(e / jnp.sum(e, axis=0, keepdims=True))
        ids.append(eidx)
    gate_ref[...] = jnp.concatenate(gates, axis=0).T
    first_word = jnp.concatenate(ids, axis=0).T * float(ROW_PARTS)
    nt = PEER_TOK
    for t in range(x.shape[0] // nt):
        for j in range(ROW_PARTS):
            r0 = (t * ROW_PARTS + j) * nt
            idx_ref[r0:r0 + nt, :] = (first_word[t * nt:(t + 1) * nt] + float(j)).astype(jnp.int32)


def _peer_route(x2, g, w_q_t, sub_keys):
    n = x2.shape[0]
    tm = 128
    return pl.pallas_call(
        _route_kernel,
        out_shape=(jax.ShapeDtypeStruct((n, D_MODEL), BF16),
                   jax.ShapeDtypeStruct((n * ROW_PARTS, PEER_HEADS * PEER_TOPK), jnp.int32),
                   jax.ShapeDtypeStruct((n, PEER_HEADS * PEER_TOPK), F32)),
        grid=(n // tm,),
        in_specs=[pl.BlockSpec((tm, D_MODEL), lambda i: (i, 0)),
                  pl.BlockSpec((1, D_MODEL), lambda i: (0, 0)),
                  pl.BlockSpec(w_q_t.shape, lambda i: (0, 0)),
                  pl.BlockSpec(sub_keys.shape, lambda i: (0, 0, 0))],
        out_specs=(pl.BlockSpec((tm, D_MODEL), lambda i: (i, 0)),
                   pl.BlockSpec((tm * ROW_PARTS, PEER_HEADS * PEER_TOPK), lambda i: (i, 0)),
                   pl.BlockSpec((tm, PEER_HEADS * PEER_TOPK), lambda i: (i, 0))),
        compiler_params=_cparams(("arbitrary",)),
        name="peer_route",
        cost_estimate=pl.CostEstimate(
            flops=n * (2 * D_MODEL * D_MODEL + 4 * PEER_HEADS * PEER_QDIM * PEER_NKEYS
                       + 6 * PEER_HEADS * PEER_TOPK * (2 * PEER_NKEYS + PEER_TOPK * PEER_TOPK)),
            transcendentals=n * PEER_HEADS * PEER_TOPK,
            bytes_accessed=n * (6 * D_MODEL + 8 * PEER_HEADS * PEER_TOPK) + 2 * D_MODEL * D_MODEL),
    )(x2, g, w_q_t, sub_keys)


GATHER_WIN = 128


def _pack_table(tab):
    bits = lax.bitcast_convert_type(tab.astype(BF16), jnp.uint16).astype(jnp.uint32)
    words = (bits[:, ROW_WORDS:] << 16) | bits[:, :ROW_WORDS]
    return lax.bitcast_convert_type(words, jnp.int32).reshape(-1, LANE)


def _sc_gather(table, idx):
    m = idx.shape[0] * idx.shape[1]
    mesh = plsc.VectorSubcoreMesh(core_axis_name="core", subcore_axis_name="subcore")

    @pl.kernel(out_type=jax.ShapeDtypeStruct((m, LANE), table.dtype), mesh=mesh,
               cost_estimate=pl.CostEstimate(flops=0, transcendentals=0,
                                             bytes_accessed=m * (2 * LANE + 1) * 4))
    def gather(tab_hbm, idx_hbm, out_hbm):
        def body(idx_vmem, out_vmem):
            pltpu.sync_copy(tab_hbm.at[idx_vmem.at[0]], out_vmem)

        pltpu.emit_pipeline(
            body,
            grid=(m // GATHER_WIN,),
            in_specs=[pl.BlockSpec((1, GATHER_WIN), lambda i: (i, 0))],
            out_specs=[pl.BlockSpec((GATHER_WIN, LANE), lambda i: (i, 0))],
            core_axis_name=("core", "subcore"),
            dimension_semantics=(pltpu.PARALLEL,),
            trace_scopes=False,
        )(idx_hbm, out_hbm)

    return gather(table, idx)


HIGH_MASK = -65536


def _unpack(words):
    lo = lax.bitcast_convert_type(lax.shift_left(words, 16), F32)
    hi = lax.bitcast_convert_type(words & HIGH_MASK, F32)
    return lo, hi


def _expert_kernel(x_ref, h_ref, gate_ref, ug_ref, vg_ref, o_ref):
    nt, nr = PEER_TOK, PEER_ROWS
    h = h_ref[...].astype(F32)
    gate_t = jnp.concatenate([gate_ref[...], jnp.zeros((LANE - nt, nr), F32)], axis=0).T
    lane = lax.broadcasted_iota(jnp.int32, (nr, LANE), 1)
    hpre = jnp.zeros((nr, LANE), F32)
    for n in range(nt):
        s = jnp.zeros((nr, LANE), F32)
        for j in range(ROW_PARTS):
            lo, hi = _unpack(ug_ref[pl.ds((j * nt + n) * nr, nr), :])
            s = s + lo * h[n:n + 1, j * LANE:(j + 1) * LANE]
            s = s + hi * h[n:n + 1, ROW_WORDS + j * LANE:ROW_WORDS + (j + 1) * LANE]
        hpre = hpre + jnp.where(lane == n, jnp.sum(s, axis=1, keepdims=True), 0.0)
    act = gate_t * (0.5 * hpre * (1.0 + lax.erf(hpre * (2.0 ** -0.5))))
    rows = []
    for n in range(nt):
        a = act[:, n:n + 1]
        los, his = [], []
        for j in range(ROW_PARTS):
            lo, hi = _unpack(vg_ref[pl.ds((j * nt + n) * nr, nr), :])
            los.append(jnp.sum(a * lo, axis=0, keepdims=True))
            his.append(jnp.sum(a * hi, axis=0, keepdims=True))
        rows.append(jnp.concatenate(los + his, axis=1))
    o_ref[...] = x_ref[...] + jnp.concatenate(rows, axis=0)


def _peer_experts(x2, h2, gate, ug, vg):
    n = x2.shape[0]
    nt, nr = PEER_TOK, PEER_ROWS
    blk = nt * ROW_PARTS * nr
    return pl.pallas_call(
        _expert_kernel,
        out_shape=jax.ShapeDtypeStruct((n, D_MODEL), F32),
        grid=(n // nt,),
        in_specs=[pl.BlockSpec((nt, D_MODEL), lambda i: (i, 0)),
                  pl.BlockSpec((nt, D_MODEL), lambda i: (i, 0)),
                  pl.BlockSpec((nt, nr), lambda i: (i, 0)),
                  pl.BlockSpec((blk, LANE), lambda i: (i, 0)),
                  pl.BlockSpec((blk, LANE), lambda i: (i, 0))],
        out_specs=pl.BlockSpec((nt, D_MODEL), lambda i: (i, 0)),
        compiler_params=_cparams(("arbitrary",)),
        name="peer_experts",
        cost_estimate=pl.CostEstimate(
            flops=n * nr * D_MODEL * 6, transcendentals=n * nr,
            bytes_accessed=n * (2 * nr * ROW_WORDS * 4 + 10 * D_MODEL + 4 * nr)),
    )(x2, h2, gate, ug, vg)


def _norm_kernel(x_ref, g_ref, o_ref):
    x = x_ref[...]
    o_ref[...] = x * lax.rsqrt(jnp.mean(x * x, axis=-1, keepdims=True) + EPS) * g_ref[...]


def _final_norm(x2, g):
    n = x2.shape[0]
    tm = 1024
    return pl.pallas_call(
        _norm_kernel,
        out_shape=jax.ShapeDtypeStruct((n, D_MODEL), F32),
        grid=(n // tm,),
        in_specs=[pl.BlockSpec((tm, D_MODEL), lambda i: (i, 0)),
                  pl.BlockSpec((1, D_MODEL), lambda i: (0, 0))],
        out_specs=pl.BlockSpec((tm, D_MODEL), lambda i: (i, 0)),
        compiler_params=_cparams(("arbitrary",)),
        name="final_norm",
    )(x2, g)


def _permute_w_in(w_in):
    o = [0, 512, 1024, 1536, 2048, 2816, 2840, 3352, 3864, 4376, 4888, IN_WIDTH]
    sb_q, sb_k, sb_v, nsa_q, nsa_kv, nsa_g, hg_q, hg_f, hg_i, hg_g, merge_g = [
        w_in[..., o[j]:o[j + 1]] for j in range(11)]
    pad = jnp.zeros(w_in.shape[:-1] + (IN_PAD - IN_WIDTH,), w_in.dtype)
    return jnp.concatenate(
        [merge_g, sb_q, sb_k, sb_v, nsa_q, hg_q, hg_f, hg_i, hg_g, nsa_kv, nsa_g, pad], axis=-1)


def _mixer_layer(x2, bsz, seq, norm_g, w_in_p, w_cmp, pe, hgrn_norm_l, lower_l,
                 w_sb, w_nsa, w_hg, w_o):
    n = bsz * seq
    proj2 = _inproj(x2, norm_g, w_in_p)
    proj3 = proj2.reshape(bsz, seq, IN_PAD)
    o_sb = _sb_attention(proj3)
    o_hg = _hgrn(proj3, lower_l, hgrn_norm_l)
    cmp_kv = _nsa_compress(proj3, w_cmp, pe)
    o_nsa = _nsa_attention(proj3, cmp_kv)
    return _merge(x2, o_sb.reshape(n, -1), o_nsa.reshape(n, -1), o_hg.reshape(n, -1),
                  proj2, w_sb, w_nsa, w_hg, w_o)


def _peer_layer(x2, norm_g, w_q_t, sub_keys, u_words, v_words):
    h2, idx, gate = _peer_route(x2, norm_g, w_q_t, sub_keys)
    ug = _sc_gather(u_words, idx)
    vg = _sc_gather(v_words, idx)
    return _peer_experts(x2, h2, gate, ug, vg)


BATCH_STREAMS = 4


def kernel(x, norm_mix, norm_ffn, w_in, nsa_w_cmp_k, nsa_w_cmp_v, nsa_cmp_pe, hgrn_norm, hgrn_lower_bounds, w_branch_sb, w_branch_nsa, w_branch_hgrn, w_out, peer_w_q, peer_sub_keys, peer_u, peer_v, norm_final):
    bsz, seq, d = x.shape
    depth = w_in.shape[0]
    lb_soft = jax.nn.softmax(hgrn_lower_bounds.astype(F32), axis=0)
    lower = jnp.cumsum(lb_soft, axis=0) - lb_soft[0]
    w_in_p = _permute_w_in(w_in).astype(BF16)
    streams = BATCH_STREAMS if bsz % BATCH_STREAMS == 0 else 1
    sb = bsz // streams
    xs = [x[s * sb:(s + 1) * sb].reshape(sb * seq, d) for s in range(streams)]
    for l in range(depth):
        w_cmp = jnp.stack([nsa_w_cmp_k[l], nsa_w_cmp_v[l]])
        w_sb, w_nsa, w_hg, w_o = (w_branch_sb[l].astype(BF16), w_branch_nsa[l].astype(BF16),
                                  w_branch_hgrn[l].astype(BF16), w_out[l].astype(BF16))
        w_q_t = peer_w_q[l].T.astype(BF16)
        u_words, v_words = _pack_table(peer_u[l]), _pack_table(peer_v[l])
        xs = [_mixer_layer(xh, sb, seq, norm_mix[l][None], w_in_p[l], w_cmp, nsa_cmp_pe[l],
                           hgrn_norm[l][None], lower[l][None], w_sb, w_nsa, w_hg, w_o) for xh in xs]
        xs = [_peer_layer(xh, norm_ffn[l][None], w_q_t, peer_sub_keys[l], u_words, v_words)
              for xh in xs]
    outs = [_final_norm(xh, norm_final[None]).reshape(sb, seq, d) for xh in xs]
    return jnp.concatenate(outs, axis=0)
```

```python
import functools

import jax
import jax.numpy as jnp
from jax import lax
from jax.experimental import pallas as pl
from jax.experimental.pallas import tpu as pltpu
from jax.experimental.pallas import tpu_sc as plsc

F32 = jnp.float32
BF16 = jnp.bfloat16

D_MODEL = 1024
HEAD_DIM = 64
EPS = 1e-6
NEG = -1e30
FORCE_SCORE = 1e4
Q_BLOCK = 128

SB_HEADS = 8
NSA_HEADS = 8
NSA_GROUPS = 2
NSA_HPG = NSA_HEADS // NSA_GROUPS
CMP_LEN = 32
CMP_STRIDE = 16
SEL_BLOCK = 64
SEL_TOPN = 4
WINDOW = 256
HGRN_HEADS = 4
HGRN_DK = 128
HGRN_CHUNK = 64
HGRN_SUB = 16
PEER_HEADS = 8
PEER_NKEYS = 128
PEER_TOPK = 16
PEER_QDIM = 128

C_MG = 0
C_SBQ = 3072
C_SBK = 3584
C_SBV = 4096
C_NQ = 4608
C_HQ = 5120
C_HF = 5632
C_HI = 6144
C_HG = 6656
C_NKV = 7168
C_NG = 7936
IN_WIDTH = 7960
IN_PAD = 8064
LANE = 128

VMEM_LIMIT = 56 * 1024 * 1024
ROW_WORDS = D_MODEL // 2
ROW_PARTS = ROW_WORDS // LANE
PEER_TOK = 32
PEER_ROWS = PEER_HEADS * PEER_TOPK
SEL_CHUNK = 512


def _cparams(sem):
    return pltpu.CompilerParams(dimension_semantics=sem, vmem_limit_bytes=VMEM_LIMIT)


def _dot(a, b):
    return jnp.dot(a, b, preferred_element_type=F32)


def _dot_nt(a, b):
    return lax.dot_general(a, b, (((1,), (1,)), ((), ())), preferred_element_type=F32)


def _dot_tn(a, b):
    return lax.dot_general(a, b, (((0,), (0,)), ((), ())), preferred_element_type=F32)


def _split2(x):
    hi = x.astype(BF16)
    lo = (x - hi.astype(F32)).astype(BF16)
    return hi, lo


def _split3(x):
    h1 = x.astype(BF16)
    r1 = x - h1.astype(F32)
    h2 = r1.astype(BF16)
    h3 = (r1 - h2.astype(F32)).astype(BF16)
    return h1, h2, h3


def _sigmoid(x):
    return 1.0 / (1.0 + jnp.exp(-x))


def _inproj_kernel(x_ref, g_ref, w_ref, o_ref):
    x = x_ref[...]
    y = x * lax.rsqrt(jnp.mean(x * x, axis=-1, keepdims=True) + EPS) * g_ref[...]
    o_ref[...] = _dot(y.astype(BF16), w_ref[...])


def _inproj(x2, g, w):
    n = x2.shape[0]
    tm, tn = 512, IN_PAD // 3
    return pl.pallas_call(
        _inproj_kernel,
        out_shape=jax.ShapeDtypeStruct((n, IN_PAD), F32),
        grid=(IN_PAD // tn, n // tm),
        in_specs=[
            pl.BlockSpec((tm, D_MODEL), lambda c, i: (i, 0)),
            pl.BlockSpec((1, D_MODEL), lambda c, i: (0, 0)),
            pl.BlockSpec((D_MODEL, tn), lambda c, i: (0, c)),
        ],
        out_specs=pl.BlockSpec((tm, tn), lambda c, i: (i, c)),
        compiler_params=_cparams(("arbitrary", "arbitrary")),
        name="inproj",
        cost_estimate=pl.CostEstimate(
            flops=2 * n * D_MODEL * IN_PAD, transcendentals=n * (IN_PAD // tn),
            bytes_accessed=4 * n * D_MODEL * (IN_PAD // tn) + 2 * D_MODEL * IN_PAD + 4 * n * IN_PAD),
    )(x2, g, w)


SB_QROWS = 512


def _sb_kernel(q_ref, k_ref, v_ref, o_ref):
    qi = pl.program_id(2)
    nq, nk = SB_QROWS, Q_BLOCK
    per = nq // nk
    scale = HEAD_DIM ** -0.5
    r = lax.broadcasted_iota(jnp.int32, (nq, nk), 0)
    c = lax.broadcasted_iota(jnp.int32, (nq, nk), 1)
    ur = lax.broadcasted_iota(jnp.int32, (nk, nk), 0)
    uc = lax.broadcasted_iota(jnp.int32, (nk, nk), 1)
    upper = (ur > uc).astype(BF16)
    heads = [slice(h * HEAD_DIM, (h + 1) * HEAD_DIM) for h in range(LANE // HEAD_DIM)]
    qs = [q_ref[0, :, sl].astype(BF16) for sl in heads]

    def step(qb, kb, vb, c_run, acc, before):
        z = _dot_nt(qb, kb) * scale
        soft = jnp.log1p(jnp.exp(-jnp.abs(z)))
        lsp = jnp.minimum(z, 0.0) - soft
        lsn = -jnp.maximum(z, 0.0) - soft
        if before is not None:
            lsn = jnp.where(before, lsn, 0.0)
        hi, lo = _split2(lsn)
        after = c_run + (_dot(hi, upper) + _dot(lo, upper))
        a = jnp.exp(lsp + after)
        if before is not None:
            a = jnp.where(before, a, 0.0)
        acc = acc + _dot(a.astype(BF16), vb)
        c_run = c_run + jnp.sum(lsn, axis=1, keepdims=True)
        return c_run, acc

    def block(j, carry, before):
        k0 = pl.multiple_of(j * nk, nk)
        kb = k_ref[0, pl.ds(k0, nk), :].astype(BF16)
        vb = v_ref[0, pl.ds(k0, nk), :].astype(BF16)
        out = []
        for h, sl in enumerate(heads):
            out.extend(step(qs[h], kb[:, sl], vb[:, sl], carry[2 * h], carry[2 * h + 1], before))
        return tuple(out)

    carry = tuple(jnp.zeros((nq, w), F32) for _ in heads for w in (1, HEAD_DIM))
    for d in range(per - 1, -1, -1):
        carry = block(qi * per + d, carry, c + d * nk < r)
    carry = lax.fori_loop(0, qi * per, lambda n, cr: block(qi * per - 1 - n, cr, None), carry)
    for h, sl in enumerate(heads):
        o_ref[0, :, sl] = carry[2 * h + 1]


def _sb_attention(proj3):
    b, t, _ = proj3.shape
    qb, kb, vb = C_SBQ // LANE, C_SBK // LANE, C_SBV // LANE
    return pl.pallas_call(
        _sb_kernel,
        out_shape=jax.ShapeDtypeStruct((b, t, SB_HEADS * HEAD_DIM), F32),
        grid=(b, SB_HEADS // 2, t // SB_QROWS),
        in_specs=[
            pl.BlockSpec((1, SB_QROWS, LANE), lambda bi, hp, i: (bi, i, qb + hp)),
            pl.BlockSpec((1, t, LANE), lambda bi, hp, i: (bi, 0, kb + hp)),
            pl.BlockSpec((1, t, LANE), lambda bi, hp, i: (bi, 0, vb + hp)),
        ],
        out_specs=pl.BlockSpec((1, SB_QROWS, LANE), lambda bi, hp, i: (bi, i, hp)),
        compiler_params=_cparams(("arbitrary", "arbitrary", "arbitrary")),
        name="sb_attn",
        cost_estimate=pl.CostEstimate(
            flops=b * SB_HEADS * t * t * (2 * HEAD_DIM + 2 * Q_BLOCK),
            transcendentals=b * SB_HEADS * t * t * 3 // 2,
            bytes_accessed=4 * 4 * b * t * SB_HEADS * HEAD_DIM),
    )(proj3, proj3, proj3)


HGRN_PAIR = 2


def _hgrn_kernel(q_ref, f_ref, i_ref, g_ref, lb_ref, nw_ref, o_ref, st_ref, *, n_chunks):
    ch, sub = HGRN_CHUNK, HGRN_SUB
    st_ref[...] = jnp.zeros_like(st_ref)
    r = lax.broadcasted_iota(jnp.int32, (ch, ch), 0)
    c = lax.broadcasted_iota(jnp.int32, (ch, ch), 1)
    lower = (r >= c).astype(BF16)
    srow = lax.broadcasted_iota(jnp.int32, (sub, HGRN_DK), 0)

    def head_chunk(t0, hh):
        hs = slice(hh * HGRN_DK, (hh + 1) * HGRN_DK)
        lb = lb_ref[:, hs]
        nw = nw_ref[:, hs]
        fz = f_ref[0, pl.ds(t0, ch), hs]
        qz = q_ref[0, pl.ds(t0, ch), hs]
        iv = i_ref[0, pl.ds(t0, ch), hs]
        gz = g_ref[0, pl.ds(t0, ch), hs]
        f = lb + (1.0 - lb) * _sigmoid(fz)
        lf = jnp.log(f)
        kk = 1.0 - f
        qh = qz * _sigmoid(qz)
        hi, lo = _split2(lf)
        bcum = _dot(lower, hi) + _dot(lower, lo)
        st = st_ref[hh]
        o = _dot_nt((qh * jnp.exp(bcum)).astype(BF16), st.astype(BF16))
        ivb = iv.astype(BF16)

        rows = []
        for s in range(ch // sub):
            lo_r, hi_r = s * sub, (s + 1) * sub
            qs, ks, bs, vs = qh[lo_r:hi_r], kk[lo_r:hi_r], bcum[lo_r:hi_r], iv[lo_r:hi_r]
            o_s = o[lo_r:hi_r]
            if s > 0:
                bref = bcum[lo_r - 1:lo_r]
                qd = (qs * jnp.exp(bs - bref)).astype(BF16)
                kd = (kk[:lo_r] * jnp.exp(bref - bcum[:lo_r])).astype(BF16)
                att = _dot_nt(qd, kd)
                o_s = o_s + _dot(att.astype(BF16), ivb[:lo_r])
            diag_rows = []
            for t in range(sub):
                dlt = jnp.where(srow <= t, bs[t:t + 1] - bs, NEG)
                w = (qs[t:t + 1] * ks) * jnp.exp(dlt)
                att_col = jnp.sum(w, axis=1, keepdims=True)
                diag_rows.append(jnp.sum(att_col * vs, axis=0, keepdims=True))
            rows.append(o_s + jnp.concatenate(diag_rows, axis=0))
        o = jnp.concatenate(rows, axis=0)

        b_last = bcum[ch - 1:ch]
        kd = (kk * jnp.exp(b_last - bcum)).astype(BF16)
        st_ref[hh] = st * jnp.exp(b_last) + _dot_tn(ivb, kd)

        y = o * lax.rsqrt(jnp.mean(o * o, axis=-1, keepdims=True) + EPS) * nw
        o_ref[0, pl.ds(t0, ch), hs] = y * (gz * _sigmoid(gz))

    def chunk(ci, carry):
        t0 = pl.multiple_of(ci * ch, ch)
        for hh in range(HGRN_PAIR):
            head_chunk(t0, hh)
        return carry

    lax.fori_loop(0, n_chunks, chunk, 0)


def _hgrn(proj3, lower_l, norm_l):
    b, t, _ = proj3.shape
    wide = HGRN_PAIR * HGRN_DK
    cq, cf, ci, cg = C_HQ // wide, C_HF // wide, C_HI // wide, C_HG // wide
    seq = lambda col: pl.BlockSpec((1, t, wide), lambda bi, h: (bi, 0, col + h))
    vec = pl.BlockSpec((1, wide), lambda bi, h: (0, h))
    return pl.pallas_call(
        functools.partial(_hgrn_kernel, n_chunks=t // HGRN_CHUNK),
        out_shape=jax.ShapeDtypeStruct((b, t, HGRN_HEADS * HGRN_DK), F32),
        grid=(b, HGRN_HEADS // HGRN_PAIR),
        in_specs=[seq(cq), seq(cf), seq(ci), seq(cg), vec, vec],
        out_specs=pl.BlockSpec((1, t, wide), lambda bi, h: (bi, 0, h)),
        scratch_shapes=[pltpu.VMEM((HGRN_PAIR, HGRN_DK, HGRN_DK), F32)],
        compiler_params=_cparams(("arbitrary", "arbitrary")),
        name="hgrn2",
        cost_estimate=pl.CostEstimate(
            flops=b * HGRN_HEADS * t * HGRN_DK * (6 * HGRN_DK + 6 * HGRN_CHUNK + 4 * HGRN_SUB),
            transcendentals=b * HGRN_HEADS * t * HGRN_DK * (8 + HGRN_SUB),
            bytes_accessed=4 * 5 * b * t * HGRN_HEADS * HGRN_DK),
    )(proj3, proj3, proj3, proj3, lower_l, norm_l)


def _cmp_kernel(x_ref, w_ref, pe_ref, o_ref):
    n_blk = o_ref.shape[3]
    w = w_ref[0].astype(BF16)
    pieces = [x_ref[0, pl.ds(r, n_blk, stride=CMP_STRIDE), :] for r in range(CMP_STRIDE)]
    for g in range(NSA_GROUPS):
        gs = slice(g * HEAD_DIM, (g + 1) * HEAD_DIM)
        first = jnp.zeros((n_blk, HEAD_DIM), F32)
        second = jnp.zeros((n_blk, HEAD_DIM), F32)
        for r in range(CMP_STRIDE):
            rows = pieces[r][:, gs]
            lo, hi = r, CMP_STRIDE + r
            first = first + _dot((rows + pe_ref[lo:lo + 1, :]).astype(BF16),
                                 w[lo * HEAD_DIM:(lo + 1) * HEAD_DIM])
            second = second + _dot((rows + pe_ref[hi:hi + 1, :]).astype(BF16),
                                   w[hi * HEAD_DIM:(hi + 1) * HEAD_DIM])
        o_ref[0, 0, g] = first + pltpu.roll(second, n_blk - 1, 0)


def _nsa_compress(proj3, w_cmp, pe):
    b, t, _ = proj3.shape
    n_blk = t // CMP_STRIDE
    kvb = C_NKV // LANE
    return pl.pallas_call(
        _cmp_kernel,
        out_shape=jax.ShapeDtypeStruct((b, 2, NSA_GROUPS, n_blk, HEAD_DIM), F32),
        grid=(b, 2),
        in_specs=[
            pl.BlockSpec((1, t, LANE), lambda bi, kv: (bi, 0, kvb + kv)),
            pl.BlockSpec((1, CMP_LEN * HEAD_DIM, HEAD_DIM), lambda bi, kv: (kv, 0, 0)),
            pl.BlockSpec((CMP_LEN, HEAD_DIM), lambda bi, kv: (0, 0)),
        ],
        out_specs=pl.BlockSpec((1, 1, NSA_GROUPS, n_blk, HEAD_DIM), lambda bi, kv: (bi, kv, 0, 0, 0)),
        compiler_params=_cparams(("arbitrary", "arbitrary")),
        name="nsa_compress",
    )(proj3, w_cmp, pe)


def _nsa_kernel(q_ref, cmp_ref, ks_ref, vs_ref, kw_ref, vw_ref, g_ref, o_ref, *, seq_len):
    i = pl.program_id(1)
    qn = Q_BLOCK
    scale = HEAD_DIM ** -0.5
    n_blk = seq_len // SEL_BLOCK
    t0 = i * qn
    trow = t0 + lax.broadcasted_iota(jnp.int32, (qn, 1), 0)
    lane = lax.broadcasted_iota(jnp.int32, (qn, LANE), 1)
    lane_f = lane.astype(F32)

    dist_c = trow - (lane * CMP_STRIDE + (CMP_LEN - 1))
    valid_c = dist_c >= 0
    dist_cf = dist_c.astype(F32)
    cr = lax.broadcasted_iota(jnp.int32, (LANE, LANE), 0) * CMP_STRIDE
    nb = lax.broadcasted_iota(jnp.int32, (LANE, LANE), 1)
    overlap = ((cr < nb * SEL_BLOCK + SEL_BLOCK) & (cr + CMP_LEN > nb * SEL_BLOCK)
               & (nb < n_blk)).astype(BF16)
    forced = (lane == trow // SEL_BLOCK) | (lane == 0)
    causal_b = lane * SEL_BLOCK <= trow

    gsig = _sigmoid(g_ref[0])
    trow4 = jnp.concatenate([trow] * NSA_HPG, axis=0)

    span = WINDOW + qn
    kstart = pl.multiple_of(jnp.maximum(i - WINDOW // qn, 0) * qn, qn)
    wpos = kstart + lax.broadcasted_iota(jnp.int32, (NSA_HPG * qn, span), 1)
    dist_w = trow4 - wpos
    valid_w = (dist_w >= 0) & (dist_w < WINDOW)
    dist_wf = dist_w.astype(F32)

    for g in range(NSA_GROUPS):
        gs = slice(g * HEAD_DIM, (g + 1) * HEAD_DIM)
        kc = cmp_ref[0, 0, g].astype(BF16)
        vc = cmp_ref[0, 1, g].astype(BF16)
        q_heads = [q_ref[0, :, (g * NSA_HPG + p) * HEAD_DIM:(g * NSA_HPG + p + 1) * HEAD_DIM].astype(BF16)
                   for p in range(NSA_HPG)]
        slopes = [2.0 ** (-(g * NSA_HPG + p + 1)) for p in range(NSA_HPG)]
        slope_col = jnp.concatenate(
            [jnp.full((qn, 1), s, F32) for s in slopes], axis=0)

        psum = jnp.zeros((qn, LANE), F32)
        o_cmp = []
        for p in range(NSA_HPG):
            s = _dot_nt(q_heads[p], kc) * scale - slopes[p] * dist_cf
            s = jnp.where(valid_c, s, NEG)
            m = jnp.max(s, axis=1, keepdims=True)
            e = jnp.where(valid_c, jnp.exp(s - m), 0.0)
            den = jnp.sum(e, axis=1, keepdims=True)
            pc = e / jnp.where(den > 0.0, den, 1.0)
            psum = psum + pc
            o_cmp.append(_dot(pc.astype(BF16), vc))
        hi, lo = _split2(psum)
        imp = _dot(hi, overlap) + _dot(lo, overlap)
        imp = jnp.where(forced, FORCE_SCORE, jnp.where(causal_b, imp, NEG))
        imp = jnp.where(lane < n_blk, imp, -jnp.inf)
        sel = jnp.zeros((qn, LANE), jnp.bool_)
        for _ in range(SEL_TOPN):
            mx = jnp.max(imp, axis=1, keepdims=True)
            idx = jnp.min(jnp.where(imp == mx, lane_f, float(LANE)), axis=1, keepdims=True)
            onehot = lane_f == idx
            sel = sel | onehot
            imp = jnp.where(onehot, -jnp.inf, imp)
        sel_b = jnp.where(sel, 1.0, 0.0).astype(BF16)

        q4 = jnp.concatenate(q_heads, axis=0)

        def sel_chunk(ci, carry):
            m_run, l_run, acc = carry
            k0 = pl.multiple_of(ci * SEL_CHUNK, SEL_CHUNK)
            kk = ks_ref[0, pl.ds(k0, SEL_CHUNK), gs].astype(BF16)
            vv = vs_ref[0, pl.ds(k0, SEL_CHUNK), gs].astype(BF16)
            er = lax.broadcasted_iota(jnp.int32, (LANE, SEL_CHUNK), 0)
            ec = lax.broadcasted_iota(jnp.int32, (LANE, SEL_CHUNK), 1)
            expand = (er == ci * (SEL_CHUNK // SEL_BLOCK) + ec // SEL_BLOCK).astype(BF16)
            kpos = k0 + lax.broadcasted_iota(jnp.int32, (qn, SEL_CHUNK), 1)
            dist = trow - kpos
            mask = (_dot(sel_b, expand) > 0.5) & (dist >= 0)
            mask4 = jnp.concatenate([mask] * NSA_HPG, axis=0)
            dist4 = jnp.concatenate([dist.astype(F32)] * NSA_HPG, axis=0)
            s = _dot_nt(q4, kk) * scale - slope_col * dist4
            s = jnp.where(mask4, s, NEG)
            m_new = jnp.maximum(m_run, jnp.max(s, axis=1, keepdims=True))
            alpha = jnp.exp(m_run - m_new)
            pm = jnp.exp(s - m_new)
            l_new = alpha * l_run + jnp.sum(pm, axis=1, keepdims=True)
            acc = alpha * acc + _dot(pm.astype(BF16), vv)
            return m_new, l_new, acc

        n_sel_chunks = (t0 + qn + SEL_CHUNK - 1) // SEL_CHUNK
        init = (jnp.full((NSA_HPG * qn, 1), NEG, F32), jnp.zeros((NSA_HPG * qn, 1), F32),
                jnp.zeros((NSA_HPG * qn, HEAD_DIM), F32))
        _, l_sel, acc_sel = lax.fori_loop(0, n_sel_chunks, sel_chunk, init)
        o_sel = acc_sel / l_sel

        kw = kw_ref[0, pl.ds(kstart, span), gs].astype(BF16)
        vw = vw_ref[0, pl.ds(kstart, span), gs].astype(BF16)
        s = _dot_nt(q4, kw) * scale - slope_col * dist_wf
        s = jnp.where(valid_w, s, NEG)
        m = jnp.max(s, axis=1, keepdims=True)
        e = jnp.exp(s - m)
        pw = e / jnp.sum(e, axis=1, keepdims=True)
        o_win = _dot(pw.astype(BF16), vw)

        for p in range(NSA_HPG):
            hh = g * NSA_HPG + p
            rows = slice(p * qn, (p + 1) * qn)
            o = (gsig[:, 3 * hh:3 * hh + 1] * o_cmp[p]
                 + gsig[:, 3 * hh + 1:3 * hh + 2] * o_sel[rows]
                 + gsig[:, 3 * hh + 2:3 * hh + 3] * o_win[rows])
            o_ref[0, :, hh * HEAD_DIM:(hh + 1) * HEAD_DIM] = o


def _nsa_attention(proj3, cmp_kv):
    b, t, _ = proj3.shape
    kvb = C_NKV // LANE
    seq = lambda col: pl.BlockSpec((1, t, LANE), lambda bi, i: (bi, 0, col))
    n_piece = cmp_kv.shape[3]
    return pl.pallas_call(
        functools.partial(_nsa_kernel, seq_len=t),
        out_shape=jax.ShapeDtypeStruct((b, t, NSA_HEADS * HEAD_DIM), F32),
        grid=(b, t // Q_BLOCK),
        in_specs=[
            pl.BlockSpec((1, Q_BLOCK, NSA_HEADS * HEAD_DIM), lambda bi, i: (bi, i, C_NQ // 512)),
            pl.BlockSpec((1, 2, NSA_GROUPS, n_piece, HEAD_DIM), lambda bi, i: (bi, 0, 0, 0, 0)),
            seq(kvb + 2), seq(kvb + 3), seq(kvb + 4), seq(kvb + 5),
            pl.BlockSpec((1, Q_BLOCK, LANE), lambda bi, i: (bi, i, C_NG // LANE)),
        ],
        out_specs=pl.BlockSpec((1, Q_BLOCK, NSA_HEADS * HEAD_DIM), lambda bi, i: (bi, i, 0)),
        compiler_params=_cparams(("arbitrary", "arbitrary")),
        name="nsa_attn",
        cost_estimate=pl.CostEstimate(
            flops=b * NSA_HEADS * t * 4 * HEAD_DIM * (t // 2 + WINDOW + Q_BLOCK + n_piece),
            transcendentals=b * NSA_HEADS * t * (t // 2 + WINDOW + Q_BLOCK + n_piece),
            bytes_accessed=4 * b * t * (2 * NSA_HEADS * HEAD_DIM + 5 * LANE)),
    )(proj3, cmp_kv, proj3, proj3, proj3, proj3, proj3)


def _merge_kernel(x_ref, osb_ref, onsa_ref, ohg_ref, gsb_ref, gnsa_ref, ghg_ref,
                  wsb_ref, wnsa_ref, whg_ref, wo_ref, o_ref):
    m = (_sigmoid(gsb_ref[...]) * _dot(osb_ref[...].astype(BF16), wsb_ref[...])
         + _sigmoid(gnsa_ref[...]) * _dot(onsa_ref[...].astype(BF16), wnsa_ref[...])
         + _sigmoid(ghg_ref[...]) * _dot(ohg_ref[...].astype(BF16), whg_ref[...]))
    o_ref[...] = x_ref[...] + _dot(m.astype(BF16), wo_ref[...])


def _merge(x2, o_sb, o_nsa, o_hg, proj2, w_sb, w_nsa, w_hg, w_o):
    n = x2.shape[0]
    tm = 512
    row = lambda w: pl.BlockSpec((tm, w), lambda i: (i, 0))
    gate = lambda j: pl.BlockSpec((tm, D_MODEL), lambda i: (i, j))
    full = lambda a: pl.BlockSpec(a.shape, lambda i: (0, 0))
    return pl.pallas_call(
        _merge_kernel,
        out_shape=jax.ShapeDtypeStruct((n, D_MODEL), F32),
        grid=(n // tm,),
        in_specs=[row(D_MODEL), row(512), row(512), row(512), gate(0), gate(1), gate(2),
                  full(w_sb), full(w_nsa), full(w_hg), full(w_o)],
        out_specs=row(D_MODEL),
        compiler_params=_cparams(("arbitrary",)),
        name="merge_out",
        cost_estimate=pl.CostEstimate(
            flops=2 * n * D_MODEL * (3 * 512 + D_MODEL), transcendentals=3 * n * D_MODEL,
            bytes_accessed=4 * n * (5 * D_MODEL + 3 * 512) + 2 * D_MODEL * (3 * 512 + D_MODEL)),
    )(x2, o_sb, o_nsa, o_hg, proj2, proj2, proj2, w_sb, w_nsa, w_hg, w_o)


def _topk_rows(s, k, payload=None, order=None):
    rows, cols = s.shape
    if order is None:
        order = lax.broadcasted_iota(jnp.int32, (rows, cols), 0).astype(F32)
    out_row = lax.broadcasted_iota(jnp.int32, (k, cols), 0)
    vals = jnp.zeros((k, cols), F32)
    tags = jnp.zeros((k, cols), F32)
    for j in range(k):
        mx = jnp.max(s, axis=0, keepdims=True)
        idx = jnp.min(jnp.where(s == mx, order, jnp.inf), axis=0, keepdims=True)
        onehot = order == idx
        s = jnp.where(onehot, -jnp.inf, s)
        tag = idx if payload is None else jnp.sum(jnp.where(onehot, payload, 0.0), axis=0, keepdims=True)
        vals = jnp.where(out_row == j, mx, vals)
        tags = jnp.where(out_row == j, tag, tags)
    return vals, tags


def _candidate_pairs(k):
    return [(i, j) for i in range(k) for j in range(k) if (i + 1) * (j + 1) <= k]


def _route_kernel(x_ref, g_ref, wqt_ref, keys_ref, h_ref, idx_ref, gate_ref):
    k = PEER_TOPK
    half = PEER_QDIM // 2
    x = x_ref[...]
    h = (x * lax.rsqrt(jnp.mean(x * x, axis=-1, keepdims=True) + EPS) * g_ref[...]).astype(BF16)
    h_ref[...] = h
    qt = _dot_nt(wqt_ref[...], h).astype(BF16)
    gates, ids = [], []
    pairs = _candidate_pairs(k)
    n_pad = -len(pairs) % 8
    tokens = x.shape[0]
    pad_val = jnp.full((n_pad, tokens), -jnp.inf, F32)
    pad_idx = jnp.zeros((n_pad, tokens), F32)
    flat = jnp.concatenate(
        [jnp.full((1, tokens), float(i * k + j), F32) for i, j in pairs]
        + [jnp.full((n_pad, tokens), float(k * k), F32)], axis=0)
    for hd in range(PEER_HEADS):
        tops = []
        for a in range(2):
            r0 = (hd * 2 + a) * half
            s = _dot(keys_ref[a].astype(BF16), qt[r0:r0 + half])
            tops.append(_topk_rows(s, k))
        (s0, i0), (s1, i1) = tops
        cand = jnp.concatenate([s0[i:i + 1] + s1[j:j + 1] for i, j in pairs] + [pad_val], axis=0)
        cidx = jnp.concatenate([i0[i:i + 1] * float(PEER_NKEYS) + i1[j:j + 1] for i, j in pairs]
                               + [pad_idx], axis=0)
        best, eidx = _topk_rows(cand, k, payload=cidx, order=flat)
        e = jnp.exp(best - jnp.max(best, axis=0, keepdims=True))
        gates.append(e / jnp.sum(e, axis=0, keepdims=True))
        ids.append(eidx)
    gate_ref[...] = jnp.concatenate(gates, axis=0).T
    first_word = jnp.concatenate(ids, axis=0).T * float(ROW_PARTS)
    nt = PEER_TOK
    for t in range(x.shape[0] // nt):
        for j in range(ROW_PARTS):
            r0 = (t * ROW_PARTS + j) * nt
            idx_ref[r0:r0 + nt, :] = (first_word[t * nt:(t + 1) * nt] + float(j)).astype(jnp.int32)


def _peer_route(x2, g, w_q_t, sub_keys):
    n = x2.shape[0]
    tm = 128
    return pl.pallas_call(
        _route_kernel,
        out_shape=(jax.ShapeDtypeStruct((n, D_MODEL), BF16),
                   jax.ShapeDtypeStruct((n * ROW_PARTS, PEER_HEADS * PEER_TOPK), jnp.int32),
                   jax.ShapeDtypeStruct((n, PEER_HEADS * PEER_TOPK), F32)),
        grid=(n // tm,),
        in_specs=[pl.BlockSpec((tm, D_MODEL), lambda i: (i, 0)),
                  pl.BlockSpec((1, D_MODEL), lambda i: (0, 0)),
                  pl.BlockSpec(w_q_t.shape, lambda i: (0, 0)),
                  pl.BlockSpec(sub_keys.shape, lambda i: (0, 0, 0))],
        out_specs=(pl.BlockSpec((tm, D_MODEL), lambda i: (i, 0)),
                   pl.BlockSpec((tm * ROW_PARTS, PEER_HEADS * PEER_TOPK), lambda i: (i, 0)),
                   pl.BlockSpec((tm, PEER_HEADS * PEER_TOPK), lambda i: (i, 0))),
        compiler_params=_cparams(("arbitrary",)),
        name="peer_route",
        cost_estimate=pl.CostEstimate(
            flops=n * (2 * D_MODEL * D_MODEL + 4 * PEER_HEADS * PEER_QDIM * PEER_NKEYS
                       + 6 * PEER_HEADS * PEER_TOPK * (2 * PEER_NKEYS + PEER_TOPK * PEER_TOPK)),
            transcendentals=n * PEER_HEADS * PEER_TOPK,
            bytes_accessed=n * (6 * D_MODEL + 8 * PEER_HEADS * PEER_TOPK) + 2 * D_MODEL * D_MODEL),
    )(x2, g, w_q_t, sub_keys)


GATHER_WIN = 128
GATHER_DEPTH = 2


def _pack_table(tab):
    bits = lax.bitcast_convert_type(tab.astype(BF16), jnp.uint16).astype(jnp.uint32)
    words = (bits[:, ROW_WORDS:] << 16) | bits[:, :ROW_WORDS]
    return lax.bitcast_convert_type(words, jnp.int32).reshape(-1, LANE)


def _sc_gather(table, idx):
    m = idx.shape[0] * idx.shape[1]
    mesh = plsc.VectorSubcoreMesh(core_axis_name="core", subcore_axis_name="subcore")

    @pl.kernel(out_type=jax.ShapeDtypeStruct((m, LANE), table.dtype), mesh=mesh,
               scratch_types=[pltpu.SemaphoreType.DMA((GATHER_DEPTH,))],
               cost_estimate=pl.CostEstimate(flops=0, transcendentals=0,
                                             bytes_accessed=m * (2 * LANE + 1) * 4))
    def gather(tab_hbm, idx_hbm, out_hbm, sems):
        def window_copy(idx_vmem, out_vmem, k):
            return pltpu.make_async_copy(tab_hbm.at[idx_vmem.at[k]],
                                         out_vmem.at[pl.ds(k * GATHER_WIN, GATHER_WIN)], sems.at[k])

        def body(idx_vmem, out_vmem):
            for k in range(GATHER_DEPTH):
                window_copy(idx_vmem, out_vmem, k).start()
            for k in range(GATHER_DEPTH):
                window_copy(idx_vmem, out_vmem, k).wait()

        pltpu.emit_pipeline(
            body,
            grid=(m // (GATHER_WIN * GATHER_DEPTH),),
            in_specs=[pl.BlockSpec((GATHER_DEPTH, GATHER_WIN), lambda i: (i, 0))],
            out_specs=[pl.BlockSpec((GATHER_WIN * GATHER_DEPTH, LANE), lambda i: (i, 0))],
            core_axis_name=("core", "subcore"),
            dimension_semantics=(pltpu.PARALLEL,),
            trace_scopes=False,
        )(idx_hbm, out_hbm)

    return gather(table, idx)


HIGH_MASK = -65536


def _unpack(words):
    lo = lax.bitcast_convert_type(lax.shift_left(words, 16), F32)
    hi = lax.bitcast_convert_type(words & HIGH_MASK, F32)
    return lo, hi


def _expert_kernel(x_ref, h_ref, gate_ref, ug_ref, vg_ref, o_ref):
    nt, nr = PEER_TOK, PEER_ROWS
    h = h_ref[...].astype(F32)
    gate_t = jnp.concatenate([gate_ref[...], jnp.zeros((LANE - nt, nr), F32)], axis=0).T
    lane = lax.broadcasted_iota(jnp.int32, (nr, LANE), 1)
    hpre = jnp.zeros((nr, LANE), F32)
    for n in range(nt):
        s = jnp.zeros((nr, LANE), F32)
        for j in range(ROW_PARTS):
            lo, hi = _unpack(ug_ref[pl.ds((j * nt + n) * nr, nr), :])
            s = s + lo * h[n:n + 1, j * LANE:(j + 1) * LANE]
            s = s + hi * h[n:n + 1, ROW_WORDS + j * LANE:ROW_WORDS + (j + 1) * LANE]
        hpre = hpre + jnp.where(lane == n, jnp.sum(s, axis=1, keepdims=True), 0.0)
    act = gate_t * (0.5 * hpre * (1.0 + lax.erf(hpre * (2.0 ** -0.5))))
    rows = []
    for n in range(nt):
        a = act[:, n:n + 1]
        los, his = [], []
        for j in range(ROW_PARTS):
            lo, hi = _unpack(vg_ref[pl.ds((j * nt + n) * nr, nr), :])
            los.append(jnp.sum(a * lo, axis=0, keepdims=True))
            his.append(jnp.sum(a * hi, axis=0, keepdims=True))
        rows.append(jnp.concatenate(los + his, axis=1))
    o_ref[...] = x_ref[...] + jnp.concatenate(rows, axis=0)


def _peer_experts(x2, h2, gate, ug, vg):
    n = x2.shape[0]
    nt, nr = PEER_TOK, PEER_ROWS
    blk = nt * ROW_PARTS * nr
    return pl.pallas_call(
        _expert_kernel,
        out_shape=jax.ShapeDtypeStruct((n, D_MODEL), F32),
        grid=(n // nt,),
        in_specs=[pl.BlockSpec((nt, D_MODEL), lambda i: (i, 0)),
                  pl.BlockSpec((nt, D_MODEL), lambda i: (i, 0)),
                  pl.BlockSpec((nt, nr), lambda i: (i, 0)),
                  pl.BlockSpec((blk, LANE), lambda i: (i, 0)),
                  pl.BlockSpec((blk, LANE), lambda i: (i, 0))],
        out_specs=pl.BlockSpec((nt, D_MODEL), lambda i: (i, 0)),
        compiler_params=_cparams(("arbitrary",)),
        name="peer_experts",
        cost_estimate=pl.CostEstimate(
            flops=n * nr * D_MODEL * 6, transcendentals=n * nr,
            bytes_accessed=n * (2 * nr * ROW_WORDS * 4 + 10 * D_MODEL + 4 * nr)),
    )(x2, h2, gate, ug, vg)


def _norm_kernel(x_ref, g_ref, o_ref):
    x = x_ref[...]
    o_ref[...] = x * lax.rsqrt(jnp.mean(x * x, axis=-1, keepdims=True) + EPS) * g_ref[...]


def _final_norm(x2, g):
    n = x2.shape[0]
    tm = 1024
    return pl.pallas_call(
        _norm_kernel,
        out_shape=jax.ShapeDtypeStruct((n, D_MODEL), F32),
        grid=(n // tm,),
        in_specs=[pl.BlockSpec((tm, D_MODEL), lambda i: (i, 0)),
                  pl.BlockSpec((1, D_MODEL), lambda i: (0, 0))],
        out_specs=pl.BlockSpec((tm, D_MODEL), lambda i: (i, 0)),
        compiler_params=_cparams(("arbitrary",)),
        name="final_norm",
    )(x2, g)


def _permute_w_in(w_in):
    o = [0, 512, 1024, 1536, 2048, 2816, 2840, 3352, 3864, 4376, 4888, IN_WIDTH]
    sb_q, sb_k, sb_v, nsa_q, nsa_kv, nsa_g, hg_q, hg_f, hg_i, hg_g, merge_g = [
        w_in[..., o[j]:o[j + 1]] for j in range(11)]
    pad = jnp.zeros(w_in.shape[:-1] + (IN_PAD - IN_WIDTH,), w_in.dtype)
    return jnp.concatenate(
        [merge_g, sb_q, sb_k, sb_v, nsa_q, hg_q, hg_f, hg_i, hg_g, nsa_kv, nsa_g, pad], axis=-1)


def _mixer_layer(x2, bsz, seq, norm_g, w_in_p, w_cmp, pe, hgrn_norm_l, lower_l,
                 w_sb, w_nsa, w_hg, w_o):
    n = bsz * seq
    proj2 = _inproj(x2, norm_g, w_in_p)
    proj3 = proj2.reshape(bsz, seq, IN_PAD)
    o_sb = _sb_attention(proj3)
    o_hg = _hgrn(proj3, lower_l, hgrn_norm_l)
    cmp_kv = _nsa_compress(proj3, w_cmp, pe)
    o_nsa = _nsa_attention(proj3, cmp_kv)
    return _merge(x2, o_sb.reshape(n, -1), o_nsa.reshape(n, -1), o_hg.reshape(n, -1),
                  proj2, w_sb, w_nsa, w_hg, w_o)


def _peer_layer(x2, norm_g, w_q_t, sub_keys, u_words, v_words):
    h2, idx, gate = _peer_route(x2, norm_g, w_q_t, sub_keys)
    ug = _sc_gather(u_words, idx)
    vg = _sc_gather(v_words, idx)
    return _peer_experts(x2, h2, gate, ug, vg)


BATCH_STREAMS = 4


def kernel(x, norm_mix, norm_ffn, w_in, nsa_w_cmp_k, nsa_w_cmp_v, nsa_cmp_pe, hgrn_norm, hgrn_lower_bounds, w_branch_sb, w_branch_nsa, w_branch_hgrn, w_out, peer_w_q, peer_sub_keys, peer_u, peer_v, norm_final):
    bsz, seq, d = x.shape
    depth = w_in.shape[0]
    lb_soft = jax.nn.softmax(hgrn_lower_bounds.astype(F32), axis=0)
    lower = jnp.cumsum(lb_soft, axis=0) - lb_soft[0]
    w_in_p = _permute_w_in(w_in).astype(BF16)
    streams = BATCH_STREAMS if bsz % BATCH_STREAMS == 0 else 1
    sb = bsz // streams
    xs = [x[s * sb:(s + 1) * sb].reshape(sb * seq, d) for s in range(streams)]
    for l in range(depth):
        w_cmp = jnp.stack([nsa_w_cmp_k[l], nsa_w_cmp_v[l]])
        w_sb, w_nsa, w_hg, w_o = (w_branch_sb[l].astype(BF16), w_branch_nsa[l].astype(BF16),
                                  w_branch_hgrn[l].astype(BF16), w_out[l].astype(BF16))
        w_q_t = peer_w_q[l].T.astype(BF16)
        u_words, v_words = _pack_table(peer_u[l]), _pack_table(peer_v[l])
        xs = [_mixer_layer(xh, sb, seq, norm_mix[l][None], w_in_p[l], w_cmp, nsa_cmp_pe[l],
                           hgrn_norm[l][None], lower[l][None], w_sb, w_nsa, w_hg, w_o) for xh in xs]
        xs = [_peer_layer(xh, norm_ffn[l][None], w_q_t, peer_sub_keys[l], u_words, v_words)
              for xh in xs]
    outs = [_final_norm(xh, norm_final[None]).reshape(sb, seq, d) for xh in xs]
    return jnp.concatenate(outs, axis=0)
```

```python
import functools

import jax
import jax.numpy as jnp
from jax import lax
from jax.experimental import pallas as pl
from jax.experimental.pallas import tpu as pltpu
from jax.experimental.pallas import tpu_sc as plsc

F32 = jnp.float32
BF16 = jnp.bfloat16

D_MODEL = 1024
HEAD_DIM = 64
EPS = 1e-6
NEG = -1e30
FORCE_SCORE = 1e4
Q_BLOCK = 128

SB_HEADS = 8
NSA_HEADS = 8
NSA_GROUPS = 2
NSA_HPG = NSA_HEADS // NSA_GROUPS
CMP_LEN = 32
CMP_STRIDE = 16
SEL_BLOCK = 64
SEL_TOPN = 4
WINDOW = 256
HGRN_HEADS = 4
HGRN_DK = 128
HGRN_CHUNK = 64
HGRN_SUB = 16
PEER_HEADS = 8
PEER_NKEYS = 128
PEER_TOPK = 16
PEER_QDIM = 128

C_MG = 0
C_SBQ = 3072
C_SBK = 3584
C_SBV = 4096
C_NQ = 4608
C_HQ = 5120
C_HF = 5632
C_HI = 6144
C_HG = 6656
C_NKV = 7168
C_NG = 7936
IN_WIDTH = 7960
IN_PAD = 8064
LANE = 128

VMEM_LIMIT = 56 * 1024 * 1024
ROW_WORDS = D_MODEL // 2
ROW_PARTS = ROW_WORDS // LANE
PEER_TOK = 32
PEER_ROWS = PEER_HEADS * PEER_TOPK
SEL_CHUNK = 512


def _cparams(sem):
    return pltpu.CompilerParams(dimension_semantics=sem, vmem_limit_bytes=VMEM_LIMIT)


def _dot(a, b):
    return jnp.dot(a, b, preferred_element_type=F32)


def _dot_nt(a, b):
    return lax.dot_general(a, b, (((1,), (1,)), ((), ())), preferred_element_type=F32)


def _dot_tn(a, b):
    return lax.dot_general(a, b, (((0,), (0,)), ((), ())), preferred_element_type=F32)


def _split2(x):
    hi = x.astype(BF16)
    lo = (x - hi.astype(F32)).astype(BF16)
    return hi, lo


def _sigmoid(x):
    return 1.0 / (1.0 + jnp.exp(-x))


def _inproj_kernel(x_ref, g_ref, w_ref, o_ref):
    x = x_ref[...]
    y = x * lax.rsqrt(jnp.mean(x * x, axis=-1, keepdims=True) + EPS) * g_ref[...]
    o_ref[...] = _dot(y.astype(BF16), w_ref[...])


def _inproj(x2, g, w):
    n = x2.shape[0]
    tm, tn = 512, IN_PAD // 3
    return pl.pallas_call(
        _inproj_kernel,
        out_shape=jax.ShapeDtypeStruct((n, IN_PAD), F32),
        grid=(IN_PAD // tn, n // tm),
        in_specs=[
            pl.BlockSpec((tm, D_MODEL), lambda c, i: (i, 0)),
            pl.BlockSpec((1, D_MODEL), lambda c, i: (0, 0)),
            pl.BlockSpec((D_MODEL, tn), lambda c, i: (0, c)),
        ],
        out_specs=pl.BlockSpec((tm, tn), lambda c, i: (i, c)),
        compiler_params=_cparams(("arbitrary", "arbitrary")),
        name="inproj",
        cost_estimate=pl.CostEstimate(
            flops=2 * n * D_MODEL * IN_PAD, transcendentals=n * (IN_PAD // tn),
            bytes_accessed=4 * n * D_MODEL * (IN_PAD // tn) + 2 * D_MODEL * IN_PAD + 4 * n * IN_PAD),
    )(x2, g, w)


SB_QROWS = 512


def _sb_kernel(q_ref, k_ref, v_ref, o_ref):
    qi = pl.program_id(2)
    nq, nk = SB_QROWS, Q_BLOCK
    per = nq // nk
    scale = HEAD_DIM ** -0.5
    r = lax.broadcasted_iota(jnp.int32, (nq, nk), 0)
    c = lax.broadcasted_iota(jnp.int32, (nq, nk), 1)
    ur = lax.broadcasted_iota(jnp.int32, (nk, nk), 0)
    uc = lax.broadcasted_iota(jnp.int32, (nk, nk), 1)
    upper = (ur > uc).astype(BF16)
    heads = [slice(h * HEAD_DIM, (h + 1) * HEAD_DIM) for h in range(LANE // HEAD_DIM)]
    qs = [(q_ref[0, :, sl] * scale).astype(BF16) for sl in heads]

    def step(qb, kb, vb, c_run, acc, before):
        z = _dot_nt(qb, kb)
        soft = jnp.log(1.0 + jnp.exp(-jnp.abs(z)))
        lsp = jnp.minimum(z, 0.0) - soft
        lsn = -jnp.maximum(z, 0.0) - soft
        if before is not None:
            lsn = jnp.where(before, lsn, 0.0)
        hi, lo = _split2(lsn)
        after = c_run + (_dot(hi, upper) + _dot(lo, upper))
        a = jnp.exp(lsp + after)
        if before is not None:
            a = jnp.where(before, a, 0.0)
        acc = acc + _dot(a.astype(BF16), vb)
        c_run = c_run + jnp.sum(lsn, axis=1, keepdims=True)
        return c_run, acc

    def block(j, carry, before):
        k0 = pl.multiple_of(j * nk, nk)
        kb = k_ref[0, pl.ds(k0, nk), :].astype(BF16)
        vb = v_ref[0, pl.ds(k0, nk), :].astype(BF16)
        out = []
        for h, sl in enumerate(heads):
            out.extend(step(qs[h], kb[:, sl], vb[:, sl], carry[2 * h], carry[2 * h + 1], before))
        return tuple(out)

    carry = tuple(jnp.zeros((nq, w), F32) for _ in heads for w in (1, HEAD_DIM))
    for d in range(per - 1, -1, -1):
        carry = block(qi * per + d, carry, c + d * nk < r)
    carry = lax.fori_loop(0, qi * per, lambda n, cr: block(qi * per - 1 - n, cr, None), carry)
    o_ref[0] = jnp.concatenate([carry[2 * h + 1] for h in range(len(heads))], axis=1).astype(BF16)


def _sb_attention(proj3):
    b, t, _ = proj3.shape
    qb, kb, vb = C_SBQ // LANE, C_SBK // LANE, C_SBV // LANE
    return pl.pallas_call(
        _sb_kernel,
        out_shape=jax.ShapeDtypeStruct((b, t, SB_HEADS * HEAD_DIM), BF16),
        grid=(b, SB_HEADS // 2, t // SB_QROWS),
        in_specs=[
            pl.BlockSpec((1, SB_QROWS, LANE), lambda bi, hp, i: (bi, i, qb + hp)),
            pl.BlockSpec((1, t, LANE), lambda bi, hp, i: (bi, 0, kb + hp)),
            pl.BlockSpec((1, t, LANE), lambda bi, hp, i: (bi, 0, vb + hp)),
        ],
        out_specs=pl.BlockSpec((1, SB_QROWS, LANE), lambda bi, hp, i: (bi, i, hp)),
        compiler_params=_cparams(("arbitrary", "arbitrary", "arbitrary")),
        name="sb_attn",
        cost_estimate=pl.CostEstimate(
            flops=b * SB_HEADS * t * t * (2 * HEAD_DIM + 2 * Q_BLOCK),
            transcendentals=b * SB_HEADS * t * t * 3 // 2,
            bytes_accessed=4 * 4 * b * t * SB_HEADS * HEAD_DIM),
    )(proj3, proj3, proj3)


HGRN_PAIR = 2


def _hgrn_kernel(q_ref, f_ref, i_ref, g_ref, lb_ref, nw_ref, o_ref, st_ref, *, n_chunks):
    ch, sub = HGRN_CHUNK, HGRN_SUB
    st_ref[...] = jnp.zeros_like(st_ref)
    r = lax.broadcasted_iota(jnp.int32, (ch, ch), 0)
    c = lax.broadcasted_iota(jnp.int32, (ch, ch), 1)
    lower = (r >= c).astype(BF16)
    srow = lax.broadcasted_iota(jnp.int32, (sub, HGRN_DK), 0)

    def head_chunk(t0, hh):
        hs = slice(hh * HGRN_DK, (hh + 1) * HGRN_DK)
        lb = lb_ref[:, hs]
        nw = nw_ref[:, hs]
        fz = f_ref[0, pl.ds(t0, ch), hs]
        qz = q_ref[0, pl.ds(t0, ch), hs]
        iv = i_ref[0, pl.ds(t0, ch), hs]
        gz = g_ref[0, pl.ds(t0, ch), hs]
        f = lb + (1.0 - lb) * _sigmoid(fz)
        lf = jnp.log(f)
        kk = 1.0 - f
        qh = qz * _sigmoid(qz)
        hi, lo = _split2(lf)
        bcum = _dot(lower, hi) + _dot(lower, lo)
        st = st_ref[hh]
        o = _dot_nt((qh * jnp.exp(bcum)).astype(BF16), st.astype(BF16))
        ivb = iv.astype(BF16)

        rows = []
        for s in range(ch // sub):
            lo_r, hi_r = s * sub, (s + 1) * sub
            qs, ks, bs, vs = qh[lo_r:hi_r], kk[lo_r:hi_r], bcum[lo_r:hi_r], iv[lo_r:hi_r]
            o_s = o[lo_r:hi_r]
            if s > 0:
                bref = bcum[lo_r - 1:lo_r]
                qd = (qs * jnp.exp(bs - bref)).astype(BF16)
                kd = (kk[:lo_r] * jnp.exp(bref - bcum[:lo_r])).astype(BF16)
                att = _dot_nt(qd, kd)
                o_s = o_s + _dot(att.astype(BF16), ivb[:lo_r])
            diag_rows = []
            for t in range(sub):
                dlt = jnp.where(srow <= t, bs[t:t + 1] - bs, NEG)
                w = (qs[t:t + 1] * ks) * jnp.exp(dlt)
                att_col = jnp.sum(w, axis=1, keepdims=True)
                diag_rows.append(jnp.sum(att_col * vs, axis=0, keepdims=True))
            rows.append(o_s + jnp.concatenate(diag_rows, axis=0))
        o = jnp.concatenate(rows, axis=0)

        b_last = bcum[ch - 1:ch]
        kd = (kk * jnp.exp(b_last - bcum)).astype(BF16)
        st_ref[hh] = st * jnp.exp(b_last) + _dot_tn(ivb, kd)

        y = o * lax.rsqrt(jnp.mean(o * o, axis=-1, keepdims=True) + EPS) * nw
        o_ref[0, pl.ds(t0, ch), hs] = (y * (gz * _sigmoid(gz))).astype(BF16)

    def chunk(ci, carry):
        t0 = pl.multiple_of(ci * ch, ch)
        for hh in range(HGRN_PAIR):
            head_chunk(t0, hh)
        return carry

    lax.fori_loop(0, n_chunks, chunk, 0)


def _hgrn(proj3, lower_l, norm_l):
    b, t, _ = proj3.shape
    wide = HGRN_PAIR * HGRN_DK
    cq, cf, ci, cg = C_HQ // wide, C_HF // wide, C_HI // wide, C_HG // wide
    seq = lambda col: pl.BlockSpec((1, t, wide), lambda bi, h: (bi, 0, col + h))
    vec = pl.BlockSpec((1, wide), lambda bi, h: (0, h))
    return pl.pallas_call(
        functools.partial(_hgrn_kernel, n_chunks=t // HGRN_CHUNK),
        out_shape=jax.ShapeDtypeStruct((b, t, HGRN_HEADS * HGRN_DK), BF16),
        grid=(b, HGRN_HEADS // HGRN_PAIR),
        in_specs=[seq(cq), seq(cf), seq(ci), seq(cg), vec, vec],
        out_specs=pl.BlockSpec((1, t, wide), lambda bi, h: (bi, 0, h)),
        scratch_shapes=[pltpu.VMEM((HGRN_PAIR, HGRN_DK, HGRN_DK), F32)],
        compiler_params=_cparams(("arbitrary", "arbitrary")),
        name="hgrn2",
        cost_estimate=pl.CostEstimate(
            flops=b * HGRN_HEADS * t * HGRN_DK * (6 * HGRN_DK + 6 * HGRN_CHUNK + 4 * HGRN_SUB),
            transcendentals=b * HGRN_HEADS * t * HGRN_DK * (8 + HGRN_SUB),
            bytes_accessed=4 * 5 * b * t * HGRN_HEADS * HGRN_DK),
    )(proj3, proj3, proj3, proj3, lower_l, norm_l)


def _cmp_kernel(x_ref, w_ref, pe_ref, o_ref):
    n_blk = o_ref.shape[3]
    w = w_ref[0].astype(BF16)
    pieces = [x_ref[0, pl.ds(r, n_blk, stride=CMP_STRIDE), :] for r in range(CMP_STRIDE)]
    for g in range(NSA_GROUPS):
        gs = slice(g * HEAD_DIM, (g + 1) * HEAD_DIM)
        first = jnp.zeros((n_blk, HEAD_DIM), F32)
        second = jnp.zeros((n_blk, HEAD_DIM), F32)
        for r in range(CMP_STRIDE):
            rows = pieces[r][:, gs]
            lo, hi = r, CMP_STRIDE + r
            first = first + _dot((rows + pe_ref[lo:lo + 1, :]).astype(BF16),
                                 w[lo * HEAD_DIM:(lo + 1) * HEAD_DIM])
            second = second + _dot((rows + pe_ref[hi:hi + 1, :]).astype(BF16),
                                   w[hi * HEAD_DIM:(hi + 1) * HEAD_DIM])
        o_ref[0, 0, g] = first + pltpu.roll(second, n_blk - 1, 0)


def _nsa_compress(proj3, w_cmp, pe):
    b, t, _ = proj3.shape
    n_blk = t // CMP_STRIDE
    kvb = C_NKV // LANE
    return pl.pallas_call(
        _cmp_kernel,
        out_shape=jax.ShapeDtypeStruct((b, 2, NSA_GROUPS, n_blk, HEAD_DIM), F32),
        grid=(b, 2),
        in_specs=[
            pl.BlockSpec((1, t, LANE), lambda bi, kv: (bi, 0, kvb + kv)),
            pl.BlockSpec((1, CMP_LEN * HEAD_DIM, HEAD_DIM), lambda bi, kv: (kv, 0, 0)),
            pl.BlockSpec((CMP_LEN, HEAD_DIM), lambda bi, kv: (0, 0)),
        ],
        out_specs=pl.BlockSpec((1, 1, NSA_GROUPS, n_blk, HEAD_DIM), lambda bi, kv: (bi, kv, 0, 0, 0)),
        compiler_params=_cparams(("arbitrary", "arbitrary")),
        name="nsa_compress",
    )(proj3, w_cmp, pe)


def _nsa_kernel(q_ref, cmp_ref, ks_ref, vs_ref, kw_ref, vw_ref, g_ref, o_ref, *, seq_len):
    i = pl.program_id(1)
    qn = Q_BLOCK
    scale = HEAD_DIM ** -0.5
    n_blk = seq_len // SEL_BLOCK
    t0 = i * qn
    trow = t0 + lax.broadcasted_iota(jnp.int32, (qn, 1), 0)
    lane = lax.broadcasted_iota(jnp.int32, (qn, LANE), 1)
    lane_f = lane.astype(F32)

    dist_c = trow - (lane * CMP_STRIDE + (CMP_LEN - 1))
    valid_c = dist_c >= 0
    dist_cf = dist_c.astype(F32)
    cr = lax.broadcasted_iota(jnp.int32, (LANE, LANE), 0) * CMP_STRIDE
    nb = lax.broadcasted_iota(jnp.int32, (LANE, LANE), 1)
    overlap = ((cr < nb * SEL_BLOCK + SEL_BLOCK) & (cr + CMP_LEN > nb * SEL_BLOCK)
               & (nb < n_blk)).astype(BF16)
    forced = (lane == trow // SEL_BLOCK) | (lane == 0)
    causal_b = lane * SEL_BLOCK <= trow

    gsig = _sigmoid(g_ref[0])
    trow4 = jnp.concatenate([trow] * NSA_HPG, axis=0)

    span = WINDOW + qn
    kstart = pl.multiple_of(jnp.maximum(i - WINDOW // qn, 0) * qn, qn)
    wpos = kstart + lax.broadcasted_iota(jnp.int32, (NSA_HPG * qn, span), 1)
    dist_w = trow4 - wpos
    valid_w = (dist_w >= 0) & (dist_w < WINDOW)
    dist_wf = dist_w.astype(F32)

    for g in range(NSA_GROUPS):
        gs = slice(g * HEAD_DIM, (g + 1) * HEAD_DIM)
        kc = cmp_ref[0, 0, g].astype(BF16)
        vc = cmp_ref[0, 1, g].astype(BF16)
        q_heads = [(q_ref[0, :, (g * NSA_HPG + p) * HEAD_DIM:(g * NSA_HPG + p + 1) * HEAD_DIM]
                    * scale).astype(BF16) for p in range(NSA_HPG)]
        slopes = [2.0 ** (-(g * NSA_HPG + p + 1)) for p in range(NSA_HPG)]
        slope_col = jnp.concatenate(
            [jnp.full((qn, 1), s, F32) for s in slopes], axis=0)

        psum = jnp.zeros((qn, LANE), F32)
        o_cmp = []
        for p in range(NSA_HPG):
            s = _dot_nt(q_heads[p], kc) - slopes[p] * dist_cf
            s = jnp.where(valid_c, s, NEG)
            m = jnp.max(s, axis=1, keepdims=True)
            e = jnp.where(valid_c, jnp.exp(s - m), 0.0)
            den = jnp.sum(e, axis=1, keepdims=True)
            pc = e / jnp.where(den > 0.0, den, 1.0)
            psum = psum + pc
            o_cmp.append(_dot(pc.astype(BF16), vc))
        hi, lo = _split2(psum)
        imp = _dot(hi, overlap) + _dot(lo, overlap)
        imp = jnp.where(forced, FORCE_SCORE, jnp.where(causal_b, imp, NEG))
        imp = jnp.where(lane < n_blk, imp, -jnp.inf)
        sel = jnp.zeros((qn, LANE), jnp.bool_)
        for _ in range(SEL_TOPN):
            mx = jnp.max(imp, axis=1, keepdims=True)
            idx = jnp.min(jnp.where(imp == mx, lane_f, float(LANE)), axis=1, keepdims=True)
            onehot = lane_f == idx
            sel = sel | onehot
            imp = jnp.where(onehot, -jnp.inf, imp)
        sel_b = jnp.where(sel, 1.0, 0.0).astype(BF16)

        q4 = jnp.concatenate(q_heads, axis=0)

        def sel_chunk(ci, carry):
            m_run, l_run, acc = carry
            k0 = pl.multiple_of(ci * SEL_CHUNK, SEL_CHUNK)
            kk = ks_ref[0, pl.ds(k0, SEL_CHUNK), gs].astype(BF16)
            vv = vs_ref[0, pl.ds(k0, SEL_CHUNK), gs].astype(BF16)
            er = lax.broadcasted_iota(jnp.int32, (LANE, SEL_CHUNK), 0)
            ec = lax.broadcasted_iota(jnp.int32, (LANE, SEL_CHUNK), 1)
            expand = (er == ci * (SEL_CHUNK // SEL_BLOCK) + ec // SEL_BLOCK).astype(BF16)
            kpos = k0 + lax.broadcasted_iota(jnp.int32, (qn, SEL_CHUNK), 1)
            dist = trow - kpos
            mask = (_dot(sel_b, expand) > 0.5) & (dist >= 0)
            mask4 = jnp.concatenate([mask] * NSA_HPG, axis=0)
            dist4 = jnp.concatenate([dist.astype(F32)] * NSA_HPG, axis=0)
            s = _dot_nt(q4, kk) - slope_col * dist4
            s = jnp.where(mask4, s, NEG)
            m_new = jnp.maximum(m_run, jnp.max(s, axis=1, keepdims=True))
            alpha = jnp.exp(m_run - m_new)
            pm = jnp.exp(s - m_new)
            l_new = alpha * l_run + jnp.sum(pm, axis=1, keepdims=True)
            acc = alpha * acc + _dot(pm.astype(BF16), vv)
            return m_new, l_new, acc

        n_sel_chunks = (t0 + qn + SEL_CHUNK - 1) // SEL_CHUNK
        init = (jnp.full((NSA_HPG * qn, 1), NEG, F32), jnp.zeros((NSA_HPG * qn, 1), F32),
                jnp.zeros((NSA_HPG * qn, HEAD_DIM), F32))
        _, l_sel, acc_sel = lax.fori_loop(0, n_sel_chunks, sel_chunk, init)
        o_sel = acc_sel / l_sel

        kw = kw_ref[0, pl.ds(kstart, span), gs].astype(BF16)
        vw = vw_ref[0, pl.ds(kstart, span), gs].astype(BF16)
        s = _dot_nt(q4, kw) - slope_col * dist_wf
        s = jnp.where(valid_w, s, NEG)
        m = jnp.max(s, axis=1, keepdims=True)
        e = jnp.exp(s - m)
        pw = e / jnp.sum(e, axis=1, keepdims=True)
        o_win = _dot(pw.astype(BF16), vw)

        outs = []
        for p in range(NSA_HPG):
            hh = g * NSA_HPG + p
            rows = slice(p * qn, (p + 1) * qn)
            outs.append(gsig[:, 3 * hh:3 * hh + 1] * o_cmp[p]
                        + gsig[:, 3 * hh + 1:3 * hh + 2] * o_sel[rows]
                        + gsig[:, 3 * hh + 2:3 * hh + 3] * o_win[rows])
        width = NSA_HPG * HEAD_DIM
        o_ref[0, :, g * width:(g + 1) * width] = jnp.concatenate(outs, axis=1).astype(BF16)


def _nsa_attention(proj3, cmp_kv):
    b, t, _ = proj3.shape
    kvb = C_NKV // LANE
    seq = lambda col: pl.BlockSpec((1, t, LANE), lambda bi, i: (bi, 0, col))
    n_piece = cmp_kv.shape[3]
    return pl.pallas_call(
        functools.partial(_nsa_kernel, seq_len=t),
        out_shape=jax.ShapeDtypeStruct((b, t, NSA_HEADS * HEAD_DIM), BF16),
        grid=(b, t // Q_BLOCK),
        in_specs=[
            pl.BlockSpec((1, Q_BLOCK, NSA_HEADS * HEAD_DIM), lambda bi, i: (bi, i, C_NQ // 512)),
            pl.BlockSpec((1, 2, NSA_GROUPS, n_piece, HEAD_DIM), lambda bi, i: (bi, 0, 0, 0, 0)),
            seq(kvb + 2), seq(kvb + 3), seq(kvb + 4), seq(kvb + 5),
            pl.BlockSpec((1, Q_BLOCK, LANE), lambda bi, i: (bi, i, C_NG // LANE)),
        ],
        out_specs=pl.BlockSpec((1, Q_BLOCK, NSA_HEADS * HEAD_DIM), lambda bi, i: (bi, i, 0)),
        compiler_params=_cparams(("arbitrary", "arbitrary")),
        name="nsa_attn",
        cost_estimate=pl.CostEstimate(
            flops=b * NSA_HEADS * t * 4 * HEAD_DIM * (t // 2 + WINDOW + Q_BLOCK + n_piece),
            transcendentals=b * NSA_HEADS * t * (t // 2 + WINDOW + Q_BLOCK + n_piece),
            bytes_accessed=4 * b * t * (2 * NSA_HEADS * HEAD_DIM + 5 * LANE)),
    )(proj3, cmp_kv, proj3, proj3, proj3, proj3, proj3)


def _merge_kernel(x_ref, osb_ref, onsa_ref, ohg_ref, gsb_ref, gnsa_ref, ghg_ref,
                  wsb_ref, wnsa_ref, whg_ref, wo_ref, o_ref):
    m = (_sigmoid(gsb_ref[...]) * _dot(osb_ref[...], wsb_ref[...])
         + _sigmoid(gnsa_ref[...]) * _dot(onsa_ref[...], wnsa_ref[...])
         + _sigmoid(ghg_ref[...]) * _dot(ohg_ref[...], whg_ref[...]))
    o_ref[...] = x_ref[...] + _dot(m.astype(BF16), wo_ref[...])


def _merge(x2, o_sb, o_nsa, o_hg, proj2, w_sb, w_nsa, w_hg, w_o):
    n = x2.shape[0]
    tm = 512
    row = lambda w: pl.BlockSpec((tm, w), lambda i: (i, 0))
    gate = lambda j: pl.BlockSpec((tm, D_MODEL), lambda i: (i, C_MG // D_MODEL + j))
    full = lambda a: pl.BlockSpec(a.shape, lambda i: (0, 0))
    return pl.pallas_call(
        _merge_kernel,
        out_shape=jax.ShapeDtypeStruct((n, D_MODEL), F32),
        grid=(n // tm,),
        in_specs=[row(D_MODEL), row(512), row(512), row(512), gate(0), gate(1), gate(2),
                  full(w_sb), full(w_nsa), full(w_hg), full(w_o)],
        out_specs=row(D_MODEL),
        compiler_params=_cparams(("arbitrary",)),
        name="merge_out",
        cost_estimate=pl.CostEstimate(
            flops=2 * n * D_MODEL * (3 * 512 + D_MODEL), transcendentals=3 * n * D_MODEL,
            bytes_accessed=4 * n * 5 * D_MODEL + 2 * n * 3 * 512 + 2 * D_MODEL * (3 * 512 + D_MODEL)),
    )(x2, o_sb, o_nsa, o_hg, proj2, proj2, proj2, w_sb, w_nsa, w_hg, w_o)


def _topk_rows(s, k, payload=None, order=None):
    rows, cols = s.shape
    if order is None:
        order = lax.broadcasted_iota(jnp.int32, (rows, cols), 0).astype(F32)
    out_row = lax.broadcasted_iota(jnp.int32, (k, cols), 0)
    vals = jnp.zeros((k, cols), F32)
    tags = jnp.zeros((k, cols), F32)
    for j in range(k):
        mx = jnp.max(s, axis=0, keepdims=True)
        idx = jnp.min(jnp.where(s == mx, order, jnp.inf), axis=0, keepdims=True)
        onehot = order == idx
        s = jnp.where(onehot, -jnp.inf, s)
        tag = idx if payload is None else jnp.sum(jnp.where(onehot, payload, 0.0), axis=0, keepdims=True)
        vals = jnp.where(out_row == j, mx, vals)
        tags = jnp.where(out_row == j, tag, tags)
    return vals, tags


def _candidate_pairs(k):
    return [(i, j) for i in range(k) for j in range(k) if (i + 1) * (j + 1) <= k]


def _route_kernel(x_ref, g_ref, wqt_ref, keys_ref, h_ref, idx_ref, gate_ref):
    k = PEER_TOPK
    half = PEER_QDIM // 2
    x = x_ref[...]
    h = (x * lax.rsqrt(jnp.mean(x * x, axis=-1, keepdims=True) + EPS) * g_ref[...]).astype(BF16)
    h_ref[...] = h
    qt = _dot_nt(wqt_ref[...], h).astype(BF16)
    gates, ids = [], []
    pairs = _candidate_pairs(k)
    n_pad = -len(pairs) % 8
    tokens = x.shape[0]
    pad_val = jnp.full((n_pad, tokens), -jnp.inf, F32)
    pad_idx = jnp.zeros((n_pad, tokens), F32)
    flat = jnp.concatenate(
        [jnp.full((1, tokens), float(i * k + j), F32) for i, j in pairs]
        + [jnp.full((n_pad, tokens), float(k * k), F32)], axis=0)
    for hd in range(PEER_HEADS):
        tops = []
        for a in range(2):
            r0 = (hd * 2 + a) * half
            s = _dot(keys_ref[a].astype(BF16), qt[r0:r0 + half])
            tops.append(_topk_rows(s, k))
        (s0, i0), (s1, i1) = tops
        cand = jnp.concatenate([s0[i:i + 1] + s1[j:j + 1] for i, j in pairs] + [pad_val], axis=0)
        cidx = jnp.concatenate([i0[i:i + 1] * float(PEER_NKEYS) + i1[j:j + 1] for i, j in pairs]
                               + [pad_idx], axis=0)
        best, eidx = _topk_rows(cand, k, payload=cidx, order=flat)
        e = jnp.exp(best - jnp.max(best, axis=0, keepdims=True))
        gates.append(e / jnp.sum(e, axis=0, keepdims=True))
        ids.append(eidx)
    gate_ref[...] = jnp.concatenate(gates, axis=0).T
    first_word = jnp.concatenate(ids, axis=0).T * float(ROW_PARTS)
    nt = PEER_TOK
    for t in range(x.shape[0] // nt):
        for j in range(ROW_PARTS):
            r0 = (t * ROW_PARTS + j) * nt
            idx_ref[r0:r0 + nt, :] = (first_word[t * nt:(t + 1) * nt] + float(j)).astype(jnp.int32)


def _peer_route(x2, g, w_q_t, sub_keys):
    n = x2.shape[0]
    tm = 128
    return pl.pallas_call(
        _route_kernel,
        out_shape=(jax.ShapeDtypeStruct((n, D_MODEL), BF16),
                   jax.ShapeDtypeStruct((n * ROW_PARTS, PEER_HEADS * PEER_TOPK), jnp.int32),
                   jax.ShapeDtypeStruct((n, PEER_HEADS * PEER_TOPK), F32)),
        grid=(n // tm,),
        in_specs=[pl.BlockSpec((tm, D_MODEL), lambda i: (i, 0)),
                  pl.BlockSpec((1, D_MODEL), lambda i: (0, 0)),
                  pl.BlockSpec(w_q_t.shape, lambda i: (0, 0)),
                  pl.BlockSpec(sub_keys.shape, lambda i: (0, 0, 0))],
        out_specs=(pl.BlockSpec((tm, D_MODEL), lambda i: (i, 0)),
                   pl.BlockSpec((tm * ROW_PARTS, PEER_HEADS * PEER_TOPK), lambda i: (i, 0)),
                   pl.BlockSpec((tm, PEER_HEADS * PEER_TOPK), lambda i: (i, 0))),
        compiler_params=_cparams(("arbitrary",)),
        name="peer_route",
        cost_estimate=pl.CostEstimate(
            flops=n * (2 * D_MODEL * D_MODEL + 4 * PEER_HEADS * PEER_QDIM * PEER_NKEYS
                       + 6 * PEER_HEADS * PEER_TOPK * (2 * PEER_NKEYS + PEER_TOPK * PEER_TOPK)),
            transcendentals=n * PEER_HEADS * PEER_TOPK,
            bytes_accessed=n * (6 * D_MODEL + 8 * PEER_HEADS * PEER_TOPK) + 2 * D_MODEL * D_MODEL),
    )(x2, g, w_q_t, sub_keys)


GATHER_WIN = 128


def _pack_table(tab):
    bits = lax.bitcast_convert_type(tab.astype(BF16), jnp.uint16).astype(jnp.uint32)
    words = (bits[:, ROW_WORDS:] << 16) | bits[:, :ROW_WORDS]
    return lax.bitcast_convert_type(words, jnp.int32).reshape(-1, LANE)


def _sc_gather(table, idx):
    m = idx.shape[0] * idx.shape[1]
    mesh = plsc.VectorSubcoreMesh(core_axis_name="core", subcore_axis_name="subcore")

    @pl.kernel(out_type=jax.ShapeDtypeStruct((m, LANE), table.dtype), mesh=mesh,
               cost_estimate=pl.CostEstimate(flops=0, transcendentals=0,
                                             bytes_accessed=m * (2 * LANE + 1) * 4))
    def gather(tab_hbm, idx_hbm, out_hbm):
        def body(idx_vmem, out_vmem):
            pltpu.sync_copy(tab_hbm.at[idx_vmem.at[0]], out_vmem)

        pltpu.emit_pipeline(
            body,
            grid=(m // GATHER_WIN,),
            in_specs=[pl.BlockSpec((1, GATHER_WIN), lambda i: (i, 0))],
            out_specs=[pl.BlockSpec((GATHER_WIN, LANE), lambda i: (i, 0))],
            core_axis_name=("core", "subcore"),
            dimension_semantics=(pltpu.PARALLEL,),
            trace_scopes=False,
        )(idx_hbm, out_hbm)

    return gather(table, idx)


HIGH_MASK = -65536


def _unpack(words):
    lo = lax.bitcast_convert_type(lax.shift_left(words, 16), F32)
    hi = lax.bitcast_convert_type(words & HIGH_MASK, F32)
    return lo, hi


def _expert_kernel(x_ref, h_ref, gate_ref, ug_ref, vg_ref, o_ref):
    nt, nr = PEER_TOK, PEER_ROWS
    h = h_ref[...].astype(F32)
    gate_t = jnp.concatenate([gate_ref[...], jnp.zeros((LANE - nt, nr), F32)], axis=0).T
    lane = lax.broadcasted_iota(jnp.int32, (nr, LANE), 1)
    hpre = jnp.zeros((nr, LANE), F32)
    for n in range(nt):
        s = jnp.zeros((nr, LANE), F32)
        for j in range(ROW_PARTS):
            lo, hi = _unpack(ug_ref[pl.ds((j * nt + n) * nr, nr), :])
            s = s + lo * h[n:n + 1, j * LANE:(j + 1) * LANE]
            s = s + hi * h[n:n + 1, ROW_WORDS + j * LANE:ROW_WORDS + (j + 1) * LANE]
        hpre = hpre + jnp.where(lane == n, jnp.sum(s, axis=1, keepdims=True), 0.0)
    act = gate_t * (0.5 * hpre * (1.0 + lax.erf(hpre * (2.0 ** -0.5))))
    rows = []
    for n in range(nt):
        a = act[:, n:n + 1]
        los, his = [], []
        for j in range(ROW_PARTS):
            lo, hi = _unpack(vg_ref[pl.ds((j * nt + n) * nr, nr), :])
            los.append(jnp.sum(a * lo, axis=0, keepdims=True))
            his.append(jnp.sum(a * hi, axis=0, keepdims=True))
        rows.append(jnp.concatenate(los + his, axis=1))
    o_ref[...] = x_ref[...] + jnp.concatenate(rows, axis=0)


def _peer_experts(x2, h2, gate, ug, vg):
    n = x2.shape[0]
    nt, nr = PEER_TOK, PEER_ROWS
    blk = nt * ROW_PARTS * nr
    return pl.pallas_call(
        _expert_kernel,
        out_shape=jax.ShapeDtypeStruct((n, D_MODEL), F32),
        grid=(n // nt,),
        in_specs=[pl.BlockSpec((nt, D_MODEL), lambda i: (i, 0)),
                  pl.BlockSpec((nt, D_MODEL), lambda i: (i, 0)),
                  pl.BlockSpec((nt, nr), lambda i: (i, 0)),
                  pl.BlockSpec((blk, LANE), lambda i: (i, 0)),
                  pl.BlockSpec((blk, LANE), lambda i: (i, 0))],
        out_specs=pl.BlockSpec((nt, D_MODEL), lambda i: (i, 0)),
        compiler_params=_cparams(("arbitrary",)),
        name="peer_experts",
        cost_estimate=pl.CostEstimate(
            flops=n * nr * D_MODEL * 6, transcendentals=n * nr,
            bytes_accessed=n * (2 * nr * ROW_WORDS * 4 + 10 * D_MODEL + 4 * nr)),
    )(x2, h2, gate, ug, vg)


def _norm_kernel(x_ref, g_ref, o_ref):
    x = x_ref[...]
    o_ref[...] = x * lax.rsqrt(jnp.mean(x * x, axis=-1, keepdims=True) + EPS) * g_ref[...]


def _final_norm(x2, g):
    n = x2.shape[0]
    tm = 1024
    return pl.pallas_call(
        _norm_kernel,
        out_shape=jax.ShapeDtypeStruct((n, D_MODEL), F32),
        grid=(n // tm,),
        in_specs=[pl.BlockSpec((tm, D_MODEL), lambda i: (i, 0)),
                  pl.BlockSpec((1, D_MODEL), lambda i: (0, 0))],
        out_specs=pl.BlockSpec((tm, D_MODEL), lambda i: (i, 0)),
        compiler_params=_cparams(("arbitrary",)),
        name="final_norm",
    )(x2, g)


def _permute_w_in(w_in):
    o = [0, 512, 1024, 1536, 2048, 2816, 2840, 3352, 3864, 4376, 4888, IN_WIDTH]
    sb_q, sb_k, sb_v, nsa_q, nsa_kv, nsa_g, hg_q, hg_f, hg_i, hg_g, merge_g = [
        w_in[..., o[j]:o[j + 1]] for j in range(11)]
    pad = jnp.zeros(w_in.shape[:-1] + (IN_PAD - IN_WIDTH,), w_in.dtype)
    return jnp.concatenate(
        [merge_g, sb_q, sb_k, sb_v, nsa_q, hg_q, hg_f, hg_i, hg_g, nsa_kv, nsa_g, pad], axis=-1)


def _mixer_layer(x2, bsz, seq, norm_g, w_in_p, w_cmp, pe, hgrn_norm_l, lower_l,
                 w_sb, w_nsa, w_hg, w_o):
    n = bsz * seq
    proj2 = _inproj(x2, norm_g, w_in_p)
    proj3 = proj2.reshape(bsz, seq, IN_PAD)
    o_sb = _sb_attention(proj3)
    o_hg = _hgrn(proj3, lower_l, hgrn_norm_l)
    cmp_kv = _nsa_compress(proj3, w_cmp, pe)
    o_nsa = _nsa_attention(proj3, cmp_kv)
    return _merge(x2, o_sb.reshape(n, -1), o_nsa.reshape(n, -1), o_hg.reshape(n, -1),
                  proj2, w_sb, w_nsa, w_hg, w_o)


def _peer_layer(x2, norm_g, w_q_t, sub_keys, u_words, v_words):
    h2, idx, gate = _peer_route(x2, norm_g, w_q_t, sub_keys)
    ug = _sc_gather(u_words, idx)
    vg = _sc_gather(v_words, idx)
    return _peer_experts(x2, h2, gate, ug, vg)


BATCH_STREAMS = 4


def kernel(x, norm_mix, norm_ffn, w_in, nsa_w_cmp_k, nsa_w_cmp_v, nsa_cmp_pe, hgrn_norm, hgrn_lower_bounds, w_branch_sb, w_branch_nsa, w_branch_hgrn, w_out, peer_w_q, peer_sub_keys, peer_u, peer_v, norm_final):
    bsz, seq, d = x.shape
    depth = w_in.shape[0]
    lb_soft = jax.nn.softmax(hgrn_lower_bounds.astype(F32), axis=0)
    lower = jnp.cumsum(lb_soft, axis=0) - lb_soft[0]
    w_in_p = _permute_w_in(w_in).astype(BF16)
    streams = BATCH_STREAMS if bsz % BATCH_STREAMS == 0 else 1
    sb = bsz // streams
    xs = [x[s * sb:(s + 1) * sb].reshape(sb * seq, d) for s in range(streams)]
    for l in range(depth):
        w_cmp = jnp.stack([nsa_w_cmp_k[l], nsa_w_cmp_v[l]])
        w_sb, w_nsa, w_hg, w_o = (w_branch_sb[l].astype(BF16), w_branch_nsa[l].astype(BF16),
                                  w_branch_hgrn[l].astype(BF16), w_out[l].astype(BF16))
        w_q_t = peer_w_q[l].T.astype(BF16)
        u_words, v_words = _pack_table(peer_u[l]), _pack_table(peer_v[l])
        xs = [_mixer_layer(xh, sb, seq, norm_mix[l][None], w_in_p[l], w_cmp, nsa_cmp_pe[l],
                           hgrn_norm[l][None], lower[l][None], w_sb, w_nsa, w_hg, w_o) for xh in xs]
        xs = [_peer_layer(xh, norm_ffn[l][None], w_q_t, peer_sub_keys[l], u_words, v_words)
              for xh in xs]
    outs = [_final_norm(xh, norm_final[None]).reshape(sb, seq, d) for xh in xs]
    return jnp.concatenate(outs, axis=0)
```

```python
import functools

import jax
import jax.numpy as jnp
from jax import lax
from jax.experimental import pallas as pl
from jax.experimental.pallas import tpu as pltpu
from jax.experimental.pallas import tpu_sc as plsc

F32 = jnp.float32
BF16 = jnp.bfloat16

D_MODEL = 1024
HEAD_DIM = 64
EPS = 1e-6
NEG = -1e30
FORCE_SCORE = 1e4
Q_BLOCK = 128

SB_HEADS = 8
NSA_HEADS = 8
NSA_GROUPS = 2
NSA_HPG = NSA_HEADS // NSA_GROUPS
CMP_LEN = 32
CMP_STRIDE = 16
SEL_BLOCK = 64
SEL_TOPN = 4
WINDOW = 256
HGRN_HEADS = 4
HGRN_DK = 128
HGRN_CHUNK = 64
HGRN_SUB = 16
PEER_HEADS = 8
PEER_NKEYS = 128
PEER_TOPK = 16
PEER_QDIM = 128

C_MG = 0
C_SBQ = 3072
C_SBK = 3584
C_SBV = 4096
C_NQ = 4608
C_HQ = 5120
C_HF = 5632
C_HI = 6144
C_HG = 6656
C_NKV = 7168
C_NG = 7936
IN_WIDTH = 7960
IN_PAD = 8064
LANE = 128

VMEM_LIMIT = 56 * 1024 * 1024
ROW_WORDS = D_MODEL // 2
ROW_PARTS = ROW_WORDS // LANE
PEER_TOK = 32
PEER_ROWS = PEER_HEADS * PEER_TOPK
SEL_CHUNK = 512


def _cparams(sem):
    return pltpu.CompilerParams(dimension_semantics=sem, vmem_limit_bytes=VMEM_LIMIT)


def _dot(a, b):
    return jnp.dot(a, b, preferred_element_type=F32)


def _dot_nt(a, b):
    return lax.dot_general(a, b, (((1,), (1,)), ((), ())), preferred_element_type=F32)


def _dot_tn(a, b):
    return lax.dot_general(a, b, (((0,), (0,)), ((), ())), preferred_element_type=F32)


def _split2(x):
    hi = x.astype(BF16)
    lo = (x - hi.astype(F32)).astype(BF16)
    return hi, lo


def _sigmoid(x):
    return 1.0 / (1.0 + jnp.exp(-x))


def _inproj_kernel(x_ref, g_ref, w_ref, o_ref):
    x = x_ref[...]
    y = x * lax.rsqrt(jnp.mean(x * x, axis=-1, keepdims=True) + EPS) * g_ref[...]
    o_ref[...] = _dot(y.astype(BF16), w_ref[...])


def _inproj(x2, g, w):
    n = x2.shape[0]
    tm, tn = 512, IN_PAD // 3
    return pl.pallas_call(
        _inproj_kernel,
        out_shape=jax.ShapeDtypeStruct((n, IN_PAD), F32),
        grid=(IN_PAD // tn, n // tm),
        in_specs=[
            pl.BlockSpec((tm, D_MODEL), lambda c, i: (i, 0)),
            pl.BlockSpec((1, D_MODEL), lambda c, i: (0, 0)),
            pl.BlockSpec((D_MODEL, tn), lambda c, i: (0, c)),
        ],
        out_specs=pl.BlockSpec((tm, tn), lambda c, i: (i, c)),
        compiler_params=_cparams(("arbitrary", "arbitrary")),
        name="inproj",
        cost_estimate=pl.CostEstimate(
            flops=2 * n * D_MODEL * IN_PAD, transcendentals=n * (IN_PAD // tn),
            bytes_accessed=4 * n * D_MODEL * (IN_PAD // tn) + 2 * D_MODEL * IN_PAD + 4 * n * IN_PAD),
    )(x2, g, w)


SB_QROWS = 512


def _sb_kernel(q_ref, k_ref, v_ref, o_ref):
    qi = pl.program_id(2)
    nq, nk = SB_QROWS, Q_BLOCK
    per = nq // nk
    scale = HEAD_DIM ** -0.5
    r = lax.broadcasted_iota(jnp.int32, (nq, nk), 0)
    c = lax.broadcasted_iota(jnp.int32, (nq, nk), 1)
    ur = lax.broadcasted_iota(jnp.int32, (nk, nk), 0)
    uc = lax.broadcasted_iota(jnp.int32, (nk, nk), 1)
    upper = (ur > uc).astype(BF16)
    heads = [slice(h * HEAD_DIM, (h + 1) * HEAD_DIM) for h in range(LANE // HEAD_DIM)]
    qs = [(q_ref[0, :, sl] * scale).astype(BF16) for sl in heads]

    def step(qb, kb, vb, c_run, acc, before):
        z = _dot_nt(qb, kb)
        soft = jnp.log(1.0 + jnp.exp(-jnp.abs(z)))
        lsp = jnp.minimum(z, 0.0) - soft
        lsn = -jnp.maximum(z, 0.0) - soft
        if before is not None:
            lsn = jnp.where(before, lsn, 0.0)
        hi, lo = _split2(lsn)
        after = c_run + (_dot(hi, upper) + _dot(lo, upper))
        a = jnp.exp(lsp + after)
        if before is not None:
            a = jnp.where(before, a, 0.0)
        acc = acc + _dot(a.astype(BF16), vb)
        c_run = c_run + jnp.sum(lsn, axis=1, keepdims=True)
        return c_run, acc

    def block(j, carry, before):
        k0 = pl.multiple_of(j * nk, nk)
        kb = k_ref[0, pl.ds(k0, nk), :].astype(BF16)
        vb = v_ref[0, pl.ds(k0, nk), :].astype(BF16)
        out = []
        for h, sl in enumerate(heads):
            out.extend(step(qs[h], kb[:, sl], vb[:, sl], carry[2 * h], carry[2 * h + 1], before))
        return tuple(out)

    carry = tuple(jnp.zeros((nq, w), F32) for _ in heads for w in (1, HEAD_DIM))
    for d in range(per - 1, -1, -1):
        carry = block(qi * per + d, carry, c + d * nk < r)
    carry = lax.fori_loop(0, qi * per, lambda n, cr: block(qi * per - 1 - n, cr, None), carry)
    o_ref[0] = jnp.concatenate([carry[2 * h + 1] for h in range(len(heads))], axis=1).astype(BF16)


def _sb_attention(proj3):
    b, t, _ = proj3.shape
    qb, kb, vb = C_SBQ // LANE, C_SBK // LANE, C_SBV // LANE
    return pl.pallas_call(
        _sb_kernel,
        out_shape=jax.ShapeDtypeStruct((b, t, SB_HEADS * HEAD_DIM), BF16),
        grid=(b, SB_HEADS // 2, t // SB_QROWS),
        in_specs=[
            pl.BlockSpec((1, SB_QROWS, LANE), lambda bi, hp, i: (bi, i, qb + hp)),
            pl.BlockSpec((1, t, LANE), lambda bi, hp, i: (bi, 0, kb + hp)),
            pl.BlockSpec((1, t, LANE), lambda bi, hp, i: (bi, 0, vb + hp)),
        ],
        out_specs=pl.BlockSpec((1, SB_QROWS, LANE), lambda bi, hp, i: (bi, i, hp)),
        compiler_params=_cparams(("arbitrary", "arbitrary", "arbitrary")),
        name="sb_attn",
        cost_estimate=pl.CostEstimate(
            flops=b * SB_HEADS * t * t * (2 * HEAD_DIM + 2 * Q_BLOCK),
            transcendentals=b * SB_HEADS * t * t * 3 // 2,
            bytes_accessed=4 * 4 * b * t * SB_HEADS * HEAD_DIM),
    )(proj3, proj3, proj3)


HGRN_PAIR = 4


def _hgrn_kernel(q_ref, f_ref, i_ref, g_ref, lb_ref, nw_ref, o_ref, st_ref, *, n_chunks):
    ch, sub = HGRN_CHUNK, HGRN_SUB
    st_ref[...] = jnp.zeros_like(st_ref)
    r = lax.broadcasted_iota(jnp.int32, (ch, ch), 0)
    c = lax.broadcasted_iota(jnp.int32, (ch, ch), 1)
    lower = (r >= c).astype(BF16)
    srow = lax.broadcasted_iota(jnp.int32, (sub, HGRN_DK), 0)

    def head_chunk(t0, hh):
        hs = slice(hh * HGRN_DK, (hh + 1) * HGRN_DK)
        lb = lb_ref[:, hs]
        nw = nw_ref[:, hs]
        fz = f_ref[0, pl.ds(t0, ch), hs]
        qz = q_ref[0, pl.ds(t0, ch), hs]
        iv = i_ref[0, pl.ds(t0, ch), hs]
        gz = g_ref[0, pl.ds(t0, ch), hs]
        f = lb + (1.0 - lb) * _sigmoid(fz)
        lf = jnp.log(f)
        kk = 1.0 - f
        qh = qz * _sigmoid(qz)
        hi, lo = _split2(lf)
        bcum = _dot(lower, hi) + _dot(lower, lo)
        st = st_ref[hh]
        o = _dot_nt((qh * jnp.exp(bcum)).astype(BF16), st.astype(BF16))
        ivb = iv.astype(BF16)

        rows = []
        for s in range(ch // sub):
            lo_r, hi_r = s * sub, (s + 1) * sub
            qs, ks, bs, vs = qh[lo_r:hi_r], kk[lo_r:hi_r], bcum[lo_r:hi_r], iv[lo_r:hi_r]
            o_s = o[lo_r:hi_r]
            if s > 0:
                bref = bcum[lo_r - 1:lo_r]
                qd = (qs * jnp.exp(bs - bref)).astype(BF16)
                kd = (kk[:lo_r] * jnp.exp(bref - bcum[:lo_r])).astype(BF16)
                att = _dot_nt(qd, kd)
                o_s = o_s + _dot(att.astype(BF16), ivb[:lo_r])
            diag_rows = []
            for t in range(sub):
                dlt = jnp.where(srow <= t, bs[t:t + 1] - bs, NEG)
                w = (qs[t:t + 1] * ks) * jnp.exp(dlt)
                att_col = jnp.sum(w, axis=1, keepdims=True)
                diag_rows.append(jnp.sum(att_col * vs, axis=0, keepdims=True))
            rows.append(o_s + jnp.concatenate(diag_rows, axis=0))
        o = jnp.concatenate(rows, axis=0)

        b_last = bcum[ch - 1:ch]
        kd = (kk * jnp.exp(b_last - bcum)).astype(BF16)
        st_ref[hh] = st * jnp.exp(b_last) + _dot_tn(ivb, kd)

        y = o * lax.rsqrt(jnp.mean(o * o, axis=-1, keepdims=True) + EPS) * nw
        o_ref[0, pl.ds(t0, ch), hs] = (y * (gz * _sigmoid(gz))).astype(BF16)

    def chunk(ci, carry):
        t0 = pl.multiple_of(ci * ch, ch)
        for hh in range(HGRN_PAIR):
            head_chunk(t0, hh)
        return carry

    lax.fori_loop(0, n_chunks, chunk, 0)


def _hgrn(proj3, lower_l, norm_l):
    b, t, _ = proj3.shape
    wide = HGRN_PAIR * HGRN_DK
    cq, cf, ci, cg = C_HQ // wide, C_HF // wide, C_HI // wide, C_HG // wide
    seq = lambda col: pl.BlockSpec((1, t, wide), lambda bi, h: (bi, 0, col + h))
    vec = pl.BlockSpec((1, wide), lambda bi, h: (0, h))
    return pl.pallas_call(
        functools.partial(_hgrn_kernel, n_chunks=t // HGRN_CHUNK),
        out_shape=jax.ShapeDtypeStruct((b, t, HGRN_HEADS * HGRN_DK), BF16),
        grid=(b, HGRN_HEADS // HGRN_PAIR),
        in_specs=[seq(cq), seq(cf), seq(ci), seq(cg), vec, vec],
        out_specs=pl.BlockSpec((1, t, wide), lambda bi, h: (bi, 0, h)),
        scratch_shapes=[pltpu.VMEM((HGRN_PAIR, HGRN_DK, HGRN_DK), F32)],
        compiler_params=_cparams(("arbitrary", "arbitrary")),
        name="hgrn2",
        cost_estimate=pl.CostEstimate(
            flops=b * HGRN_HEADS * t * HGRN_DK * (6 * HGRN_DK + 6 * HGRN_CHUNK + 4 * HGRN_SUB),
            transcendentals=b * HGRN_HEADS * t * HGRN_DK * (8 + HGRN_SUB),
            bytes_accessed=4 * 5 * b * t * HGRN_HEADS * HGRN_DK),
    )(proj3, proj3, proj3, proj3, lower_l, norm_l)


def _cmp_kernel(x_ref, w_ref, pe_ref, o_ref):
    n_blk = o_ref.shape[3]
    w = w_ref[0].astype(BF16)
    pieces = [x_ref[0, pl.ds(r, n_blk, stride=CMP_STRIDE), :] for r in range(CMP_STRIDE)]
    for g in range(NSA_GROUPS):
        gs = slice(g * HEAD_DIM, (g + 1) * HEAD_DIM)
        first = jnp.zeros((n_blk, HEAD_DIM), F32)
        second = jnp.zeros((n_blk, HEAD_DIM), F32)
        for r in range(CMP_STRIDE):
            rows = pieces[r][:, gs]
            lo, hi = r, CMP_STRIDE + r
            first = first + _dot((rows + pe_ref[lo:lo + 1, :]).astype(BF16),
                                 w[lo * HEAD_DIM:(lo + 1) * HEAD_DIM])
            second = second + _dot((rows + pe_ref[hi:hi + 1, :]).astype(BF16),
                                   w[hi * HEAD_DIM:(hi + 1) * HEAD_DIM])
        o_ref[0, 0, g] = first + pltpu.roll(second, n_blk - 1, 0)


def _nsa_compress(proj3, w_cmp, pe):
    b, t, _ = proj3.shape
    n_blk = t // CMP_STRIDE
    kvb = C_NKV // LANE
    return pl.pallas_call(
        _cmp_kernel,
        out_shape=jax.ShapeDtypeStruct((b, 2, NSA_GROUPS, n_blk, HEAD_DIM), F32),
        grid=(b, 2),
        in_specs=[
            pl.BlockSpec((1, t, LANE), lambda bi, kv: (bi, 0, kvb + kv)),
            pl.BlockSpec((1, CMP_LEN * HEAD_DIM, HEAD_DIM), lambda bi, kv: (kv, 0, 0)),
            pl.BlockSpec((CMP_LEN, HEAD_DIM), lambda bi, kv: (0, 0)),
        ],
        out_specs=pl.BlockSpec((1, 1, NSA_GROUPS, n_blk, HEAD_DIM), lambda bi, kv: (bi, kv, 0, 0, 0)),
        compiler_params=_cparams(("arbitrary", "arbitrary")),
        name="nsa_compress",
    )(proj3, w_cmp, pe)


def _nsa_kernel(q_ref, cmp_ref, ks_ref, vs_ref, kw_ref, vw_ref, g_ref, o_ref, *, seq_len):
    i = pl.program_id(1)
    qn = Q_BLOCK
    scale = HEAD_DIM ** -0.5
    n_blk = seq_len // SEL_BLOCK
    t0 = i * qn
    trow = t0 + lax.broadcasted_iota(jnp.int32, (qn, 1), 0)
    lane = lax.broadcasted_iota(jnp.int32, (qn, LANE), 1)
    lane_f = lane.astype(F32)

    dist_c = trow - (lane * CMP_STRIDE + (CMP_LEN - 1))
    valid_c = dist_c >= 0
    dist_cf = dist_c.astype(F32)
    cr = lax.broadcasted_iota(jnp.int32, (LANE, LANE), 0) * CMP_STRIDE
    nb = lax.broadcasted_iota(jnp.int32, (LANE, LANE), 1)
    overlap = ((cr < nb * SEL_BLOCK + SEL_BLOCK) & (cr + CMP_LEN > nb * SEL_BLOCK)
               & (nb < n_blk)).astype(BF16)
    forced = (lane == trow // SEL_BLOCK) | (lane == 0)
    causal_b = lane * SEL_BLOCK <= trow

    gsig = _sigmoid(g_ref[0])
    trow4 = jnp.concatenate([trow] * NSA_HPG, axis=0)

    span = WINDOW + qn
    kstart = pl.multiple_of(jnp.maximum(i - WINDOW // qn, 0) * qn, qn)
    wpos = kstart + lax.broadcasted_iota(jnp.int32, (NSA_HPG * qn, span), 1)
    dist_w = trow4 - wpos
    valid_w = (dist_w >= 0) & (dist_w < WINDOW)
    dist_wf = dist_w.astype(F32)

    for g in range(NSA_GROUPS):
        gs = slice(g * HEAD_DIM, (g + 1) * HEAD_DIM)
        kc = cmp_ref[0, 0, g].astype(BF16)
        vc = cmp_ref[0, 1, g].astype(BF16)
        q_heads = [(q_ref[0, :, (g * NSA_HPG + p) * HEAD_DIM:(g * NSA_HPG + p + 1) * HEAD_DIM]
                    * scale).astype(BF16) for p in range(NSA_HPG)]
        slopes = [2.0 ** (-(g * NSA_HPG + p + 1)) for p in range(NSA_HPG)]
        slope_col = jnp.concatenate(
            [jnp.full((qn, 1), s, F32) for s in slopes], axis=0)

        psum = jnp.zeros((qn, LANE), F32)
        o_cmp = []
        for p in range(NSA_HPG):
            s = _dot_nt(q_heads[p], kc) - slopes[p] * dist_cf
            s = jnp.where(valid_c, s, NEG)
            m = jnp.max(s, axis=1, keepdims=True)
            e = jnp.where(valid_c, jnp.exp(s - m), 0.0)
            den = jnp.sum(e, axis=1, keepdims=True)
            pc = e / jnp.where(den > 0.0, den, 1.0)
            psum = psum + pc
            o_cmp.append(_dot(pc.astype(BF16), vc))
        hi, lo = _split2(psum)
        imp = _dot(hi, overlap) + _dot(lo, overlap)
        imp = jnp.where(forced, FORCE_SCORE, jnp.where(causal_b, imp, NEG))
        imp = jnp.where(lane < n_blk, imp, -jnp.inf)
        sel = jnp.zeros((qn, LANE), jnp.bool_)
        for _ in range(SEL_TOPN):
            mx = jnp.max(imp, axis=1, keepdims=True)
            idx = jnp.min(jnp.where(imp == mx, lane_f, float(LANE)), axis=1, keepdims=True)
            onehot = lane_f == idx
            sel = sel | onehot
            imp = jnp.where(onehot, -jnp.inf, imp)
        sel_b = jnp.where(sel, 1.0, 0.0).astype(BF16)

        q4 = jnp.concatenate(q_heads, axis=0)

        def sel_chunk(ci, carry):
            m_run, l_run, acc = carry
            k0 = pl.multiple_of(ci * SEL_CHUNK, SEL_CHUNK)
            kk = ks_ref[0, pl.ds(k0, SEL_CHUNK), gs].astype(BF16)
            vv = vs_ref[0, pl.ds(k0, SEL_CHUNK), gs].astype(BF16)
            er = lax.broadcasted_iota(jnp.int32, (LANE, SEL_CHUNK), 0)
            ec = lax.broadcasted_iota(jnp.int32, (LANE, SEL_CHUNK), 1)
            expand = (er == ci * (SEL_CHUNK // SEL_BLOCK) + ec // SEL_BLOCK).astype(BF16)
            kpos = k0 + lax.broadcasted_iota(jnp.int32, (qn, SEL_CHUNK), 1)
            dist = trow - kpos
            mask = (_dot(sel_b, expand) > 0.5) & (dist >= 0)
            mask4 = jnp.concatenate([mask] * NSA_HPG, axis=0)
            dist4 = jnp.concatenate([dist.astype(F32)] * NSA_HPG, axis=0)
            s = _dot_nt(q4, kk) - slope_col * dist4
            s = jnp.where(mask4, s, NEG)
            m_new = jnp.maximum(m_run, jnp.max(s, axis=1, keepdims=True))
            alpha = jnp.exp(m_run - m_new)
            pm = jnp.exp(s - m_new)
            l_new = alpha * l_run + jnp.sum(pm, axis=1, keepdims=True)
            acc = alpha * acc + _dot(pm.astype(BF16), vv)
            return m_new, l_new, acc

        n_sel_chunks = (t0 + qn + SEL_CHUNK - 1) // SEL_CHUNK
        init = (jnp.full((NSA_HPG * qn, 1), NEG, F32), jnp.zeros((NSA_HPG * qn, 1), F32),
                jnp.zeros((NSA_HPG * qn, HEAD_DIM), F32))
        _, l_sel, acc_sel = lax.fori_loop(0, n_sel_chunks, sel_chunk, init)
        o_sel = acc_sel / l_sel

        kw = kw_ref[0, pl.ds(kstart, span), gs].astype(BF16)
        vw = vw_ref[0, pl.ds(kstart, span), gs].astype(BF16)
        s = _dot_nt(q4, kw) - slope_col * dist_wf
        s = jnp.where(valid_w, s, NEG)
        m = jnp.max(s, axis=1, keepdims=True)
        e = jnp.exp(s - m)
        pw = e / jnp.sum(e, axis=1, keepdims=True)
        o_win = _dot(pw.astype(BF16), vw)

        outs = []
        for p in range(NSA_HPG):
            hh = g * NSA_HPG + p
            rows = slice(p * qn, (p + 1) * qn)
            outs.append(gsig[:, 3 * hh:3 * hh + 1] * o_cmp[p]
                        + gsig[:, 3 * hh + 1:3 * hh + 2] * o_sel[rows]
                        + gsig[:, 3 * hh + 2:3 * hh + 3] * o_win[rows])
        width = NSA_HPG * HEAD_DIM
        o_ref[0, :, g * width:(g + 1) * width] = jnp.concatenate(outs, axis=1).astype(BF16)


def _nsa_attention(proj3, cmp_kv):
    b, t, _ = proj3.shape
    kvb = C_NKV // LANE
    seq = lambda col: pl.BlockSpec((1, t, LANE), lambda bi, i: (bi, 0, col))
    n_piece = cmp_kv.shape[3]
    return pl.pallas_call(
        functools.partial(_nsa_kernel, seq_len=t),
        out_shape=jax.ShapeDtypeStruct((b, t, NSA_HEADS * HEAD_DIM), BF16),
        grid=(b, t // Q_BLOCK),
        in_specs=[
            pl.BlockSpec((1, Q_BLOCK, NSA_HEADS * HEAD_DIM), lambda bi, i: (bi, i, C_NQ // 512)),
            pl.BlockSpec((1, 2, NSA_GROUPS, n_piece, HEAD_DIM), lambda bi, i: (bi, 0, 0, 0, 0)),
            seq(kvb + 2), seq(kvb + 3), seq(kvb + 4), seq(kvb + 5),
            pl.BlockSpec((1, Q_BLOCK, LANE), lambda bi, i: (bi, i, C_NG // LANE)),
        ],
        out_specs=pl.BlockSpec((1, Q_BLOCK, NSA_HEADS * HEAD_DIM), lambda bi, i: (bi, i, 0)),
        compiler_params=_cparams(("arbitrary", "arbitrary")),
        name="nsa_attn",
        cost_estimate=pl.CostEstimate(
            flops=b * NSA_HEADS * t * 4 * HEAD_DIM * (t // 2 + WINDOW + Q_BLOCK + n_piece),
            transcendentals=b * NSA_HEADS * t * (t // 2 + WINDOW + Q_BLOCK + n_piece),
            bytes_accessed=4 * b * t * (2 * NSA_HEADS * HEAD_DIM + 5 * LANE)),
    )(proj3, cmp_kv, proj3, proj3, proj3, proj3, proj3)


def _merge_kernel(x_ref, osb_ref, onsa_ref, ohg_ref, gsb_ref, gnsa_ref, ghg_ref,
                  wsb_ref, wnsa_ref, whg_ref, wo_ref, o_ref):
    m = (_sigmoid(gsb_ref[...]) * _dot(osb_ref[...], wsb_ref[...])
         + _sigmoid(gnsa_ref[...]) * _dot(onsa_ref[...], wnsa_ref[...])
         + _sigmoid(ghg_ref[...]) * _dot(ohg_ref[...], whg_ref[...]))
    o_ref[...] = x_ref[...] + _dot(m.astype(BF16), wo_ref[...])


def _merge(x2, o_sb, o_nsa, o_hg, proj2, w_sb, w_nsa, w_hg, w_o):
    n = x2.shape[0]
    tm = 512
    row = lambda w: pl.BlockSpec((tm, w), lambda i: (i, 0))
    gate = lambda j: pl.BlockSpec((tm, D_MODEL), lambda i: (i, C_MG // D_MODEL + j))
    full = lambda a: pl.BlockSpec(a.shape, lambda i: (0, 0))
    return pl.pallas_call(
        _merge_kernel,
        out_shape=jax.ShapeDtypeStruct((n, D_MODEL), F32),
        grid=(n // tm,),
        in_specs=[row(D_MODEL), row(512), row(512), row(512), gate(0), gate(1), gate(2),
                  full(w_sb), full(w_nsa), full(w_hg), full(w_o)],
        out_specs=row(D_MODEL),
        compiler_params=_cparams(("arbitrary",)),
        name="merge_out",
        cost_estimate=pl.CostEstimate(
            flops=2 * n * D_MODEL * (3 * 512 + D_MODEL), transcendentals=3 * n * D_MODEL,
            bytes_accessed=4 * n * 5 * D_MODEL + 2 * n * 3 * 512 + 2 * D_MODEL * (3 * 512 + D_MODEL)),
    )(x2, o_sb, o_nsa, o_hg, proj2, proj2, proj2, w_sb, w_nsa, w_hg, w_o)


def _topk_rows(s, k, payload=None, order=None):
    rows, cols = s.shape
    if order is None:
        order = lax.broadcasted_iota(jnp.int32, (rows, cols), 0).astype(F32)
    out_row = lax.broadcasted_iota(jnp.int32, (k, cols), 0)
    vals = jnp.zeros((k, cols), F32)
    tags = jnp.zeros((k, cols), F32)
    for j in range(k):
        mx = jnp.max(s, axis=0, keepdims=True)
        idx = jnp.min(jnp.where(s == mx, order, jnp.inf), axis=0, keepdims=True)
        onehot = order == idx
        s = jnp.where(onehot, -jnp.inf, s)
        tag = idx if payload is None else jnp.sum(jnp.where(onehot, payload, 0.0), axis=0, keepdims=True)
        vals = jnp.where(out_row == j, mx, vals)
        tags = jnp.where(out_row == j, tag, tags)
    return vals, tags


def _candidate_pairs(k):
    return [(i, j) for i in range(k) for j in range(k) if (i + 1) * (j + 1) <= k]


def _route_kernel(x_ref, g_ref, wqt_ref, keys_ref, h_ref, idx_ref, gate_ref):
    k = PEER_TOPK
    half = PEER_QDIM // 2
    x = x_ref[...]
    h = (x * lax.rsqrt(jnp.mean(x * x, axis=-1, keepdims=True) + EPS) * g_ref[...]).astype(BF16)
    h_ref[...] = h
    qt = _dot_nt(wqt_ref[...], h).astype(BF16)
    gates, ids = [], []
    pairs = _candidate_pairs(k)
    n_pad = -len(pairs) % 8
    tokens = x.shape[0]
    pad_val = jnp.full((n_pad, tokens), -jnp.inf, F32)
    pad_idx = jnp.zeros((n_pad, tokens), F32)
    flat = jnp.concatenate(
        [jnp.full((1, tokens), float(i * k + j), F32) for i, j in pairs]
        + [jnp.full((n_pad, tokens), float(k * k), F32)], axis=0)
    for hd in range(PEER_HEADS):
        tops = []
        for a in range(2):
            r0 = (hd * 2 + a) * half
            s = _dot(keys_ref[a].astype(BF16), qt[r0:r0 + half])
            tops.append(_topk_rows(s, k))
        (s0, i0), (s1, i1) = tops
        cand = jnp.concatenate([s0[i:i + 1] + s1[j:j + 1] for i, j in pairs] + [pad_val], axis=0)
        cidx = jnp.concatenate([i0[i:i + 1] * float(PEER_NKEYS) + i1[j:j + 1] for i, j in pairs]
                               + [pad_idx], axis=0)
        best, eidx = _topk_rows(cand, k, payload=cidx, order=flat)
        e = jnp.exp(best - jnp.max(best, axis=0, keepdims=True))
        gates.append(e / jnp.sum(e, axis=0, keepdims=True))
        ids.append(eidx)
    gate_ref[...] = jnp.concatenate(gates, axis=0).T
    first_word = jnp.concatenate(ids, axis=0).T * float(ROW_PARTS)
    nt = PEER_TOK
    for t in range(x.shape[0] // nt):
        for j in range(ROW_PARTS):
            r0 = (t * ROW_PARTS + j) * nt
            idx_ref[r0:r0 + nt, :] = (first_word[t * nt:(t + 1) * nt] + float(j)).astype(jnp.int32)


def _peer_route(x2, g, w_q_t, sub_keys):
    n = x2.shape[0]
    tm = 128
    return pl.pallas_call(
        _route_kernel,
        out_shape=(jax.ShapeDtypeStruct((n, D_MODEL), BF16),
                   jax.ShapeDtypeStruct((n * ROW_PARTS, PEER_HEADS * PEER_TOPK), jnp.int32),
                   jax.ShapeDtypeStruct((n, PEER_HEADS * PEER_TOPK), F32)),
        grid=(n // tm,),
        in_specs=[pl.BlockSpec((tm, D_MODEL), lambda i: (i, 0)),
                  pl.BlockSpec((1, D_MODEL), lambda i: (0, 0)),
                  pl.BlockSpec(w_q_t.shape, lambda i: (0, 0)),
                  pl.BlockSpec(sub_keys.shape, lambda i: (0, 0, 0))],
        out_specs=(pl.BlockSpec((tm, D_MODEL), lambda i: (i, 0)),
                   pl.BlockSpec((tm * ROW_PARTS, PEER_HEADS * PEER_TOPK), lambda i: (i, 0)),
                   pl.BlockSpec((tm, PEER_HEADS * PEER_TOPK), lambda i: (i, 0))),
        compiler_params=_cparams(("arbitrary",)),
        name="peer_route",
        cost_estimate=pl.CostEstimate(
            flops=n * (2 * D_MODEL * D_MODEL + 4 * PEER_HEADS * PEER_QDIM * PEER_NKEYS
                       + 6 * PEER_HEADS * PEER_TOPK * (2 * PEER_NKEYS + PEER_TOPK * PEER_TOPK)),
            transcendentals=n * PEER_HEADS * PEER_TOPK,
            bytes_accessed=n * (6 * D_MODEL + 8 * PEER_HEADS * PEER_TOPK) + 2 * D_MODEL * D_MODEL),
    )(x2, g, w_q_t, sub_keys)


GATHER_WIN = 128


def _pack_table(tab):
    bits = lax.bitcast_convert_type(tab.astype(BF16), jnp.uint16).astype(jnp.uint32)
    words = (bits[:, ROW_WORDS:] << 16) | bits[:, :ROW_WORDS]
    return lax.bitcast_convert_type(words, jnp.int32).reshape(-1, LANE)


def _sc_gather(table, idx):
    m = idx.shape[0] * idx.shape[1]
    mesh = plsc.VectorSubcoreMesh(core_axis_name="core", subcore_axis_name="subcore")

    @pl.kernel(out_type=jax.ShapeDtypeStruct((m, LANE), table.dtype), mesh=mesh,
               cost_estimate=pl.CostEstimate(flops=0, transcendentals=0,
                                             bytes_accessed=m * (2 * LANE + 1) * 4))
    def gather(tab_hbm, idx_hbm, out_hbm):
        def body(idx_vmem, out_vmem):
            pltpu.sync_copy(tab_hbm.at[idx_vmem.at[0]], out_vmem)

        pltpu.emit_pipeline(
            body,
            grid=(m // GATHER_WIN,),
            in_specs=[pl.BlockSpec((1, GATHER_WIN), lambda i: (i, 0))],
            out_specs=[pl.BlockSpec((GATHER_WIN, LANE), lambda i: (i, 0))],
            core_axis_name=("core", "subcore"),
            dimension_semantics=(pltpu.PARALLEL,),
            trace_scopes=False,
        )(idx_hbm, out_hbm)

    return gather(table, idx)


HIGH_MASK = -65536


def _unpack(words):
    lo = lax.bitcast_convert_type(lax.shift_left(words, 16), F32)
    hi = lax.bitcast_convert_type(words & HIGH_MASK, F32)
    return lo, hi


def _expert_kernel(x_ref, h_ref, gate_ref, ug_ref, vg_ref, o_ref):
    nt, nr = PEER_TOK, PEER_ROWS
    h = h_ref[...].astype(F32)
    gate_t = jnp.concatenate([gate_ref[...], jnp.zeros((LANE - nt, nr), F32)], axis=0).T
    lane = lax.broadcasted_iota(jnp.int32, (nr, LANE), 1)
    hpre = jnp.zeros((nr, LANE), F32)
    for n in range(nt):
        s = jnp.zeros((nr, LANE), F32)
        for j in range(ROW_PARTS):
            lo, hi = _unpack(ug_ref[pl.ds((j * nt + n) * nr, nr), :])
            s = s + lo * h[n:n + 1, j * LANE:(j + 1) * LANE]
            s = s + hi * h[n:n + 1, ROW_WORDS + j * LANE:ROW_WORDS + (j + 1) * LANE]
        hpre = hpre + jnp.where(lane == n, jnp.sum(s, axis=1, keepdims=True), 0.0)
    act = gate_t * (0.5 * hpre * (1.0 + lax.erf(hpre * (2.0 ** -0.5))))
    rows = []
    for n in range(nt):
        a = act[:, n:n + 1]
        los, his = [], []
        for j in range(ROW_PARTS):
            lo, hi = _unpack(vg_ref[pl.ds((j * nt + n) * nr, nr), :])
            los.append(jnp.sum(a * lo, axis=0, keepdims=True))
            his.append(jnp.sum(a * hi, axis=0, keepdims=True))
        rows.append(jnp.concatenate(los + his, axis=1))
    o_ref[...] = x_ref[...] + jnp.concatenate(rows, axis=0)


def _peer_experts(x2, h2, gate, ug, vg):
    n = x2.shape[0]
    nt, nr = PEER_TOK, PEER_ROWS
    blk = nt * ROW_PARTS * nr
    return pl.pallas_call(
        _expert_kernel,
        out_shape=jax.ShapeDtypeStruct((n, D_MODEL), F32),
        grid=(n // nt,),
        in_specs=[pl.BlockSpec((nt, D_MODEL), lambda i: (i, 0)),
                  pl.BlockSpec((nt, D_MODEL), lambda i: (i, 0)),
                  pl.BlockSpec((nt, nr), lambda i: (i, 0)),
                  pl.BlockSpec((blk, LANE), lambda i: (i, 0)),
                  pl.BlockSpec((blk, LANE), lambda i: (i, 0))],
        out_specs=pl.BlockSpec((nt, D_MODEL), lambda i: (i, 0)),
        compiler_params=_cparams(("arbitrary",)),
        name="peer_experts",
        cost_estimate=pl.CostEstimate(
            flops=n * nr * D_MODEL * 6, transcendentals=n * nr,
            bytes_accessed=n * (2 * nr * ROW_WORDS * 4 + 10 * D_MODEL + 4 * nr)),
    )(x2, h2, gate, ug, vg)


def _norm_kernel(x_ref, g_ref, o_ref):
    x = x_ref[...]
    o_ref[...] = x * lax.rsqrt(jnp.mean(x * x, axis=-1, keepdims=True) + EPS) * g_ref[...]


def _final_norm(x2, g):
    n = x2.shape[0]
    tm = 1024
    return pl.pallas_call(
        _norm_kernel,
        out_shape=jax.ShapeDtypeStruct((n, D_MODEL), F32),
        grid=(n // tm,),
        in_specs=[pl.BlockSpec((tm, D_MODEL), lambda i: (i, 0)),
                  pl.BlockSpec((1, D_MODEL), lambda i: (0, 0))],
        out_specs=pl.BlockSpec((tm, D_MODEL), lambda i: (i, 0)),
        compiler_params=_cparams(("arbitrary",)),
        name="final_norm",
    )(x2, g)


def _permute_w_in(w_in):
    o = [0, 512, 1024, 1536, 2048, 2816, 2840, 3352, 3864, 4376, 4888, IN_WIDTH]
    sb_q, sb_k, sb_v, nsa_q, nsa_kv, nsa_g, hg_q, hg_f, hg_i, hg_g, merge_g = [
        w_in[..., o[j]:o[j + 1]] for j in range(11)]
    pad = jnp.zeros(w_in.shape[:-1] + (IN_PAD - IN_WIDTH,), w_in.dtype)
    return jnp.concatenate(
        [merge_g, sb_q, sb_k, sb_v, nsa_q, hg_q, hg_f, hg_i, hg_g, nsa_kv, nsa_g, pad], axis=-1)


def _mixer_layer(x2, bsz, seq, norm_g, w_in_p, w_cmp, pe, hgrn_norm_l, lower_l,
                 w_sb, w_nsa, w_hg, w_o):
    n = bsz * seq
    proj2 = _inproj(x2, norm_g, w_in_p)
    proj3 = proj2.reshape(bsz, seq, IN_PAD)
    o_sb = _sb_attention(proj3)
    o_hg = _hgrn(proj3, lower_l, hgrn_norm_l)
    cmp_kv = _nsa_compress(proj3, w_cmp, pe)
    o_nsa = _nsa_attention(proj3, cmp_kv)
    return _merge(x2, o_sb.reshape(n, -1), o_nsa.reshape(n, -1), o_hg.reshape(n, -1),
                  proj2, w_sb, w_nsa, w_hg, w_o)


def _peer_layer(x2, norm_g, w_q_t, sub_keys, u_words, v_words):
    h2, idx, gate = _peer_route(x2, norm_g, w_q_t, sub_keys)
    ug = _sc_gather(u_words, idx)
    vg = _sc_gather(v_words, idx)
    return _peer_experts(x2, h2, gate, ug, vg)


BATCH_STREAMS = 4


def kernel(x, norm_mix, norm_ffn, w_in, nsa_w_cmp_k, nsa_w_cmp_v, nsa_cmp_pe, hgrn_norm, hgrn_lower_bounds, w_branch_sb, w_branch_nsa, w_branch_hgrn, w_out, peer_w_q, peer_sub_keys, peer_u, peer_v, norm_final):
    bsz, seq, d = x.shape
    depth = w_in.shape[0]
    lb_soft = jax.nn.softmax(hgrn_lower_bounds.astype(F32), axis=0)
    lower = jnp.cumsum(lb_soft, axis=0) - lb_soft[0]
    w_in_p = _permute_w_in(w_in).astype(BF16)
    streams = BATCH_STREAMS if bsz % BATCH_STREAMS == 0 else 1
    sb = bsz // streams
    xs = [x[s * sb:(s + 1) * sb].reshape(sb * seq, d) for s in range(streams)]
    for l in range(depth):
        w_cmp = jnp.stack([nsa_w_cmp_k[l], nsa_w_cmp_v[l]])
        w_sb, w_nsa, w_hg, w_o = (w_branch_sb[l].astype(BF16), w_branch_nsa[l].astype(BF16),
                                  w_branch_hgrn[l].astype(BF16), w_out[l].astype(BF16))
        w_q_t = peer_w_q[l].T.astype(BF16)
        u_words, v_words = _pack_table(peer_u[l]), _pack_table(peer_v[l])
        xs = [_mixer_layer(xh, sb, seq, norm_mix[l][None], w_in_p[l], w_cmp, nsa_cmp_pe[l],
                           hgrn_norm[l][None], lower[l][None], w_sb, w_nsa, w_hg, w_o) for xh in xs]
        xs = [_peer_layer(xh, norm_ffn[l][None], w_q_t, peer_sub_keys[l], u_words, v_words)
              for xh in xs]
    outs = [_final_norm(xh, norm_final[None]).reshape(sb, seq, d) for xh in xs]
    return jnp.concatenate(outs, axis=0)
```

```python
import functools

import jax
import jax.numpy as jnp
from jax import lax
from jax.experimental import pallas as pl
from jax.experimental.pallas import tpu as pltpu
from jax.experimental.pallas import tpu_sc as plsc

F32 = jnp.float32
BF16 = jnp.bfloat16

D_MODEL = 1024
HEAD_DIM = 64
EPS = 1e-6
NEG = -1e30
FORCE_SCORE = 1e4
Q_BLOCK = 128

SB_HEADS = 8
NSA_HEADS = 8
NSA_GROUPS = 2
NSA_HPG = NSA_HEADS // NSA_GROUPS
CMP_LEN = 32
CMP_STRIDE = 16
SEL_BLOCK = 64
SEL_TOPN = 4
WINDOW = 256
HGRN_HEADS = 4
HGRN_DK = 128
HGRN_CHUNK = 64
HGRN_SUB = 16
PEER_HEADS = 8
PEER_NKEYS = 128
PEER_TOPK = 16
PEER_QDIM = 128

C_MG = 0
C_SBQ = 3072
C_SBK = 3584
C_SBV = 4096
C_NQ = 4608
C_HQ = 5120
C_HF = 5632
C_HI = 6144
C_HG = 6656
C_NKV = 7168
C_NG = 7936
IN_WIDTH = 7960
IN_PAD = 8064
LANE = 128

VMEM_LIMIT = 56 * 1024 * 1024
ROW_WORDS = D_MODEL // 2
ROW_PARTS = ROW_WORDS // LANE
PEER_TOK = 32
PEER_ROWS = PEER_HEADS * PEER_TOPK
SEL_CHUNK = 512


def _cparams(sem):
    return pltpu.CompilerParams(dimension_semantics=sem, vmem_limit_bytes=VMEM_LIMIT)


def _dot(a, b):
    return jnp.dot(a, b, preferred_element_type=F32)


def _dot_nt(a, b):
    return lax.dot_general(a, b, (((1,), (1,)), ((), ())), preferred_element_type=F32)


def _dot_tn(a, b):
    return lax.dot_general(a, b, (((0,), (0,)), ((), ())), preferred_element_type=F32)


def _split2(x):
    hi = x.astype(BF16)
    lo = (x - hi.astype(F32)).astype(BF16)
    return hi, lo


def _sigmoid(x):
    return 1.0 / (1.0 + jnp.exp(-x))


def _inproj_kernel(x_ref, g_ref, w_ref, o_ref):
    x = x_ref[...]
    y = x * lax.rsqrt(jnp.mean(x * x, axis=-1, keepdims=True) + EPS) * g_ref[...]
    o_ref[...] = _dot(y.astype(BF16), w_ref[...])


def _inproj(x2, g, w):
    n = x2.shape[0]
    tm, tn = 512, IN_PAD // 3
    return pl.pallas_call(
        _inproj_kernel,
        out_shape=jax.ShapeDtypeStruct((n, IN_PAD), F32),
        grid=(IN_PAD // tn, n // tm),
        in_specs=[
            pl.BlockSpec((tm, D_MODEL), lambda c, i: (i, 0)),
            pl.BlockSpec((1, D_MODEL), lambda c, i: (0, 0)),
            pl.BlockSpec((D_MODEL, tn), lambda c, i: (0, c)),
        ],
        out_specs=pl.BlockSpec((tm, tn), lambda c, i: (i, c)),
        compiler_params=_cparams(("arbitrary", "arbitrary")),
        name="inproj",
        cost_estimate=pl.CostEstimate(
            flops=2 * n * D_MODEL * IN_PAD, transcendentals=n * (IN_PAD // tn),
            bytes_accessed=4 * n * D_MODEL * (IN_PAD // tn) + 2 * D_MODEL * IN_PAD + 4 * n * IN_PAD),
    )(x2, g, w)


SB_QROWS = 512
SB_LANES = 256


def _sb_kernel(q_ref, k_ref, v_ref, o_ref):
    qi = pl.program_id(2)
    nq, nk = SB_QROWS, Q_BLOCK
    per = nq // nk
    scale = HEAD_DIM ** -0.5
    r = lax.broadcasted_iota(jnp.int32, (nq, nk), 0)
    c = lax.broadcasted_iota(jnp.int32, (nq, nk), 1)
    ur = lax.broadcasted_iota(jnp.int32, (nk, nk), 0)
    uc = lax.broadcasted_iota(jnp.int32, (nk, nk), 1)
    upper = (ur > uc).astype(BF16)
    heads = [slice(h * HEAD_DIM, (h + 1) * HEAD_DIM) for h in range(SB_LANES // HEAD_DIM)]
    qs = [(q_ref[0, :, sl] * scale).astype(BF16) for sl in heads]

    def step(qb, kb, vb, c_run, acc, before):
        z = _dot_nt(qb, kb)
        soft = jnp.log(1.0 + jnp.exp(-jnp.abs(z)))
        lsp = jnp.minimum(z, 0.0) - soft
        lsn = -jnp.maximum(z, 0.0) - soft
        if before is not None:
            lsn = jnp.where(before, lsn, 0.0)
        hi, lo = _split2(lsn)
        after = c_run + (_dot(hi, upper) + _dot(lo, upper))
        a = jnp.exp(lsp + after)
        if before is not None:
            a = jnp.where(before, a, 0.0)
        acc = acc + _dot(a.astype(BF16), vb)
        c_run = c_run + jnp.sum(lsn, axis=1, keepdims=True)
        return c_run, acc

    def block(j, carry, before):
        k0 = pl.multiple_of(j * nk, nk)
        kb = k_ref[0, pl.ds(k0, nk), :].astype(BF16)
        vb = v_ref[0, pl.ds(k0, nk), :].astype(BF16)
        out = []
        for h, sl in enumerate(heads):
            out.extend(step(qs[h], kb[:, sl], vb[:, sl], carry[2 * h], carry[2 * h + 1], before))
        return tuple(out)

    carry = tuple(jnp.zeros((nq, w), F32) for _ in heads for w in (1, HEAD_DIM))
    for d in range(per - 1, -1, -1):
        carry = block(qi * per + d, carry, c + d * nk < r)
    carry = lax.fori_loop(0, qi * per, lambda n, cr: block(qi * per - 1 - n, cr, None), carry)
    o_ref[0] = jnp.concatenate([carry[2 * h + 1] for h in range(len(heads))], axis=1).astype(BF16)


def _sb_attention(proj3):
    b, t, _ = proj3.shape
    qb, kb, vb = C_SBQ // SB_LANES, C_SBK // SB_LANES, C_SBV // SB_LANES
    return pl.pallas_call(
        _sb_kernel,
        out_shape=jax.ShapeDtypeStruct((b, t, SB_HEADS * HEAD_DIM), BF16),
        grid=(b, SB_HEADS * HEAD_DIM // SB_LANES, t // SB_QROWS),
        in_specs=[
            pl.BlockSpec((1, SB_QROWS, SB_LANES), lambda bi, hp, i: (bi, i, qb + hp)),
            pl.BlockSpec((1, t, SB_LANES), lambda bi, hp, i: (bi, 0, kb + hp)),
            pl.BlockSpec((1, t, SB_LANES), lambda bi, hp, i: (bi, 0, vb + hp)),
        ],
        out_specs=pl.BlockSpec((1, SB_QROWS, SB_LANES), lambda bi, hp, i: (bi, i, hp)),
        compiler_params=_cparams(("arbitrary", "arbitrary", "arbitrary")),
        name="sb_attn",
        cost_estimate=pl.CostEstimate(
            flops=b * SB_HEADS * t * t * (2 * HEAD_DIM + 2 * Q_BLOCK),
            transcendentals=b * SB_HEADS * t * t * 3 // 2,
            bytes_accessed=4 * 4 * b * t * SB_HEADS * HEAD_DIM),
    )(proj3, proj3, proj3)


HGRN_PAIR = 4


def _hgrn_kernel(q_ref, f_ref, i_ref, g_ref, lb_ref, nw_ref, o_ref, st_ref, *, n_chunks):
    ch, sub = HGRN_CHUNK, HGRN_SUB
    st_ref[...] = jnp.zeros_like(st_ref)
    r = lax.broadcasted_iota(jnp.int32, (ch, ch), 0)
    c = lax.broadcasted_iota(jnp.int32, (ch, ch), 1)
    lower = (r >= c).astype(BF16)
    srow = lax.broadcasted_iota(jnp.int32, (sub, HGRN_DK), 0)

    def head_chunk(t0, hh):
        hs = slice(hh * HGRN_DK, (hh + 1) * HGRN_DK)
        lb = lb_ref[:, hs]
        nw = nw_ref[:, hs]
        fz = f_ref[0, pl.ds(t0, ch), hs]
        qz = q_ref[0, pl.ds(t0, ch), hs]
        iv = i_ref[0, pl.ds(t0, ch), hs]
        gz = g_ref[0, pl.ds(t0, ch), hs]
        f = lb + (1.0 - lb) * _sigmoid(fz)
        lf = jnp.log(f)
        kk = 1.0 - f
        qh = qz * _sigmoid(qz)
        hi, lo = _split2(lf)
        bcum = _dot(lower, hi) + _dot(lower, lo)
        st = st_ref[hh]
        o = _dot_nt((qh * jnp.exp(bcum)).astype(BF16), st.astype(BF16))
        ivb = iv.astype(BF16)

        rows = []
        for s in range(ch // sub):
            lo_r, hi_r = s * sub, (s + 1) * sub
            qs, ks, bs, vs = qh[lo_r:hi_r], kk[lo_r:hi_r], bcum[lo_r:hi_r], iv[lo_r:hi_r]
            o_s = o[lo_r:hi_r]
            if s > 0:
                bref = bcum[lo_r - 1:lo_r]
                qd = (qs * jnp.exp(bs - bref)).astype(BF16)
                kd = (kk[:lo_r] * jnp.exp(bref - bcum[:lo_r])).astype(BF16)
                att = _dot_nt(qd, kd)
                o_s = o_s + _dot(att.astype(BF16), ivb[:lo_r])
            diag_rows = []
            for t in range(sub):
                dlt = jnp.where(srow <= t, bs[t:t + 1] - bs, NEG)
                w = (qs[t:t + 1] * ks) * jnp.exp(dlt)
                att_col = jnp.sum(w, axis=1, keepdims=True)
                diag_rows.append(jnp.sum(att_col * vs, axis=0, keepdims=True))
            rows.append(o_s + jnp.concatenate(diag_rows, axis=0))
        o = jnp.concatenate(rows, axis=0)

        b_last = bcum[ch - 1:ch]
        kd = (kk * jnp.exp(b_last - bcum)).astype(BF16)
        st_ref[hh] = st * jnp.exp(b_last) + _dot_tn(ivb, kd)

        y = o * lax.rsqrt(jnp.mean(o * o, axis=-1, keepdims=True) + EPS) * nw
        o_ref[0, pl.ds(t0, ch), hs] = (y * (gz * _sigmoid(gz))).astype(BF16)

    def chunk(ci, carry):
        t0 = pl.multiple_of(ci * ch, ch)
        for hh in range(HGRN_PAIR):
            head_chunk(t0, hh)
        return carry

    lax.fori_loop(0, n_chunks, chunk, 0)


def _hgrn(proj3, lower_l, norm_l):
    b, t, _ = proj3.shape
    wide = HGRN_PAIR * HGRN_DK
    cq, cf, ci, cg = C_HQ // wide, C_HF // wide, C_HI // wide, C_HG // wide
    seq = lambda col: pl.BlockSpec((1, t, wide), lambda bi, h: (bi, 0, col + h))
    vec = pl.BlockSpec((1, wide), lambda bi, h: (0, h))
    return pl.pallas_call(
        functools.partial(_hgrn_kernel, n_chunks=t // HGRN_CHUNK),
        out_shape=jax.ShapeDtypeStruct((b, t, HGRN_HEADS * HGRN_DK), BF16),
        grid=(b, HGRN_HEADS // HGRN_PAIR),
        in_specs=[seq(cq), seq(cf), seq(ci), seq(cg), vec, vec],
        out_specs=pl.BlockSpec((1, t, wide), lambda bi, h: (bi, 0, h)),
        scratch_shapes=[pltpu.VMEM((HGRN_PAIR, HGRN_DK, HGRN_DK), F32)],
        compiler_params=_cparams(("arbitrary", "arbitrary")),
        name="hgrn2",
        cost_estimate=pl.CostEstimate(
            flops=b * HGRN_HEADS * t * HGRN_DK * (6 * HGRN_DK + 6 * HGRN_CHUNK + 4 * HGRN_SUB),
            transcendentals=b * HGRN_HEADS * t * HGRN_DK * (8 + HGRN_SUB),
            bytes_accessed=4 * 5 * b * t * HGRN_HEADS * HGRN_DK),
    )(proj3, proj3, proj3, proj3, lower_l, norm_l)


def _cmp_kernel(x_ref, w_ref, pe_ref, o_ref):
    n_blk = o_ref.shape[3]
    w = w_ref[0].astype(BF16)
    pieces = [x_ref[0, pl.ds(r, n_blk, stride=CMP_STRIDE), :] for r in range(CMP_STRIDE)]
    for g in range(NSA_GROUPS):
        gs = slice(g * HEAD_DIM, (g + 1) * HEAD_DIM)
        first = jnp.zeros((n_blk, HEAD_DIM), F32)
        second = jnp.zeros((n_blk, HEAD_DIM), F32)
        for r in range(CMP_STRIDE):
            rows = pieces[r][:, gs]
            lo, hi = r, CMP_STRIDE + r
            first = first + _dot((rows + pe_ref[lo:lo + 1, :]).astype(BF16),
                                 w[lo * HEAD_DIM:(lo + 1) * HEAD_DIM])
            second = second + _dot((rows + pe_ref[hi:hi + 1, :]).astype(BF16),
                                   w[hi * HEAD_DIM:(hi + 1) * HEAD_DIM])
        o_ref[0, 0, g] = first + pltpu.roll(second, n_blk - 1, 0)


def _nsa_compress(proj3, w_cmp, pe):
    b, t, _ = proj3.shape
    n_blk = t // CMP_STRIDE
    kvb = C_NKV // LANE
    return pl.pallas_call(
        _cmp_kernel,
        out_shape=jax.ShapeDtypeStruct((b, 2, NSA_GROUPS, n_blk, HEAD_DIM), F32),
        grid=(b, 2),
        in_specs=[
            pl.BlockSpec((1, t, LANE), lambda bi, kv: (bi, 0, kvb + kv)),
            pl.BlockSpec((1, CMP_LEN * HEAD_DIM, HEAD_DIM), lambda bi, kv: (kv, 0, 0)),
            pl.BlockSpec((CMP_LEN, HEAD_DIM), lambda bi, kv: (0, 0)),
        ],
        out_specs=pl.BlockSpec((1, 1, NSA_GROUPS, n_blk, HEAD_DIM), lambda bi, kv: (bi, kv, 0, 0, 0)),
        compiler_params=_cparams(("arbitrary", "arbitrary")),
        name="nsa_compress",
    )(proj3, w_cmp, pe)


def _nsa_kernel(q_ref, cmp_ref, ks_ref, vs_ref, kw_ref, vw_ref, g_ref, o_ref, *, seq_len):
    i = pl.program_id(1)
    qn = Q_BLOCK
    scale = HEAD_DIM ** -0.5
    n_blk = seq_len // SEL_BLOCK
    t0 = i * qn
    trow = t0 + lax.broadcasted_iota(jnp.int32, (qn, 1), 0)
    lane = lax.broadcasted_iota(jnp.int32, (qn, LANE), 1)
    lane_f = lane.astype(F32)

    dist_c = trow - (lane * CMP_STRIDE + (CMP_LEN - 1))
    valid_c = dist_c >= 0
    dist_cf = dist_c.astype(F32)
    cr = lax.broadcasted_iota(jnp.int32, (LANE, LANE), 0) * CMP_STRIDE
    nb = lax.broadcasted_iota(jnp.int32, (LANE, LANE), 1)
    overlap = ((cr < nb * SEL_BLOCK + SEL_BLOCK) & (cr + CMP_LEN > nb * SEL_BLOCK)
               & (nb < n_blk)).astype(BF16)
    forced = (lane == trow // SEL_BLOCK) | (lane == 0)
    causal_b = lane * SEL_BLOCK <= trow

    gsig = _sigmoid(g_ref[0])
    trow4 = jnp.concatenate([trow] * NSA_HPG, axis=0)

    span = WINDOW + qn
    kstart = pl.multiple_of(jnp.maximum(i - WINDOW // qn, 0) * qn, qn)
    wpos = kstart + lax.broadcasted_iota(jnp.int32, (NSA_HPG * qn, span), 1)
    dist_w = trow4 - wpos
    valid_w = (dist_w >= 0) & (dist_w < WINDOW)
    dist_wf = dist_w.astype(F32)

    for g in range(NSA_GROUPS):
        gs = slice(g * HEAD_DIM, (g + 1) * HEAD_DIM)
        kc = cmp_ref[0, 0, g].astype(BF16)
        vc = cmp_ref[0, 1, g].astype(BF16)
        q_heads = [(q_ref[0, :, (g * NSA_HPG + p) * HEAD_DIM:(g * NSA_HPG + p + 1) * HEAD_DIM]
                    * scale).astype(BF16) for p in range(NSA_HPG)]
        slopes = [2.0 ** (-(g * NSA_HPG + p + 1)) for p in range(NSA_HPG)]
        slope_col = jnp.concatenate(
            [jnp.full((qn, 1), s, F32) for s in slopes], axis=0)

        psum = jnp.zeros((qn, LANE), F32)
        o_cmp = []
        for p in range(NSA_HPG):
            s = _dot_nt(q_heads[p], kc) - slopes[p] * dist_cf
            s = jnp.where(valid_c, s, NEG)
            m = jnp.max(s, axis=1, keepdims=True)
            e = jnp.where(valid_c, jnp.exp(s - m), 0.0)
            den = jnp.sum(e, axis=1, keepdims=True)
            pc = e / jnp.where(den > 0.0, den, 1.0)
            psum = psum + pc
            o_cmp.append(_dot(pc.astype(BF16), vc))
        hi, lo = _split2(psum)
        imp = _dot(hi, overlap) + _dot(lo, overlap)
        imp = jnp.where(forced, FORCE_SCORE, jnp.where(causal_b, imp, NEG))
        imp = jnp.where(lane < n_blk, imp, -jnp.inf)
        sel = jnp.zeros((qn, LANE), jnp.bool_)
        for _ in range(SEL_TOPN):
            mx = jnp.max(imp, axis=1, keepdims=True)
            idx = jnp.min(jnp.where(imp == mx, lane_f, float(LANE)), axis=1, keepdims=True)
            onehot = lane_f == idx
            sel = sel | onehot
            imp = jnp.where(onehot, -jnp.inf, imp)
        sel_b = jnp.where(sel, 1.0, 0.0).astype(BF16)

        q4 = jnp.concatenate(q_heads, axis=0)

        def sel_chunk(ci, carry):
            m_run, l_run, acc = carry
            k0 = pl.multiple_of(ci * SEL_CHUNK, SEL_CHUNK)
            kk = ks_ref[0, pl.ds(k0, SEL_CHUNK), gs].astype(BF16)
            vv = vs_ref[0, pl.ds(k0, SEL_CHUNK), gs].astype(BF16)
            er = lax.broadcasted_iota(jnp.int32, (LANE, SEL_CHUNK), 0)
            ec = lax.broadcasted_iota(jnp.int32, (LANE, SEL_CHUNK), 1)
            expand = (er == ci * (SEL_CHUNK // SEL_BLOCK) + ec // SEL_BLOCK).astype(BF16)
            kpos = k0 + lax.broadcasted_iota(jnp.int32, (qn, SEL_CHUNK), 1)
            dist = trow - kpos
            mask = (_dot(sel_b, expand) > 0.5) & (dist >= 0)
            mask4 = jnp.concatenate([mask] * NSA_HPG, axis=0)
            dist4 = jnp.concatenate([dist.astype(F32)] * NSA_HPG, axis=0)
            s = _dot_nt(q4, kk) - slope_col * dist4
            s = jnp.where(mask4, s, NEG)
            m_new = jnp.maximum(m_run, jnp.max(s, axis=1, keepdims=True))
            alpha = jnp.exp(m_run - m_new)
            pm = jnp.exp(s - m_new)
            l_new = alpha * l_run + jnp.sum(pm, axis=1, keepdims=True)
            acc = alpha * acc + _dot(pm.astype(BF16), vv)
            return m_new, l_new, acc

        n_sel_chunks = (t0 + qn + SEL_CHUNK - 1) // SEL_CHUNK
        init = (jnp.full((NSA_HPG * qn, 1), NEG, F32), jnp.zeros((NSA_HPG * qn, 1), F32),
                jnp.zeros((NSA_HPG * qn, HEAD_DIM), F32))
        _, l_sel, acc_sel = lax.fori_loop(0, n_sel_chunks, sel_chunk, init)
        o_sel = acc_sel / l_sel

        kw = kw_ref[0, pl.ds(kstart, span), gs].astype(BF16)
        vw = vw_ref[0, pl.ds(kstart, span), gs].astype(BF16)
        s = _dot_nt(q4, kw) - slope_col * dist_wf
        s = jnp.where(valid_w, s, NEG)
        m = jnp.max(s, axis=1, keepdims=True)
        e = jnp.exp(s - m)
        pw = e / jnp.sum(e, axis=1, keepdims=True)
        o_win = _dot(pw.astype(BF16), vw)

        outs = []
        for p in range(NSA_HPG):
            hh = g * NSA_HPG + p
            rows = slice(p * qn, (p + 1) * qn)
            outs.append(gsig[:, 3 * hh:3 * hh + 1] * o_cmp[p]
                        + gsig[:, 3 * hh + 1:3 * hh + 2] * o_sel[rows]
                        + gsig[:, 3 * hh + 2:3 * hh + 3] * o_win[rows])
        width = NSA_HPG * HEAD_DIM
        o_ref[0, :, g * width:(g + 1) * width] = jnp.concatenate(outs, axis=1).astype(BF16)


def _nsa_attention(proj3, cmp_kv):
    b, t, _ = proj3.shape
    kvb = C_NKV // LANE
    seq = lambda col: pl.BlockSpec((1, t, LANE), lambda bi, i: (bi, 0, col))
    n_piece = cmp_kv.shape[3]
    return pl.pallas_call(
        functools.partial(_nsa_kernel, seq_len=t),
        out_shape=jax.ShapeDtypeStruct((b, t, NSA_HEADS * HEAD_DIM), BF16),
        grid=(b, t // Q_BLOCK),
        in_specs=[
            pl.BlockSpec((1, Q_BLOCK, NSA_HEADS * HEAD_DIM), lambda bi, i: (bi, i, C_NQ // 512)),
            pl.BlockSpec((1, 2, NSA_GROUPS, n_piece, HEAD_DIM), lambda bi, i: (bi, 0, 0, 0, 0)),
            seq(kvb + 2), seq(kvb + 3), seq(kvb + 4), seq(kvb + 5),
            pl.BlockSpec((1, Q_BLOCK, LANE), lambda bi, i: (bi, i, C_NG // LANE)),
        ],
        out_specs=pl.BlockSpec((1, Q_BLOCK, NSA_HEADS * HEAD_DIM), lambda bi, i: (bi, i, 0)),
        compiler_params=_cparams(("arbitrary", "arbitrary")),
        name="nsa_attn",
        cost_estimate=pl.CostEstimate(
            flops=b * NSA_HEADS * t * 4 * HEAD_DIM * (t // 2 + WINDOW + Q_BLOCK + n_piece),
            transcendentals=b * NSA_HEADS * t * (t // 2 + WINDOW + Q_BLOCK + n_piece),
            bytes_accessed=4 * b * t * (2 * NSA_HEADS * HEAD_DIM + 5 * LANE)),
    )(proj3, cmp_kv, proj3, proj3, proj3, proj3, proj3)


def _merge_kernel(x_ref, osb_ref, onsa_ref, ohg_ref, gsb_ref, gnsa_ref, ghg_ref,
                  wsb_ref, wnsa_ref, whg_ref, wo_ref, o_ref):
    m = (_sigmoid(gsb_ref[...]) * _dot(osb_ref[...], wsb_ref[...])
         + _sigmoid(gnsa_ref[...]) * _dot(onsa_ref[...], wnsa_ref[...])
         + _sigmoid(ghg_ref[...]) * _dot(ohg_ref[...], whg_ref[...]))
    o_ref[...] = x_ref[...] + _dot(m.astype(BF16), wo_ref[...])


def _merge(x2, o_sb, o_nsa, o_hg, proj2, w_sb, w_nsa, w_hg, w_o):
    n = x2.shape[0]
    tm = 512
    row = lambda w: pl.BlockSpec((tm, w), lambda i: (i, 0))
    gate = lambda j: pl.BlockSpec((tm, D_MODEL), lambda i: (i, C_MG // D_MODEL + j))
    full = lambda a: pl.BlockSpec(a.shape, lambda i: (0, 0))
    return pl.pallas_call(
        _merge_kernel,
        out_shape=jax.ShapeDtypeStruct((n, D_MODEL), F32),
        grid=(n // tm,),
        in_specs=[row(D_MODEL), row(512), row(512), row(512), gate(0), gate(1), gate(2),
                  full(w_sb), full(w_nsa), full(w_hg), full(w_o)],
        out_specs=row(D_MODEL),
        compiler_params=_cparams(("arbitrary",)),
        name="merge_out",
        cost_estimate=pl.CostEstimate(
            flops=2 * n * D_MODEL * (3 * 512 + D_MODEL), transcendentals=3 * n * D_MODEL,
            bytes_accessed=4 * n * 5 * D_MODEL + 2 * n * 3 * 512 + 2 * D_MODEL * (3 * 512 + D_MODEL)),
    )(x2, o_sb, o_nsa, o_hg, proj2, proj2, proj2, w_sb, w_nsa, w_hg, w_o)


def _topk_rows(s, k, payload=None, order=None):
    rows, cols = s.shape
    if order is None:
        order = lax.broadcasted_iota(jnp.int32, (rows, cols), 0).astype(F32)
    out_row = lax.broadcasted_iota(jnp.int32, (k, cols), 0)
    vals = jnp.zeros((k, cols), F32)
    tags = jnp.zeros((k, cols), F32)
    for j in range(k):
        mx = jnp.max(s, axis=0, keepdims=True)
        idx = jnp.min(jnp.where(s == mx, order, jnp.inf), axis=0, keepdims=True)
        onehot = order == idx
        s = jnp.where(onehot, -jnp.inf, s)
        tag = idx if payload is None else jnp.sum(jnp.where(onehot, payload, 0.0), axis=0, keepdims=True)
        vals = jnp.where(out_row == j, mx, vals)
        tags = jnp.where(out_row == j, tag, tags)
    return vals, tags


def _candidate_pairs(k):
    return [(i, j) for i in range(k) for j in range(k) if (i + 1) * (j + 1) <= k]


def _route_kernel(x_ref, g_ref, wqt_ref, keys_ref, h_ref, idx_ref, gate_ref):
    k = PEER_TOPK
    half = PEER_QDIM // 2
    x = x_ref[...]
    h = (x * lax.rsqrt(jnp.mean(x * x, axis=-1, keepdims=True) + EPS) * g_ref[...]).astype(BF16)
    h_ref[...] = h
    qt = _dot_nt(wqt_ref[...], h).astype(BF16)
    gates, ids = [], []
    pairs = _candidate_pairs(k)
    n_pad = -len(pairs) % 8
    tokens = x.shape[0]
    pad_val = jnp.full((n_pad, tokens), -jnp.inf, F32)
    pad_idx = jnp.zeros((n_pad, tokens), F32)
    flat = jnp.concatenate(
        [jnp.full((1, tokens), float(i * k + j), F32) for i, j in pairs]
        + [jnp.full((n_pad, tokens), float(k * k), F32)], axis=0)
    for hd in range(PEER_HEADS):
        tops = []
        for a in range(2):
            r0 = (hd * 2 + a) * half
            s = _dot(keys_ref[a].astype(BF16), qt[r0:r0 + half])
            tops.append(_topk_rows(s, k))
        (s0, i0), (s1, i1) = tops
        cand = jnp.concatenate([s0[i:i + 1] + s1[j:j + 1] for i, j in pairs] + [pad_val], axis=0)
        cidx = jnp.concatenate([i0[i:i + 1] * float(PEER_NKEYS) + i1[j:j + 1] for i, j in pairs]
                               + [pad_idx], axis=0)
        best, eidx = _topk_rows(cand, k, payload=cidx, order=flat)
        e = jnp.exp(best - jnp.max(best, axis=0, keepdims=True))
        gates.append(e / jnp.sum(e, axis=0, keepdims=True))
        ids.append(eidx)
    gate_ref[...] = jnp.concatenate(gates, axis=0).T
    first_word = jnp.concatenate(ids, axis=0).T * float(ROW_PARTS)
    nt = PEER_TOK
    for t in range(x.shape[0] // nt):
        for j in range(ROW_PARTS):
            r0 = (t * ROW_PARTS + j) * nt
            idx_ref[r0:r0 + nt, :] = (first_word[t * nt:(t + 1) * nt] + float(j)).astype(jnp.int32)


def _peer_route(x2, g, w_q_t, sub_keys):
    n = x2.shape[0]
    tm = 128
    return pl.pallas_call(
        _route_kernel,
        out_shape=(jax.ShapeDtypeStruct((n, D_MODEL), BF16),
                   jax.ShapeDtypeStruct((n * ROW_PARTS, PEER_HEADS * PEER_TOPK), jnp.int32),
                   jax.ShapeDtypeStruct((n, PEER_HEADS * PEER_TOPK), F32)),
        grid=(n // tm,),
        in_specs=[pl.BlockSpec((tm, D_MODEL), lambda i: (i, 0)),
                  pl.BlockSpec((1, D_MODEL), lambda i: (0, 0)),
                  pl.BlockSpec(w_q_t.shape, lambda i: (0, 0)),
                  pl.BlockSpec(sub_keys.shape, lambda i: (0, 0, 0))],
        out_specs=(pl.BlockSpec((tm, D_MODEL), lambda i: (i, 0)),
                   pl.BlockSpec((tm * ROW_PARTS, PEER_HEADS * PEER_TOPK), lambda i: (i, 0)),
                   pl.BlockSpec((tm, PEER_HEADS * PEER_TOPK), lambda i: (i, 0))),
        compiler_params=_cparams(("arbitrary",)),
        name="peer_route",
        cost_estimate=pl.CostEstimate(
            flops=n * (2 * D_MODEL * D_MODEL + 4 * PEER_HEADS * PEER_QDIM * PEER_NKEYS
                       + 6 * PEER_HEADS * PEER_TOPK * (2 * PEER_NKEYS + PEER_TOPK * PEER_TOPK)),
            transcendentals=n * PEER_HEADS * PEER_TOPK,
            bytes_accessed=n * (6 * D_MODEL + 8 * PEER_HEADS * PEER_TOPK) + 2 * D_MODEL * D_MODEL),
    )(x2, g, w_q_t, sub_keys)


GATHER_WIN = 128


def _pack_table(tab):
    bits = lax.bitcast_convert_type(tab.astype(BF16), jnp.uint16).astype(jnp.uint32)
    words = (bits[:, ROW_WORDS:] << 16) | bits[:, :ROW_WORDS]
    return lax.bitcast_convert_type(words, jnp.int32).reshape(-1, LANE)


def _sc_gather(table, idx):
    m = idx.shape[0] * idx.shape[1]
    mesh = plsc.VectorSubcoreMesh(core_axis_name="core", subcore_axis_name="subcore")

    @pl.kernel(out_type=jax.ShapeDtypeStruct((m, LANE), table.dtype), mesh=mesh,
               cost_estimate=pl.CostEstimate(flops=0, transcendentals=0,
                                             bytes_accessed=m * (2 * LANE + 1) * 4))
    def gather(tab_hbm, idx_hbm, out_hbm):
        def body(idx_vmem, out_vmem):
            pltpu.sync_copy(tab_hbm.at[idx_vmem.at[0]], out_vmem)

        pltpu.emit_pipeline(
            body,
            grid=(m // GATHER_WIN,),
            in_specs=[pl.BlockSpec((1, GATHER_WIN), lambda i: (i, 0))],
            out_specs=[pl.BlockSpec((GATHER_WIN, LANE), lambda i: (i, 0))],
            core_axis_name=("core", "subcore"),
            dimension_semantics=(pltpu.PARALLEL,),
            trace_scopes=False,
        )(idx_hbm, out_hbm)

    return gather(table, idx)


HIGH_MASK = -65536


def _unpack(words):
    lo = lax.bitcast_convert_type(lax.shift_left(words, 16), F32)
    hi = lax.bitcast_convert_type(words & HIGH_MASK, F32)
    return lo, hi


def _expert_kernel(x_ref, h_ref, gate_ref, ug_ref, vg_ref, o_ref):
    nt, nr = PEER_TOK, PEER_ROWS
    h = h_ref[...].astype(F32)
    gate_t = jnp.concatenate([gate_ref[...], jnp.zeros((LANE - nt, nr), F32)], axis=0).T
    lane = lax.broadcasted_iota(jnp.int32, (nr, LANE), 1)
    hpre = jnp.zeros((nr, LANE), F32)
    for n in range(nt):
        s = jnp.zeros((nr, LANE), F32)
        for j in range(ROW_PARTS):
            lo, hi = _unpack(ug_ref[pl.ds((j * nt + n) * nr, nr), :])
            s = s + lo * h[n:n + 1, j * LANE:(j + 1) * LANE]
            s = s + hi * h[n:n + 1, ROW_WORDS + j * LANE:ROW_WORDS + (j + 1) * LANE]
        hpre = hpre + jnp.where(lane == n, jnp.sum(s, axis=1, keepdims=True), 0.0)
    act = gate_t * (0.5 * hpre * (1.0 + lax.erf(hpre * (2.0 ** -0.5))))
    rows = []
    for n in range(nt):
        a = act[:, n:n + 1]
        los, his = [], []
        for j in range(ROW_PARTS):
            lo, hi = _unpack(vg_ref[pl.ds((j * nt + n) * nr, nr), :])
            los.append(jnp.sum(a * lo, axis=0, keepdims=True))
            his.append(jnp.sum(a * hi, axis=0, keepdims=True))
        rows.append(jnp.concatenate(los + his, axis=1))
    o_ref[...] = x_ref[...] + jnp.concatenate(rows, axis=0)


def _peer_experts(x2, h2, gate, ug, vg):
    n = x2.shape[0]
    nt, nr = PEER_TOK, PEER_ROWS
    blk = nt * ROW_PARTS * nr
    return pl.pallas_call(
        _expert_kernel,
        out_shape=jax.ShapeDtypeStruct((n, D_MODEL), F32),
        grid=(n // nt,),
        in_specs=[pl.BlockSpec((nt, D_MODEL), lambda i: (i, 0)),
                  pl.BlockSpec((nt, D_MODEL), lambda i: (i, 0)),
                  pl.BlockSpec((nt, nr), lambda i: (i, 0)),
                  pl.BlockSpec((blk, LANE), lambda i: (i, 0)),
                  pl.BlockSpec((blk, LANE), lambda i: (i, 0))],
        out_specs=pl.BlockSpec((nt, D_MODEL), lambda i: (i, 0)),
        compiler_params=_cparams(("arbitrary",)),
        name="peer_experts",
        cost_estimate=pl.CostEstimate(
            flops=n * nr * D_MODEL * 6, transcendentals=n * nr,
            bytes_accessed=n * (2 * nr * ROW_WORDS * 4 + 10 * D_MODEL + 4 * nr)),
    )(x2, h2, gate, ug, vg)


def _norm_kernel(x_ref, g_ref, o_ref):
    x = x_ref[...]
    o_ref[...] = x * lax.rsqrt(jnp.mean(x * x, axis=-1, keepdims=True) + EPS) * g_ref[...]


def _final_norm(x2, g):
    n = x2.shape[0]
    tm = 1024
    return pl.pallas_call(
        _norm_kernel,
        out_shape=jax.ShapeDtypeStruct((n, D_MODEL), F32),
        grid=(n // tm,),
        in_specs=[pl.BlockSpec((tm, D_MODEL), lambda i: (i, 0)),
                  pl.BlockSpec((1, D_MODEL), lambda i: (0, 0))],
        out_specs=pl.BlockSpec((tm, D_MODEL), lambda i: (i, 0)),
        compiler_params=_cparams(("arbitrary",)),
        name="final_norm",
    )(x2, g)


def _permute_w_in(w_in):
    o = [0, 512, 1024, 1536, 2048, 2816, 2840, 3352, 3864, 4376, 4888, IN_WIDTH]
    sb_q, sb_k, sb_v, nsa_q, nsa_kv, nsa_g, hg_q, hg_f, hg_i, hg_g, merge_g = [
        w_in[..., o[j]:o[j + 1]] for j in range(11)]
    pad = jnp.zeros(w_in.shape[:-1] + (IN_PAD - IN_WIDTH,), w_in.dtype)
    return jnp.concatenate(
        [merge_g, sb_q, sb_k, sb_v, nsa_q, hg_q, hg_f, hg_i, hg_g, nsa_kv, nsa_g, pad], axis=-1)


def _mixer_layer(x2, bsz, seq, norm_g, w_in_p, w_cmp, pe, hgrn_norm_l, lower_l,
                 w_sb, w_nsa, w_hg, w_o):
    n = bsz * seq
    proj2 = _inproj(x2, norm_g, w_in_p)
    proj3 = proj2.reshape(bsz, seq, IN_PAD)
    o_sb = _sb_attention(proj3)
    o_hg = _hgrn(proj3, lower_l, hgrn_norm_l)
    cmp_kv = _nsa_compress(proj3, w_cmp, pe)
    o_nsa = _nsa_attention(proj3, cmp_kv)
    return _merge(x2, o_sb.reshape(n, -1), o_nsa.reshape(n, -1), o_hg.reshape(n, -1),
                  proj2, w_sb, w_nsa, w_hg, w_o)


def _peer_layer(x2, norm_g, w_q_t, sub_keys, u_words, v_words):
    h2, idx, gate = _peer_route(x2, norm_g, w_q_t, sub_keys)
    ug = _sc_gather(u_words, idx)
    vg = _sc_gather(v_words, idx)
    return _peer_experts(x2, h2, gate, ug, vg)


BATCH_STREAMS = 4


def kernel(x, norm_mix, norm_ffn, w_in, nsa_w_cmp_k, nsa_w_cmp_v, nsa_cmp_pe, hgrn_norm, hgrn_lower_bounds, w_branch_sb, w_branch_nsa, w_branch_hgrn, w_out, peer_w_q, peer_sub_keys, peer_u, peer_v, norm_final):
    bsz, seq, d = x.shape
    depth = w_in.shape[0]
    lb_soft = jax.nn.softmax(hgrn_lower_bounds.astype(F32), axis=0)
    lower = jnp.cumsum(lb_soft, axis=0) - lb_soft[0]
    w_in_p = _permute_w_in(w_in).astype(BF16)
    streams = BATCH_STREAMS if bsz % BATCH_STREAMS == 0 else 1
    sb = bsz // streams
    xs = [x[s * sb:(s + 1) * sb].reshape(sb * seq, d) for s in range(streams)]
    for l in range(depth):
        w_cmp = jnp.stack([nsa_w_cmp_k[l], nsa_w_cmp_v[l]])
        w_sb, w_nsa, w_hg, w_o = (w_branch_sb[l].astype(BF16), w_branch_nsa[l].astype(BF16),
                                  w_branch_hgrn[l].astype(BF16), w_out[l].astype(BF16))
        w_q_t = peer_w_q[l].T.astype(BF16)
        u_words, v_words = _pack_table(peer_u[l]), _pack_table(peer_v[l])
        xs = [_mixer_layer(xh, sb, seq, norm_mix[l][None], w_in_p[l], w_cmp, nsa_cmp_pe[l],
                           hgrn_norm[l][None], lower[l][None], w_sb, w_nsa, w_hg, w_o) for xh in xs]
        xs = [_peer_layer(xh, norm_ffn[l][None], w_q_t, peer_sub_keys[l], u_words, v_words)
              for xh in xs]
    outs = [_final_norm(xh, norm_final[None]).reshape(sb, seq, d) for xh in xs]
    return jnp.concatenate(outs, axis=0)
```

```python
import functools

import jax
import jax.numpy as jnp
from jax import lax
from jax.experimental import pallas as pl
from jax.experimental.pallas import tpu as pltpu
from jax.experimental.pallas import tpu_sc as plsc

F32 = jnp.float32
BF16 = jnp.bfloat16

D_MODEL = 1024
HEAD_DIM = 64
EPS = 1e-6
NEG = -1e30
FORCE_SCORE = 1e4
Q_BLOCK = 128

SB_HEADS = 8
NSA_HEADS = 8
NSA_GROUPS = 2
NSA_HPG = NSA_HEADS // NSA_GROUPS
CMP_LEN = 32
CMP_STRIDE = 16
SEL_BLOCK = 64
SEL_TOPN = 4
WINDOW = 256
HGRN_HEADS = 4
HGRN_DK = 128
HGRN_CHUNK = 64
HGRN_SUB = 16
PEER_HEADS = 8
PEER_NKEYS = 128
PEER_TOPK = 16
PEER_QDIM = 128

C_MG = 0
C_SBQ = 3072
C_SBK = 3584
C_SBV = 4096
C_NQ = 4608
C_HQ = 5120
C_HF = 5632
C_HI = 6144
C_HG = 6656
C_NKV = 7168
C_NG = 7936
IN_WIDTH = 7960
IN_PAD = 8064
LANE = 128

VMEM_LIMIT = 56 * 1024 * 1024
ROW_WORDS = D_MODEL // 2
ROW_PARTS = ROW_WORDS // LANE
PEER_TOK = 32
PEER_ROWS = PEER_HEADS * PEER_TOPK
SEL_CHUNK = 512


def _cparams(sem):
    return pltpu.CompilerParams(dimension_semantics=sem, vmem_limit_bytes=VMEM_LIMIT)


def _dot(a, b):
    return jnp.dot(a, b, preferred_element_type=F32)


def _dot_nt(a, b):
    return lax.dot_general(a, b, (((1,), (1,)), ((), ())), preferred_element_type=F32)


def _dot_tn(a, b):
    return lax.dot_general(a, b, (((0,), (0,)), ((), ())), preferred_element_type=F32)


def _split2(x):
    hi = x.astype(BF16)
    lo = (x - hi.astype(F32)).astype(BF16)
    return hi, lo


def _sigmoid(x):
    return 1.0 / (1.0 + jnp.exp(-x))


def _inproj_kernel(x_ref, g_ref, w_ref, o_ref):
    x = x_ref[...]
    y = x * lax.rsqrt(jnp.mean(x * x, axis=-1, keepdims=True) + EPS) * g_ref[...]
    o_ref[...] = _dot(y.astype(BF16), w_ref[...])


def _inproj(x2, g, w):
    n = x2.shape[0]
    tm = 256
    return pl.pallas_call(
        _inproj_kernel,
        out_shape=jax.ShapeDtypeStruct((n, IN_PAD), F32),
        grid=(n // tm,),
        in_specs=[
            pl.BlockSpec((tm, D_MODEL), lambda i: (i, 0)),
            pl.BlockSpec((1, D_MODEL), lambda i: (0, 0)),
            pl.BlockSpec((D_MODEL, IN_PAD), lambda i: (0, 0)),
        ],
        out_specs=pl.BlockSpec((tm, IN_PAD), lambda i: (i, 0)),
        compiler_params=_cparams(("arbitrary",)),
        name="inproj",
        cost_estimate=pl.CostEstimate(
            flops=2 * n * D_MODEL * IN_PAD, transcendentals=n,
            bytes_accessed=4 * n * D_MODEL + 2 * D_MODEL * IN_PAD + 4 * n * IN_PAD),
    )(x2, g, w)


SB_QROWS = 512
SB_LANES = 256


def _sb_kernel(q_ref, k_ref, v_ref, o_ref):
    qi = pl.program_id(2)
    nq, nk = SB_QROWS, Q_BLOCK
    per = nq // nk
    scale = HEAD_DIM ** -0.5
    r = lax.broadcasted_iota(jnp.int32, (nq, nk), 0)
    c = lax.broadcasted_iota(jnp.int32, (nq, nk), 1)
    ur = lax.broadcasted_iota(jnp.int32, (nk, nk), 0)
    uc = lax.broadcasted_iota(jnp.int32, (nk, nk), 1)
    upper = (ur > uc).astype(BF16)
    heads = [slice(h * HEAD_DIM, (h + 1) * HEAD_DIM) for h in range(SB_LANES // HEAD_DIM)]
    qs = [(q_ref[0, :, sl] * scale).astype(BF16) for sl in heads]

    def step(qb, kb, vb, c_run, acc, before):
        z = _dot_nt(qb, kb)
        soft = jnp.log(1.0 + jnp.exp(-jnp.abs(z)))
        lsp = jnp.minimum(z, 0.0) - soft
        lsn = -jnp.maximum(z, 0.0) - soft
        if before is not None:
            lsn = jnp.where(before, lsn, 0.0)
        hi, lo = _split2(lsn)
        after = c_run + (_dot(hi, upper) + _dot(lo, upper))
        a = jnp.exp(lsp + after)
        if before is not None:
            a = jnp.where(before, a, 0.0)
        acc = acc + _dot(a.astype(BF16), vb)
        c_run = c_run + jnp.sum(lsn, axis=1, keepdims=True)
        return c_run, acc

    def block(j, carry, before):
        k0 = pl.multiple_of(j * nk, nk)
        kb = k_ref[0, pl.ds(k0, nk), :].astype(BF16)
        vb = v_ref[0, pl.ds(k0, nk), :].astype(BF16)
        out = []
        for h, sl in enumerate(heads):
            out.extend(step(qs[h], kb[:, sl], vb[:, sl], carry[2 * h], carry[2 * h + 1], before))
        return tuple(out)

    carry = tuple(jnp.zeros((nq, w), F32) for _ in heads for w in (1, HEAD_DIM))
    for d in range(per - 1, -1, -1):
        carry = block(qi * per + d, carry, c + d * nk < r)
    carry = lax.fori_loop(0, qi * per, lambda n, cr: block(qi * per - 1 - n, cr, None), carry)
    o_ref[0] = jnp.concatenate([carry[2 * h + 1] for h in range(len(heads))], axis=1).astype(BF16)


def _sb_attention(proj3):
    b, t, _ = proj3.shape
    qb, kb, vb = C_SBQ // SB_LANES, C_SBK // SB_LANES, C_SBV // SB_LANES
    return pl.pallas_call(
        _sb_kernel,
        out_shape=jax.ShapeDtypeStruct((b, t, SB_HEADS * HEAD_DIM), BF16),
        grid=(b, SB_HEADS * HEAD_DIM // SB_LANES, t // SB_QROWS),
        in_specs=[
            pl.BlockSpec((1, SB_QROWS, SB_LANES), lambda bi, hp, i: (bi, i, qb + hp)),
            pl.BlockSpec((1, t, SB_LANES), lambda bi, hp, i: (bi, 0, kb + hp)),
            pl.BlockSpec((1, t, SB_LANES), lambda bi, hp, i: (bi, 0, vb + hp)),
        ],
        out_specs=pl.BlockSpec((1, SB_QROWS, SB_LANES), lambda bi, hp, i: (bi, i, hp)),
        compiler_params=_cparams(("arbitrary", "arbitrary", "arbitrary")),
        name="sb_attn",
        cost_estimate=pl.CostEstimate(
            flops=b * SB_HEADS * t * t * (2 * HEAD_DIM + 2 * Q_BLOCK),
            transcendentals=b * SB_HEADS * t * t * 3 // 2,
            bytes_accessed=4 * 4 * b * t * SB_HEADS * HEAD_DIM),
    )(proj3, proj3, proj3)


HGRN_PAIR = 4


def _hgrn_kernel(q_ref, f_ref, i_ref, g_ref, lb_ref, nw_ref, o_ref, st_ref, *, n_chunks):
    ch, sub = HGRN_CHUNK, HGRN_SUB
    st_ref[...] = jnp.zeros_like(st_ref)
    r = lax.broadcasted_iota(jnp.int32, (ch, ch), 0)
    c = lax.broadcasted_iota(jnp.int32, (ch, ch), 1)
    lower = (r >= c).astype(BF16)
    srow = lax.broadcasted_iota(jnp.int32, (sub, HGRN_DK), 0)

    def head_chunk(t0, hh):
        hs = slice(hh * HGRN_DK, (hh + 1) * HGRN_DK)
        lb = lb_ref[:, hs]
        nw = nw_ref[:, hs]
        fz = f_ref[0, pl.ds(t0, ch), hs]
        qz = q_ref[0, pl.ds(t0, ch), hs]
        iv = i_ref[0, pl.ds(t0, ch), hs]
        gz = g_ref[0, pl.ds(t0, ch), hs]
        f = lb + (1.0 - lb) * _sigmoid(fz)
        lf = jnp.log(f)
        kk = 1.0 - f
        qh = qz * _sigmoid(qz)
        hi, lo = _split2(lf)
        bcum = _dot(lower, hi) + _dot(lower, lo)
        st = st_ref[hh]
        o = _dot_nt((qh * jnp.exp(bcum)).astype(BF16), st.astype(BF16))
        ivb = iv.astype(BF16)

        rows = []
        for s in range(ch // sub):
            lo_r, hi_r = s * sub, (s + 1) * sub
            qs, ks, bs, vs = qh[lo_r:hi_r], kk[lo_r:hi_r], bcum[lo_r:hi_r], iv[lo_r:hi_r]
            o_s = o[lo_r:hi_r]
            if s > 0:
                bref = bcum[lo_r - 1:lo_r]
                qd = (qs * jnp.exp(bs - bref)).astype(BF16)
                kd = (kk[:lo_r] * jnp.exp(bref - bcum[:lo_r])).astype(BF16)
                att = _dot_nt(qd, kd)
                o_s = o_s + _dot(att.astype(BF16), ivb[:lo_r])
            diag_rows = []
            for t in range(sub):
                dlt = jnp.where(srow <= t, bs[t:t + 1] - bs, NEG)
                w = (qs[t:t + 1] * ks) * jnp.exp(dlt)
                att_col = jnp.sum(w, axis=1, keepdims=True)
                diag_rows.append(jnp.sum(att_col * vs, axis=0, keepdims=True))
            rows.append(o_s + jnp.concatenate(diag_rows, axis=0))
        o = jnp.concatenate(rows, axis=0)

        b_last = bcum[ch - 1:ch]
        kd = (kk * jnp.exp(b_last - bcum)).astype(BF16)
        st_ref[hh] = st * jnp.exp(b_last) + _dot_tn(ivb, kd)

        y = o * lax.rsqrt(jnp.mean(o * o, axis=-1, keepdims=True) + EPS) * nw
        o_ref[0, pl.ds(t0, ch), hs] = (y * (gz * _sigmoid(gz))).astype(BF16)

    def chunk(ci, carry):
        t0 = pl.multiple_of(ci * ch, ch)
        for hh in range(HGRN_PAIR):
            head_chunk(t0, hh)
        return carry

    lax.fori_loop(0, n_chunks, chunk, 0)


def _hgrn(proj3, lower_l, norm_l):
    b, t, _ = proj3.shape
    wide = HGRN_PAIR * HGRN_DK
    cq, cf, ci, cg = C_HQ // wide, C_HF // wide, C_HI // wide, C_HG // wide
    seq = lambda col: pl.BlockSpec((1, t, wide), lambda bi, h: (bi, 0, col + h))
    vec = pl.BlockSpec((1, wide), lambda bi, h: (0, h))
    return pl.pallas_call(
        functools.partial(_hgrn_kernel, n_chunks=t // HGRN_CHUNK),
        out_shape=jax.ShapeDtypeStruct((b, t, HGRN_HEADS * HGRN_DK), BF16),
        grid=(b, HGRN_HEADS // HGRN_PAIR),
        in_specs=[seq(cq), seq(cf), seq(ci), seq(cg), vec, vec],
        out_specs=pl.BlockSpec((1, t, wide), lambda bi, h: (bi, 0, h)),
        scratch_shapes=[pltpu.VMEM((HGRN_PAIR, HGRN_DK, HGRN_DK), F32)],
        compiler_params=_cparams(("arbitrary", "arbitrary")),
        name="hgrn2",
        cost_estimate=pl.CostEstimate(
            flops=b * HGRN_HEADS * t * HGRN_DK * (6 * HGRN_DK + 6 * HGRN_CHUNK + 4 * HGRN_SUB),
            transcendentals=b * HGRN_HEADS * t * HGRN_DK * (8 + HGRN_SUB),
            bytes_accessed=4 * 5 * b * t * HGRN_HEADS * HGRN_DK),
    )(proj3, proj3, proj3, proj3, lower_l, norm_l)


def _cmp_kernel(x_ref, w_ref, pe_ref, o_ref):
    n_blk = o_ref.shape[3]
    w = w_ref[0].astype(BF16)
    pieces = [x_ref[0, pl.ds(r, n_blk, stride=CMP_STRIDE), :] for r in range(CMP_STRIDE)]
    for g in range(NSA_GROUPS):
        gs = slice(g * HEAD_DIM, (g + 1) * HEAD_DIM)
        first = jnp.zeros((n_blk, HEAD_DIM), F32)
        second = jnp.zeros((n_blk, HEAD_DIM), F32)
        for r in range(CMP_STRIDE):
            rows = pieces[r][:, gs]
            lo, hi = r, CMP_STRIDE + r
            first = first + _dot((rows + pe_ref[lo:lo + 1, :]).astype(BF16),
                                 w[lo * HEAD_DIM:(lo + 1) * HEAD_DIM])
            second = second + _dot((rows + pe_ref[hi:hi + 1, :]).astype(BF16),
                                   w[hi * HEAD_DIM:(hi + 1) * HEAD_DIM])
        o_ref[0, 0, g] = first + pltpu.roll(second, n_blk - 1, 0)


def _nsa_compress(proj3, w_cmp, pe):
    b, t, _ = proj3.shape
    n_blk = t // CMP_STRIDE
    kvb = C_NKV // LANE
    return pl.pallas_call(
        _cmp_kernel,
        out_shape=jax.ShapeDtypeStruct((b, 2, NSA_GROUPS, n_blk, HEAD_DIM), F32),
        grid=(b, 2),
        in_specs=[
            pl.BlockSpec((1, t, LANE), lambda bi, kv: (bi, 0, kvb + kv)),
            pl.BlockSpec((1, CMP_LEN * HEAD_DIM, HEAD_DIM), lambda bi, kv: (kv, 0, 0)),
            pl.BlockSpec((CMP_LEN, HEAD_DIM), lambda bi, kv: (0, 0)),
        ],
        out_specs=pl.BlockSpec((1, 1, NSA_GROUPS, n_blk, HEAD_DIM), lambda bi, kv: (bi, kv, 0, 0, 0)),
        compiler_params=_cparams(("arbitrary", "arbitrary")),
        name="nsa_compress",
    )(proj3, w_cmp, pe)


def _nsa_kernel(q_ref, cmp_ref, ks_ref, vs_ref, kw_ref, vw_ref, g_ref, o_ref, *, seq_len):
    i = pl.program_id(1)
    qn = Q_BLOCK
    scale = HEAD_DIM ** -0.5
    n_blk = seq_len // SEL_BLOCK
    t0 = i * qn
    trow = t0 + lax.broadcasted_iota(jnp.int32, (qn, 1), 0)
    lane = lax.broadcasted_iota(jnp.int32, (qn, LANE), 1)
    lane_f = lane.astype(F32)

    dist_c = trow - (lane * CMP_STRIDE + (CMP_LEN - 1))
    valid_c = dist_c >= 0
    dist_cf = dist_c.astype(F32)
    cr = lax.broadcasted_iota(jnp.int32, (LANE, LANE), 0) * CMP_STRIDE
    nb = lax.broadcasted_iota(jnp.int32, (LANE, LANE), 1)
    overlap = ((cr < nb * SEL_BLOCK + SEL_BLOCK) & (cr + CMP_LEN > nb * SEL_BLOCK)
               & (nb < n_blk)).astype(BF16)
    forced = (lane == trow // SEL_BLOCK) | (lane == 0)
    causal_b = lane * SEL_BLOCK <= trow

    gsig = _sigmoid(g_ref[0])
    trow4 = jnp.concatenate([trow] * NSA_HPG, axis=0)

    span = WINDOW + qn
    kstart = pl.multiple_of(jnp.maximum(i - WINDOW // qn, 0) * qn, qn)
    wpos = kstart + lax.broadcasted_iota(jnp.int32, (NSA_HPG * qn, span), 1)
    dist_w = trow4 - wpos
    valid_w = (dist_w >= 0) & (dist_w < WINDOW)
    dist_wf = dist_w.astype(F32)

    for g in range(NSA_GROUPS):
        gs = slice(g * HEAD_DIM, (g + 1) * HEAD_DIM)
        kc = cmp_ref[0, 0, g].astype(BF16)
        vc = cmp_ref[0, 1, g].astype(BF16)
        q_heads = [(q_ref[0, :, (g * NSA_HPG + p) * HEAD_DIM:(g * NSA_HPG + p + 1) * HEAD_DIM]
                    * scale).astype(BF16) for p in range(NSA_HPG)]
        slopes = [2.0 ** (-(g * NSA_HPG + p + 1)) for p in range(NSA_HPG)]
        slope_col = jnp.concatenate(
            [jnp.full((qn, 1), s, F32) for s in slopes], axis=0)

        psum = jnp.zeros((qn, LANE), F32)
        o_cmp = []
        for p in range(NSA_HPG):
            s = _dot_nt(q_heads[p], kc) - slopes[p] * dist_cf
            s = jnp.where(valid_c, s, NEG)
            m = jnp.max(s, axis=1, keepdims=True)
            e = jnp.where(valid_c, jnp.exp(s - m), 0.0)
            den = jnp.sum(e, axis=1, keepdims=True)
            pc = e / jnp.where(den > 0.0, den, 1.0)
            psum = psum + pc
            o_cmp.append(_dot(pc.astype(BF16), vc))
        hi, lo = _split2(psum)
        imp = _dot(hi, overlap) + _dot(lo, overlap)
        imp = jnp.where(forced, FORCE_SCORE, jnp.where(causal_b, imp, NEG))
        imp = jnp.where(lane < n_blk, imp, -jnp.inf)
        sel = jnp.zeros((qn, LANE), jnp.bool_)
        for _ in range(SEL_TOPN):
            mx = jnp.max(imp, axis=1, keepdims=True)
            idx = jnp.min(jnp.where(imp == mx, lane_f, float(LANE)), axis=1, keepdims=True)
            onehot = lane_f == idx
            sel = sel | onehot
            imp = jnp.where(onehot, -jnp.inf, imp)
        sel_b = jnp.where(sel, 1.0, 0.0).astype(BF16)

        q4 = jnp.concatenate(q_heads, axis=0)

        def sel_chunk(ci, carry):
            m_run, l_run, acc = carry
            k0 = pl.multiple_of(ci * SEL_CHUNK, SEL_CHUNK)
            kk = ks_ref[0, pl.ds(k0, SEL_CHUNK), gs].astype(BF16)
            vv = vs_ref[0, pl.ds(k0, SEL_CHUNK), gs].astype(BF16)
            er = lax.broadcasted_iota(jnp.int32, (LANE, SEL_CHUNK), 0)
            ec = lax.broadcasted_iota(jnp.int32, (LANE, SEL_CHUNK), 1)
            expand = (er == ci * (SEL_CHUNK // SEL_BLOCK) + ec // SEL_BLOCK).astype(BF16)
            kpos = k0 + lax.broadcasted_iota(jnp.int32, (qn, SEL_CHUNK), 1)
            dist = trow - kpos
            mask = (_dot(sel_b, expand) > 0.5) & (dist >= 0)
            mask4 = jnp.concatenate([mask] * NSA_HPG, axis=0)
            dist4 = jnp.concatenate([dist.astype(F32)] * NSA_HPG, axis=0)
            s = _dot_nt(q4, kk) - slope_col * dist4
            s = jnp.where(mask4, s, NEG)
            m_new = jnp.maximum(m_run, jnp.max(s, axis=1, keepdims=True))
            alpha = jnp.exp(m_run - m_new)
            pm = jnp.exp(s - m_new)
            l_new = alpha * l_run + jnp.sum(pm, axis=1, keepdims=True)
            acc = alpha * acc + _dot(pm.astype(BF16), vv)
            return m_new, l_new, acc

        n_sel_chunks = (t0 + qn + SEL_CHUNK - 1) // SEL_CHUNK
        init = (jnp.full((NSA_HPG * qn, 1), NEG, F32), jnp.zeros((NSA_HPG * qn, 1), F32),
                jnp.zeros((NSA_HPG * qn, HEAD_DIM), F32))
        _, l_sel, acc_sel = lax.fori_loop(0, n_sel_chunks, sel_chunk, init)
        o_sel = acc_sel / l_sel

        kw = kw_ref[0, pl.ds(kstart, span), gs].astype(BF16)
        vw = vw_ref[0, pl.ds(kstart, span), gs].astype(BF16)
        s = _dot_nt(q4, kw) - slope_col * dist_wf
        s = jnp.where(valid_w, s, NEG)
        m = jnp.max(s, axis=1, keepdims=True)
        e = jnp.exp(s - m)
        pw = e / jnp.sum(e, axis=1, keepdims=True)
        o_win = _dot(pw.astype(BF16), vw)

        outs = []
        for p in range(NSA_HPG):
            hh = g * NSA_HPG + p
            rows = slice(p * qn, (p + 1) * qn)
            outs.append(gsig[:, 3 * hh:3 * hh + 1] * o_cmp[p]
                        + gsig[:, 3 * hh + 1:3 * hh + 2] * o_sel[rows]
                        + gsig[:, 3 * hh + 2:3 * hh + 3] * o_win[rows])
        width = NSA_HPG * HEAD_DIM
        o_ref[0, :, g * width:(g + 1) * width] = jnp.concatenate(outs, axis=1).astype(BF16)


def _nsa_attention(proj3, cmp_kv):
    b, t, _ = proj3.shape
    kvb = C_NKV // LANE
    seq = lambda col: pl.BlockSpec((1, t, LANE), lambda bi, i: (bi, 0, col))
    n_piece = cmp_kv.shape[3]
    return pl.pallas_call(
        functools.partial(_nsa_kernel, seq_len=t),
        out_shape=jax.ShapeDtypeStruct((b, t, NSA_HEADS * HEAD_DIM), BF16),
        grid=(b, t // Q_BLOCK),
        in_specs=[
            pl.BlockSpec((1, Q_BLOCK, NSA_HEADS * HEAD_DIM), lambda bi, i: (bi, i, C_NQ // 512)),
            pl.BlockSpec((1, 2, NSA_GROUPS, n_piece, HEAD_DIM), lambda bi, i: (bi, 0, 0, 0, 0)),
            seq(kvb + 2), seq(kvb + 3), seq(kvb + 4), seq(kvb + 5),
            pl.BlockSpec((1, Q_BLOCK, LANE), lambda bi, i: (bi, i, C_NG // LANE)),
        ],
        out_specs=pl.BlockSpec((1, Q_BLOCK, NSA_HEADS * HEAD_DIM), lambda bi, i: (bi, i, 0)),
        compiler_params=_cparams(("arbitrary", "arbitrary")),
        name="nsa_attn",
        cost_estimate=pl.CostEstimate(
            flops=b * NSA_HEADS * t * 4 * HEAD_DIM * (t // 2 + WINDOW + Q_BLOCK + n_piece),
            transcendentals=b * NSA_HEADS * t * (t // 2 + WINDOW + Q_BLOCK + n_piece),
            bytes_accessed=4 * b * t * (2 * NSA_HEADS * HEAD_DIM + 5 * LANE)),
    )(proj3, cmp_kv, proj3, proj3, proj3, proj3, proj3)


def _merge_kernel(x_ref, osb_ref, onsa_ref, ohg_ref, gsb_ref, gnsa_ref, ghg_ref,
                  wsb_ref, wnsa_ref, whg_ref, wo_ref, o_ref):
    m = (_sigmoid(gsb_ref[...]) * _dot(osb_ref[...], wsb_ref[...])
         + _sigmoid(gnsa_ref[...]) * _dot(onsa_ref[...], wnsa_ref[...])
         + _sigmoid(ghg_ref[...]) * _dot(ohg_ref[...], whg_ref[...]))
    o_ref[...] = x_ref[...] + _dot(m.astype(BF16), wo_ref[...])


def _merge(x2, o_sb, o_nsa, o_hg, proj2, w_sb, w_nsa, w_hg, w_o):
    n = x2.shape[0]
    tm = 512
    row = lambda w: pl.BlockSpec((tm, w), lambda i: (i, 0))
    gate = lambda j: pl.BlockSpec((tm, D_MODEL), lambda i: (i, C_MG // D_MODEL + j))
    full = lambda a: pl.BlockSpec(a.shape, lambda i: (0, 0))
    return pl.pallas_call(
        _merge_kernel,
        out_shape=jax.ShapeDtypeStruct((n, D_MODEL), F32),
        grid=(n // tm,),
        in_specs=[row(D_MODEL), row(512), row(512), row(512), gate(0), gate(1), gate(2),
                  full(w_sb), full(w_nsa), full(w_hg), full(w_o)],
        out_specs=row(D_MODEL),
        compiler_params=_cparams(("arbitrary",)),
        name="merge_out",
        cost_estimate=pl.CostEstimate(
            flops=2 * n * D_MODEL * (3 * 512 + D_MODEL), transcendentals=3 * n * D_MODEL,
            bytes_accessed=4 * n * 5 * D_MODEL + 2 * n * 3 * 512 + 2 * D_MODEL * (3 * 512 + D_MODEL)),
    )(x2, o_sb, o_nsa, o_hg, proj2, proj2, proj2, w_sb, w_nsa, w_hg, w_o)


def _topk_rows(s, k, payload=None, order=None):
    rows, cols = s.shape
    if order is None:
        order = lax.broadcasted_iota(jnp.int32, (rows, cols), 0).astype(F32)
    out_row = lax.broadcasted_iota(jnp.int32, (k, cols), 0)
    vals = jnp.zeros((k, cols), F32)
    tags = jnp.zeros((k, cols), F32)
    for j in range(k):
        mx = jnp.max(s, axis=0, keepdims=True)
        idx = jnp.min(jnp.where(s == mx, order, jnp.inf), axis=0, keepdims=True)
        onehot = order == idx
        s = jnp.where(onehot, -jnp.inf, s)
        tag = idx if payload is None else jnp.sum(jnp.where(onehot, payload, 0.0), axis=0, keepdims=True)
        vals = jnp.where(out_row == j, mx, vals)
        tags = jnp.where(out_row == j, tag, tags)
    return vals, tags


def _candidate_pairs(k):
    return [(i, j) for i in range(k) for j in range(k) if (i + 1) * (j + 1) <= k]


def _route_kernel(x_ref, g_ref, wqt_ref, keys_ref, h_ref, idx_ref, gate_ref):
    k = PEER_TOPK
    half = PEER_QDIM // 2
    x = x_ref[...]
    h = (x * lax.rsqrt(jnp.mean(x * x, axis=-1, keepdims=True) + EPS) * g_ref[...]).astype(BF16)
    h_ref[...] = h
    qt = _dot_nt(wqt_ref[...], h).astype(BF16)
    gates, ids = [], []
    pairs = _candidate_pairs(k)
    n_pad = -len(pairs) % 8
    tokens = x.shape[0]
    pad_val = jnp.full((n_pad, tokens), -jnp.inf, F32)
    pad_idx = jnp.zeros((n_pad, tokens), F32)
    flat = jnp.concatenate(
        [jnp.full((1, tokens), float(i * k + j), F32) for i, j in pairs]
        + [jnp.full((n_pad, tokens), float(k * k), F32)], axis=0)
    for hd in range(PEER_HEADS):
        tops = []
        for a in range(2):
            r0 = (hd * 2 + a) * half
            s = _dot(keys_ref[a].astype(BF16), qt[r0:r0 + half])
            tops.append(_topk_rows(s, k))
        (s0, i0), (s1, i1) = tops
        cand = jnp.concatenate([s0[i:i + 1] + s1[j:j + 1] for i, j in pairs] + [pad_val], axis=0)
        cidx = jnp.concatenate([i0[i:i + 1] * float(PEER_NKEYS) + i1[j:j + 1] for i, j in pairs]
                               + [pad_idx], axis=0)
        best, eidx = _topk_rows(cand, k, payload=cidx, order=flat)
        e = jnp.exp(best - jnp.max(best, axis=0, keepdims=True))
        gates.append(e / jnp.sum(e, axis=0, keepdims=True))
        ids.append(eidx)
    gate_ref[...] = jnp.concatenate(gates, axis=0).T
    first_word = jnp.concatenate(ids, axis=0).T * float(ROW_PARTS)
    nt = PEER_TOK
    for t in range(x.shape[0] // nt):
        for j in range(ROW_PARTS):
            r0 = (t * ROW_PARTS + j) * nt
            idx_ref[r0:r0 + nt, :] = (first_word[t * nt:(t + 1) * nt] + float(j)).astype(jnp.int32)


def _peer_route(x2, g, w_q_t, sub_keys):
    n = x2.shape[0]
    tm = 128
    return pl.pallas_call(
        _route_kernel,
        out_shape=(jax.ShapeDtypeStruct((n, D_MODEL), BF16),
                   jax.ShapeDtypeStruct((n * ROW_PARTS, PEER_HEADS * PEER_TOPK), jnp.int32),
                   jax.ShapeDtypeStruct((n, PEER_HEADS * PEER_TOPK), F32)),
        grid=(n // tm,),
        in_specs=[pl.BlockSpec((tm, D_MODEL), lambda i: (i, 0)),
                  pl.BlockSpec((1, D_MODEL), lambda i: (0, 0)),
                  pl.BlockSpec(w_q_t.shape, lambda i: (0, 0)),
                  pl.BlockSpec(sub_keys.shape, lambda i: (0, 0, 0))],
        out_specs=(pl.BlockSpec((tm, D_MODEL), lambda i: (i, 0)),
                   pl.BlockSpec((tm * ROW_PARTS, PEER_HEADS * PEER_TOPK), lambda i: (i, 0)),
                   pl.BlockSpec((tm, PEER_HEADS * PEER_TOPK), lambda i: (i, 0))),
        compiler_params=_cparams(("arbitrary",)),
        name="peer_route",
        cost_estimate=pl.CostEstimate(
            flops=n * (2 * D_MODEL * D_MODEL + 4 * PEER_HEADS * PEER_QDIM * PEER_NKEYS
                       + 6 * PEER_HEADS * PEER_TOPK * (2 * PEER_NKEYS + PEER_TOPK * PEER_TOPK)),
            transcendentals=n * PEER_HEADS * PEER_TOPK,
            bytes_accessed=n * (6 * D_MODEL + 8 * PEER_HEADS * PEER_TOPK) + 2 * D_MODEL * D_MODEL),
    )(x2, g, w_q_t, sub_keys)


GATHER_WIN = 128


def _pack_table(tab):
    bits = lax.bitcast_convert_type(tab.astype(BF16), jnp.uint16).astype(jnp.uint32)
    words = (bits[:, ROW_WORDS:] << 16) | bits[:, :ROW_WORDS]
    return lax.bitcast_convert_type(words, jnp.int32).reshape(-1, LANE)


def _sc_gather(table, idx):
    m = idx.shape[0] * idx.shape[1]
    mesh = plsc.VectorSubcoreMesh(core_axis_name="core", subcore_axis_name="subcore")

    @pl.kernel(out_type=jax.ShapeDtypeStruct((m, LANE), table.dtype), mesh=mesh,
               cost_estimate=pl.CostEstimate(flops=0, transcendentals=0,
                                             bytes_accessed=m * (2 * LANE + 1) * 4))
    def gather(tab_hbm, idx_hbm, out_hbm):
        def body(idx_vmem, out_vmem):
            pltpu.sync_copy(tab_hbm.at[idx_vmem.at[0]], out_vmem)

        pltpu.emit_pipeline(
            body,
            grid=(m // GATHER_WIN,),
            in_specs=[pl.BlockSpec((1, GATHER_WIN), lambda i: (i, 0))],
            out_specs=[pl.BlockSpec((GATHER_WIN, LANE), lambda i: (i, 0))],
            core_axis_name=("core", "subcore"),
            dimension_semantics=(pltpu.PARALLEL,),
            trace_scopes=False,
        )(idx_hbm, out_hbm)

    return gather(table, idx)


HIGH_MASK = -65536


def _unpack(words):
    lo = lax.bitcast_convert_type(lax.shift_left(words, 16), F32)
    hi = lax.bitcast_convert_type(words & HIGH_MASK, F32)
    return lo, hi


def _expert_kernel(x_ref, h_ref, gate_ref, ug_ref, vg_ref, o_ref):
    nt, nr = PEER_TOK, PEER_ROWS
    h = h_ref[...].astype(F32)
    gate_t = jnp.concatenate([gate_ref[...], jnp.zeros((LANE - nt, nr), F32)], axis=0).T
    lane = lax.broadcasted_iota(jnp.int32, (nr, LANE), 1)
    hpre = jnp.zeros((nr, LANE), F32)
    for n in range(nt):
        s = jnp.zeros((nr, LANE), F32)
        for j in range(ROW_PARTS):
            lo, hi = _unpack(ug_ref[pl.ds((j * nt + n) * nr, nr), :])
            s = s + lo * h[n:n + 1, j * LANE:(j + 1) * LANE]
            s = s + hi * h[n:n + 1, ROW_WORDS + j * LANE:ROW_WORDS + (j + 1) * LANE]
        hpre = hpre + jnp.where(lane == n, jnp.sum(s, axis=1, keepdims=True), 0.0)
    act = gate_t * (0.5 * hpre * (1.0 + lax.erf(hpre * (2.0 ** -0.5))))
    rows = []
    for n in range(nt):
        a = act[:, n:n + 1]
        los, his = [], []
        for j in range(ROW_PARTS):
            lo, hi = _unpack(vg_ref[pl.ds((j * nt + n) * nr, nr), :])
            los.append(jnp.sum(a * lo, axis=0, keepdims=True))
            his.append(jnp.sum(a * hi, axis=0, keepdims=True))
        rows.append(jnp.concatenate(los + his, axis=1))
    o_ref[...] = x_ref[...] + jnp.concatenate(rows, axis=0)


def _peer_experts(x2, h2, gate, ug, vg):
    n = x2.shape[0]
    nt, nr = PEER_TOK, PEER_ROWS
    blk = nt * ROW_PARTS * nr
    return pl.pallas_call(
        _expert_kernel,
        out_shape=jax.ShapeDtypeStruct((n, D_MODEL), F32),
        grid=(n // nt,),
        in_specs=[pl.BlockSpec((nt, D_MODEL), lambda i: (i, 0)),
                  pl.BlockSpec((nt, D_MODEL), lambda i: (i, 0)),
                  pl.BlockSpec((nt, nr), lambda i: (i, 0)),
                  pl.BlockSpec((blk, LANE), lambda i: (i, 0)),
                  pl.BlockSpec((blk, LANE), lambda i: (i, 0))],
        out_specs=pl.BlockSpec((nt, D_MODEL), lambda i: (i, 0)),
        compiler_params=_cparams(("arbitrary",)),
        name="peer_experts",
        cost_estimate=pl.CostEstimate(
            flops=n * nr * D_MODEL * 6, transcendentals=n * nr,
            bytes_accessed=n * (2 * nr * ROW_WORDS * 4 + 10 * D_MODEL + 4 * nr)),
    )(x2, h2, gate, ug, vg)


def _norm_kernel(x_ref, g_ref, o_ref):
    x = x_ref[...]
    o_ref[...] = x * lax.rsqrt(jnp.mean(x * x, axis=-1, keepdims=True) + EPS) * g_ref[...]


def _final_norm(x2, g):
    n = x2.shape[0]
    tm = 1024
    return pl.pallas_call(
        _norm_kernel,
        out_shape=jax.ShapeDtypeStruct((n, D_MODEL), F32),
        grid=(n // tm,),
        in_specs=[pl.BlockSpec((tm, D_MODEL), lambda i: (i, 0)),
                  pl.BlockSpec((1, D_MODEL), lambda i: (0, 0))],
        out_specs=pl.BlockSpec((tm, D_MODEL), lambda i: (i, 0)),
        compiler_params=_cparams(("arbitrary",)),
        name="final_norm",
    )(x2, g)


def _permute_w_in(w_in):
    o = [0, 512, 1024, 1536, 2048, 2816, 2840, 3352, 3864, 4376, 4888, IN_WIDTH]
    sb_q, sb_k, sb_v, nsa_q, nsa_kv, nsa_g, hg_q, hg_f, hg_i, hg_g, merge_g = [
        w_in[..., o[j]:o[j + 1]] for j in range(11)]
    pad = jnp.zeros(w_in.shape[:-1] + (IN_PAD - IN_WIDTH,), w_in.dtype)
    return jnp.concatenate(
        [merge_g, sb_q, sb_k, sb_v, nsa_q, hg_q, hg_f, hg_i, hg_g, nsa_kv, nsa_g, pad], axis=-1)


def _mixer_layer(x2, bsz, seq, norm_g, w_in_p, w_cmp, pe, hgrn_norm_l, lower_l,
                 w_sb, w_nsa, w_hg, w_o):
    n = bsz * seq
    proj2 = _inproj(x2, norm_g, w_in_p)
    proj3 = proj2.reshape(bsz, seq, IN_PAD)
    o_sb = _sb_attention(proj3)
    o_hg = _hgrn(proj3, lower_l, hgrn_norm_l)
    cmp_kv = _nsa_compress(proj3, w_cmp, pe)
    o_nsa = _nsa_attention(proj3, cmp_kv)
    return _merge(x2, o_sb.reshape(n, -1), o_nsa.reshape(n, -1), o_hg.reshape(n, -1),
                  proj2, w_sb, w_nsa, w_hg, w_o)


def _peer_layer(x2, norm_g, w_q_t, sub_keys, u_words, v_words):
    h2, idx, gate = _peer_route(x2, norm_g, w_q_t, sub_keys)
    ug = _sc_gather(u_words, idx)
    vg = _sc_gather(v_words, idx)
    return _peer_experts(x2, h2, gate, ug, vg)


BATCH_STREAMS = 4


def kernel(x, norm_mix, norm_ffn, w_in, nsa_w_cmp_k, nsa_w_cmp_v, nsa_cmp_pe, hgrn_norm, hgrn_lower_bounds, w_branch_sb, w_branch_nsa, w_branch_hgrn, w_out, peer_w_q, peer_sub_keys, peer_u, peer_v, norm_final):
    bsz, seq, d = x.shape
    depth = w_in.shape[0]
    lb_soft = jax.nn.softmax(hgrn_lower_bounds.astype(F32), axis=0)
    lower = jnp.cumsum(lb_soft, axis=0) - lb_soft[0]
    w_in_p = _permute_w_in(w_in).astype(BF16)
    streams = BATCH_STREAMS if bsz % BATCH_STREAMS == 0 else 1
    sb = bsz // streams
    xs = [x[s * sb:(s + 1) * sb].reshape(sb * seq, d) for s in range(streams)]
    for l in range(depth):
        w_cmp = jnp.stack([nsa_w_cmp_k[l], nsa_w_cmp_v[l]])
        w_sb, w_nsa, w_hg, w_o = (w_branch_sb[l].astype(BF16), w_branch_nsa[l].astype(BF16),
                                  w_branch_hgrn[l].astype(BF16), w_out[l].astype(BF16))
        w_q_t = peer_w_q[l].T.astype(BF16)
        u_words, v_words = _pack_table(peer_u[l]), _pack_table(peer_v[l])
        xs = [_mixer_layer(xh, sb, seq, norm_mix[l][None], w_in_p[l], w_cmp, nsa_cmp_pe[l],
                           hgrn_norm[l][None], lower[l][None], w_sb, w_nsa, w_hg, w_o) for xh in xs]
        xs = [_peer_layer(xh, norm_ffn[l][None], w_q_t, peer_sub_keys[l], u_words, v_words)
              for xh in xs]
    outs = [_final_norm(xh, norm_final[None]).reshape(sb, seq, d) for xh in xs]
    return jnp.concatenate(outs, axis=0)
```
